```python
import jax, jax.numpy as jnp
from jax import lax
import numpy as np

D_MODEL = 2048
BATCH = 16
SEQ = 2048
DEPTH = 1
DEC_BATCH = 8
DEC_SEQ = 64
PAST_LEN = 1024

CHUNK = 64
RMS_EPS = 1e-6
ROPE_BASE = 10000.0
MIX_WIDTH = D_MODEL
RET_WIDTH = MIX_WIDTH // 2
RET_HEADS = 8
RET_DV = RET_WIDTH // RET_HEADS
RET_DK = RET_DV
MLSTM_WIDTH = MIX_WIDTH - RET_WIDTH
MLSTM_HEADS = 4
MLSTM_DV = MLSTM_WIDTH // MLSTM_HEADS
MLSTM_DK = MLSTM_DV // 2
IN_SIZES = (RET_HEADS * RET_DK, RET_HEADS * RET_DK, RET_WIDTH, RET_WIDTH,
            MLSTM_HEADS * MLSTM_DK, MLSTM_HEADS * MLSTM_DK, MLSTM_WIDTH, MLSTM_WIDTH,
            MLSTM_HEADS, MLSTM_HEADS)
IN_COLS = sum(IN_SIZES)
PEER_HEADS = 8
PEER_QDIM = 256
N_KEYS = 128
N_EXPERTS = N_KEYS * N_KEYS
PEER_TOPK = 16
EXPERT_BLOCK = 128
PLE_DIM = 256

kernel_name = 'hymba_retention_mlstm_peer_stream_step'


def _rmsnorm(x, g):
    x32 = x.astype(jnp.float32)
    y = x32 * lax.rsqrt(jnp.mean(x32 * x32, axis=-1, keepdims=True) + RMS_EPS)
    return (y * g.astype(jnp.float32)).astype(x.dtype)


def _head_norm(o, g):
    n_heads, d = o.shape[1], o.shape[3]
    y = o * lax.rsqrt(jnp.mean(o * o, axis=-1, keepdims=True) + RMS_EPS)
    return y * g.astype(jnp.float32).reshape(1, n_heads, 1, d)


def _rope(x, pos):
    half = x.shape[-1] // 2
    inv = ROPE_BASE ** (-jnp.arange(half, dtype=jnp.float32) / half)
    ang = pos[:, None] * inv[None, :]
    cos, sin = jnp.cos(ang), jnp.sin(ang)
    x1, x2 = x[..., :half], x[..., half:]
    return jnp.concatenate([x1 * cos - x2 * sin, x1 * sin + x2 * cos], axis=-1)


def _split_heads(t, n_heads):
    b, l, _ = t.shape
    return t.reshape(b, l, n_heads, -1).transpose(0, 2, 1, 3)


def _to_chunks(t, cl):
    b, h, l = t.shape[:3]
    t = t.reshape((b, h, l // cl, cl) + t.shape[3:])
    return jnp.moveaxis(t, 2, 0)


def _from_chunks(t):
    nc, b, h, cl = t.shape[:4]
    t = jnp.moveaxis(t, 0, 2)
    return t.reshape((b, h, nc * cl) + t.shape[4:])


def _in_offsets():
    return [int(o) for o in np.cumsum(np.array(IN_SIZES))[:-1]]


def _retention_chunk(S, q, k, v, log_gamma):
    L = q.shape[2]
    pos = jnp.arange(L, dtype=jnp.float32)
    diff = pos[:, None] - pos[None, :]
    causal = diff >= 0
    decay = jnp.where(causal[None], jnp.exp(jnp.where(causal, diff, 0.0)[None] * log_gamma[:, None, None]), 0.0)
    scores = jnp.einsum('bhld,bhmd->bhlm', q, k) * decay[None]
    q_dec = q * jnp.exp((pos[None, :] + 1.0) * log_gamma[:, None])[None, :, :, None]
    k_dec = k * jnp.exp((L - 1.0 - pos[None, :]) * log_gamma[:, None])[None, :, :, None]
    o = jnp.einsum('bhlm,bhmv->bhlv', scores, v) + jnp.einsum('bhld,bhdv->bhlv', q_dec, S)
    S_new = jnp.exp(L * log_gamma)[None, :, None, None] * S + jnp.einsum('bhmd,bhmv->bhdv', k_dec, v)
    return o, S_new


def _mlstm_chunk(C, n, m, q, k, v, ig, fg):
    L = q.shape[2]
    b = jnp.cumsum(jax.nn.log_sigmoid(fg), axis=-1)
    causal = jnp.tril(jnp.ones((L, L), dtype=bool))
    dlog = jnp.where(causal, b[..., :, None] - b[..., None, :] + ig[..., None, :], -jnp.inf)
    inter_log = b + m[..., None]
    m_t = jnp.maximum(inter_log, jnp.max(dlog, axis=-1))
    dw = jnp.exp(dlog - m_t[..., None])
    inter_w = jnp.exp(inter_log - m_t)
    sm = jnp.einsum('bhld,bhsd->bhls', q, k) * dw
    num = jnp.einsum('bhls,bhsv->bhlv', sm, v) + inter_w[..., None] * jnp.einsum('bhld,bhdv->bhlv', q, C)
    den = jnp.sum(sm, axis=-1) + inter_w * jnp.einsum('bhld,bhd->bhl', q, n)
    h = num / jnp.maximum(jnp.abs(den), jnp.exp(-m_t))[..., None]
    m_new = m_t[..., -1]
    ws = jnp.exp(b[..., -1:] - b + ig - m_new[..., None])
    carry_decay = jnp.exp(b[..., -1] + m - m_new)
    C_new = carry_decay[..., None, None] * C + jnp.einsum('bhs,bhsd,bhsv->bhdv', ws, k, v)
    n_new = carry_decay[..., None] * n + jnp.einsum('bhs,bhsd->bhd', ws, k)
    return h, C_new, n_new, m_new


def _token_mixers(a, pos, S0, C0, n0, m0, w_in, b_gates, g_ret, g_ml, w_out):
    f32 = jnp.float32
    B, L, _ = a.shape
    z = a @ w_in
    rq, rk, rv, rg, mq, mk, mv, mo, mi, mf = jnp.split(z, _in_offsets(), axis=-1)
    rq = _rope(_split_heads(rq, RET_HEADS).astype(f32), pos)
    rk = _rope(_split_heads(rk, RET_HEADS).astype(f32), pos) * (RET_DK ** -0.5)
    rv = _split_heads(rv, RET_HEADS).astype(f32)
    mq = _split_heads(mq, MLSTM_HEADS).astype(f32)
    mk = _split_heads(mk, MLSTM_HEADS).astype(f32) * (MLSTM_DK ** -0.5)
    mv = _split_heads(mv, MLSTM_HEADS).astype(f32)
    gates = (jnp.concatenate([mi, mf], axis=-1).astype(f32) + b_gates.astype(f32)).transpose(0, 2, 1)
    ig, fg = gates[:, :MLSTM_HEADS], gates[:, MLSTM_HEADS:]
    cl = min(CHUNK, L)
    xs = tuple(_to_chunks(t, cl) for t in (rq, rk, rv, mq, mk, mv, ig, fg))
    log_gamma = jnp.log(1.0 - jnp.power(2.0, -5.0 - jnp.arange(RET_HEADS, dtype=f32)))

    def body(carry, xc):
        S, C, n, m = carry
        qr, kr, vr, qm, km, vm, igc, fgc = xc
        o_r, S = _retention_chunk(S, qr, kr, vr, log_gamma)
        o_m, C, n, m = _mlstm_chunk(C, n, m, qm, km, vm, igc, fgc)
        return (S, C, n, m), (o_r, o_m)

    carry0 = (S0.astype(f32), C0.astype(f32), n0.astype(f32), m0.astype(f32))
    (S, C, n, m), (o_r, o_m) = lax.scan(body, carry0, xs)
    o_r = _from_chunks(o_r)
    o_m = _from_chunks(o_m)
    ret = _head_norm(o_r, g_ret).transpose(0, 2, 1, 3).reshape(B, L, RET_WIDTH) * jax.nn.silu(rg.astype(f32))
    mls = _head_norm(o_m, g_ml).transpose(0, 2, 1, 3).reshape(B, L, MLSTM_WIDTH) * jax.nn.sigmoid(mo.astype(f32))
    mix = jnp.concatenate([ret, mls], axis=-1).astype(a.dtype) @ w_out
    return mix, S, C, n, m


def _peer(c, w_q, keys1, keys2, u, v):
    f32 = jnp.float32
    B, L, D = c.shape
    T = B * L
    ct = c.reshape(T, D)
    q = (ct @ w_q).reshape(T, PEER_HEADS, PEER_QDIM).astype(f32)
    half = PEER_QDIM // 2
    s1 = jnp.einsum('thd,kd->thk', q[..., :half], keys1.astype(f32))
    s2 = jnp.einsum('thd,kd->thk', q[..., half:], keys2.astype(f32))
    t1, i1 = lax.top_k(s1, PEER_TOPK)
    t2, i2 = lax.top_k(s2, PEER_TOPK)
    cand = (t1[..., :, None] + t2[..., None, :]).reshape(T, PEER_HEADS, PEER_TOPK * PEER_TOPK)
    cidx = (i1[..., :, None] * N_KEYS + i2[..., None, :]).reshape(T, PEER_HEADS, PEER_TOPK * PEER_TOPK)
    sc, sel = lax.top_k(cand, PEER_TOPK)
    eidx = jnp.take_along_axis(cidx, sel, axis=-1).reshape(T, PEER_HEADS * PEER_TOPK)
    gate = jax.nn.softmax(sc, axis=-1).reshape(T, PEER_HEADS * PEER_TOPK).astype(c.dtype)
    n_blk = -(-T // EXPERT_BLOCK)
    pad = n_blk * EXPERT_BLOCK - T
    cb = jnp.pad(ct, ((0, pad), (0, 0))).reshape(n_blk, EXPERT_BLOCK, D)
    ib = jnp.pad(eidx, ((0, pad), (0, 0))).reshape(n_blk, EXPERT_BLOCK, PEER_HEADS * PEER_TOPK)
    gb = jnp.pad(gate, ((0, pad), (0, 0))).reshape(n_blk, EXPERT_BLOCK, PEER_HEADS * PEER_TOPK)

    def expert_block(args):
        xb, idx, g = args
        act = jax.nn.gelu(jnp.einsum('td,tkd->tk', xb, u[idx]), approximate=False)
        return jnp.einsum('tk,tkd->td', g * act, v[idx])

    out = lax.map(expert_block, (cb, ib, gb)).reshape(n_blk * EXPERT_BLOCK, D)[:T]
    return out.reshape(B, L, D)


def _trunk(x, p, pos, st_ret, st_C, st_n, st_m, g_mix, w_in, b_gates, g_ret, g_mlstm, w_out, g_ffn,
           w_peer_q, peer_keys1, peer_keys2, peer_u, peer_v, g_ple, w_ple_gate, w_ple_proj, g_final):
    h = x
    Ss, Cs, ns, ms = [], [], [], []
    for i in range(DEPTH):
        mix, S, C, n, m = _token_mixers(_rmsnorm(h, g_mix[i]), pos, st_ret[i], st_C[i], st_n[i], st_m[i],
                                        w_in[i], b_gates[i], g_ret[i], g_mlstm[i], w_out[i])
        h = h + mix
        h = h + _peer(_rmsnorm(h, g_ffn[i]), w_peer_q[i], peer_keys1[i], peer_keys2[i], peer_u[i], peer_v[i])
        e = _rmsnorm(h, g_ple[i])
        h = h + jax.nn.sigmoid(e @ w_ple_gate[i]) * (p[i].astype(h.dtype) @ w_ple_proj[i])
        Ss.append(S)
        Cs.append(C)
        ns.append(n)
        ms.append(m)
    y = _rmsnorm(h, g_final)
    return y, jnp.stack(Ss), jnp.stack(Cs), jnp.stack(ns), jnp.stack(ms)


def setup_inputs(seed: int = 0) -> dict:
    key = jax.random.key(seed)
    ks = jax.random.split(key, 32)

    def nrm(k, shape, scale):
        return scale * jax.random.normal(k, shape, jnp.float32)

    b_gates = jnp.concatenate([
        nrm(ks[10], (DEPTH, MLSTM_HEADS), 0.1),
        jnp.linspace(3.0, 6.0, MLSTM_HEADS, dtype=jnp.float32)[None, :] + nrm(ks[11], (DEPTH, MLSTM_HEADS), 0.1)],
        axis=-1)
    return {
        'x_prompt': nrm(ks[0], (BATCH, SEQ, D_MODEL), 1.0),
        'x_sample': nrm(ks[1], (DEC_BATCH, DEC_SEQ, D_MODEL), 1.0),
        'p_prompt': nrm(ks[2], (DEPTH, BATCH, SEQ, PLE_DIM), 1.0),
        'p_sample': nrm(ks[3], (DEPTH, DEC_BATCH, DEC_SEQ, PLE_DIM), 1.0),
        'state_ret': nrm(ks[4], (DEPTH, DEC_BATCH, RET_HEADS, RET_DK, RET_DV), 1.0),
        'state_mlstm_C': nrm(ks[5], (DEPTH, DEC_BATCH, MLSTM_HEADS, MLSTM_DK, MLSTM_DV), 0.5),
        'state_mlstm_n': nrm(ks[6], (DEPTH, DEC_BATCH, MLSTM_HEADS, MLSTM_DK), 0.5),
        'state_mlstm_m': nrm(ks[7], (DEPTH, DEC_BATCH, MLSTM_HEADS), 1.0),
        'g_mix': 1.0 + nrm(ks[8], (DEPTH, D_MODEL), 0.05),
        'w_in': nrm(ks[9], (DEPTH, D_MODEL, IN_COLS), D_MODEL ** -0.5),
        'b_gates': b_gates,
        'g_ret': 1.0 + nrm(ks[12], (DEPTH, RET_WIDTH), 0.05),
        'g_mlstm': 1.0 + nrm(ks[13], (DEPTH, MLSTM_WIDTH), 0.05),
        'w_out': nrm(ks[14], (DEPTH, MIX_WIDTH, D_MODEL), MIX_WIDTH ** -0.5),
        'g_ffn': 1.0 + nrm(ks[15], (DEPTH, D_MODEL), 0.05),
        'w_peer_q': nrm(ks[16], (DEPTH, D_MODEL, PEER_HEADS * PEER_QDIM), D_MODEL ** -0.5),
        'peer_keys1': nrm(ks[17], (DEPTH, N_KEYS, PEER_QDIM // 2), (PEER_QDIM // 2) ** -0.5),
        'peer_keys2': nrm(ks[18], (DEPTH, N_KEYS, PEER_QDIM // 2), (PEER_QDIM // 2) ** -0.5),
        'peer_u': nrm(ks[19], (DEPTH, N_EXPERTS, D_MODEL), D_MODEL ** -0.5),
        'peer_v': nrm(ks[20], (DEPTH, N_EXPERTS, D_MODEL), 0.5),
        'g_ple': 1.0 + nrm(ks[21], (DEPTH, D_MODEL), 0.05),
        'w_ple_gate': nrm(ks[22], (DEPTH, D_MODEL, D_MODEL), D_MODEL ** -0.5),
        'w_ple_proj': nrm(ks[23], (DEPTH, PLE_DIM, D_MODEL), PLE_DIM ** -0.5),
        'g_final': 1.0 + nrm(ks[24], (D_MODEL,), 0.05),
    }


def reference(x_prompt, x_sample, p_prompt, p_sample, state_ret, state_mlstm_C, state_mlstm_n, state_mlstm_m,
              g_mix, w_in, b_gates, g_ret, g_mlstm, w_out, g_ffn, w_peer_q, peer_keys1, peer_keys2,
              peer_u, peer_v, g_ple, w_ple_gate, w_ple_proj, g_final):
    f32 = jnp.float32
    bp = x_prompt.shape[0]
    z_ret = jnp.zeros((DEPTH, bp, RET_HEADS, RET_DK, RET_DV), f32)
    z_C = jnp.zeros((DEPTH, bp, MLSTM_HEADS, MLSTM_DK, MLSTM_DV), f32)
    z_n = jnp.zeros((DEPTH, bp, MLSTM_HEADS, MLSTM_DK), f32)
    z_m = jnp.zeros((DEPTH, bp, MLSTM_HEADS), f32)
    pos_prompt = jnp.arange(x_prompt.shape[1], dtype=f32)
    pos_sample = PAST_LEN + jnp.arange(x_sample.shape[1], dtype=f32)
    y_prompt, ret_p, C_p, n_p, m_p = _trunk(
        x_prompt, p_prompt, pos_prompt, z_ret, z_C, z_n, z_m, g_mix, w_in, b_gates, g_ret, g_mlstm, w_out,
        g_ffn, w_peer_q, peer_keys1, peer_keys2, peer_u, peer_v, g_ple, w_ple_gate, w_ple_proj, g_final)
    y_sample, ret_s, C_s, n_s, m_s = _trunk(
        x_sample, p_sample, pos_sample, state_ret, state_mlstm_C, state_mlstm_n, state_mlstm_m, g_mix, w_in,
        b_gates, g_ret, g_mlstm, w_out, g_ffn, w_peer_q, peer_keys1, peer_keys2, peer_u, peer_v, g_ple,
        w_ple_gate, w_ple_proj, g_final)
    return (y_prompt, y_sample, ret_p, C_p, n_p, m_p, ret_s, C_s, n_s, m_s)
```

```python
import functools
import math

import numpy as np
import jax
import jax.numpy as jnp
from jax import lax
from jax.experimental import pallas as pl
from jax.experimental.pallas import tpu as pltpu

F32 = jnp.float32
BF16 = jnp.bfloat16

D_MODEL = 2048
CHUNK = 64
RMS_EPS = 1e-6
ROPE_BASE = 10000.0
RET_HEADS = 8
RET_D = 128
MLSTM_HEADS = 4
MLSTM_DK = 128
MLSTM_DV = 256
Z_COLS = 7168
OFF_RQ, OFF_RK, OFF_RV, OFF_RG = 0, 1024, 2048, 3072
OFF_MQ, OFF_MK, OFF_MV, OFF_MO = 4096, 4608, 5120, 6144
PEER_HEADS = 8
N_KEYS = 128
PEER_TOPK = 16
PEER_K = PEER_HEADS * PEER_TOPK
PLE_DIM = 256
LANES = 128

LOG_GAMMA = [float(np.log(np.float32(1.0) - np.float32(2.0) ** np.float32(-5.0 - h))) for h in range(RET_HEADS)]

VMEM_LIMIT = 56 * 1024 * 1024


def _params(sem):
    return pltpu.CompilerParams(dimension_semantics=sem, vmem_limit_bytes=VMEM_LIMIT)


def _rms(x, g):
    return x * lax.rsqrt(jnp.mean(x * x, axis=-1, keepdims=True) + RMS_EPS) * g


def _mm(a, b):
    return jnp.dot(a.astype(BF16), b.astype(BF16), preferred_element_type=F32)


def _mm_nt(a, b):
    return lax.dot_general(a.astype(BF16), b.astype(BF16), (((1,), (1,)), ((), ())), preferred_element_type=F32)


def _mm_tn(a, b):
    return lax.dot_general(a.astype(BF16), b.astype(BF16), (((0,), (0,)), ((), ())), preferred_element_type=F32)


def _sigmoid(x):
    return 1.0 / (1.0 + jnp.exp(-x))


def _in_proj_kernel(x_ref, g_ref, w_ref, wg_ref, z_ref, gz_ref, a_scr):
    @pl.when(pl.program_id(1) == 0)
    def _():
        a_scr[...] = _rms(x_ref[...], g_ref[...]).astype(BF16)
        gz_ref[...] = jnp.dot(a_scr[...], wg_ref[...], preferred_element_type=F32)

    z_ref[...] = jnp.dot(a_scr[...], w_ref[...], preferred_element_type=F32)


def _in_proj(x, g, w, wg, tm=512, tn=1024):
    t = x.shape[0]
    return pl.pallas_call(
        _in_proj_kernel,
        grid=(t // tm, Z_COLS // tn),
        in_specs=[
            pl.BlockSpec((tm, D_MODEL), lambda i, j: (i, 0)),
            pl.BlockSpec((1, D_MODEL), lambda i, j: (0, 0)),
            pl.BlockSpec((D_MODEL, tn), lambda i, j: (0, j)),
            pl.BlockSpec((D_MODEL, LANES), lambda i, j: (0, 0)),
        ],
        out_specs=[
            pl.BlockSpec((tm, tn), lambda i, j: (i, j)),
            pl.BlockSpec((tm, LANES), lambda i, j: (i, 0)),
        ],
        out_shape=[jax.ShapeDtypeStruct((t, Z_COLS), F32), jax.ShapeDtypeStruct((t, LANES), F32)],
        scratch_shapes=[pltpu.VMEM((tm, D_MODEL), BF16)],
        compiler_params=_params(("arbitrary", "arbitrary")),
        name="in_proj",
    )(x, g, w, wg)


def _log_sigmoid(x):
    return -(jnp.maximum(-x, 0.0) + jnp.log1p(jnp.exp(-jnp.abs(x))))


def _mixers_kernel(z_ref, gz_ref, bias_ref, cos_ref, sin_ref, s0_ref, c0_ref, n0_ref, m0_ref, gret_ref, gml_ref,
                   mix_ref, s_ref, c_ref, n_ref, m_ref):
    @pl.when(pl.program_id(1) == 0)
    def _():
        s_ref[...] = s0_ref[...]
        c_ref[...] = c0_ref[...]
        n_ref[...] = n0_ref[...]
        m_ref[...] = m0_ref[...]

    cl = CHUNK
    row = lax.broadcasted_iota(jnp.int32, (cl, cl), 0)
    col = lax.broadcasted_iota(jnp.int32, (cl, cl), 1)
    causal = row >= col
    diff = jnp.where(causal, (row - col).astype(F32), 0.0)
    posc = lax.broadcasted_iota(jnp.int32, (cl, 1), 0).astype(F32)
    cosf = cos_ref[...]
    sinf = sin_ref[...]

    def rope(x):
        return x * cosf + pltpu.roll(x, RET_D // 2, axis=1) * sinf

    for h in range(RET_HEADS):
        lg = LOG_GAMMA[h]
        lo = h * RET_D
        q = rope(z_ref[0, :, OFF_RQ + lo:OFF_RQ + lo + RET_D])
        k = rope(z_ref[0, :, OFF_RK + lo:OFF_RK + lo + RET_D]) * (RET_D ** -0.5)
        v = z_ref[0, :, OFF_RV + lo:OFF_RV + lo + RET_D]
        rg = z_ref[0, :, OFF_RG + lo:OFF_RG + lo + RET_D]
        decay = jnp.where(causal, jnp.exp(diff * lg), 0.0)
        s_old = s_ref[0, h]
        scores = _mm_nt(q, k) * decay
        q_dec = q * jnp.exp((posc + 1.0) * lg)
        k_dec = k * jnp.exp((cl - 1.0 - posc) * lg)
        o = _mm(scores, v) + _mm(q_dec, s_old)
        s_ref[0, h] = math.exp(cl * lg) * s_old + _mm_tn(k_dec, v)
        y = o * lax.rsqrt(jnp.mean(o * o, axis=-1, keepdims=True) + RMS_EPS) * gret_ref[:, lo:lo + RET_D]
        mix_ref[0, :, lo:lo + RET_D] = (y * (rg * _sigmoid(rg))).astype(BF16)

    gates = gz_ref[0] + bias_ref[...]
    tri = causal.astype(F32)
    bcum = jnp.dot(tri, _log_sigmoid(gates), preferred_element_type=F32, precision=lax.Precision.HIGHEST)
    bcum_t = bcum.T
    gates_t = gates.T
    for h in range(MLSTM_HEADS):
        q = z_ref[0, :, OFF_MQ + h * MLSTM_DK:OFF_MQ + (h + 1) * MLSTM_DK]
        k = z_ref[0, :, OFF_MK + h * MLSTM_DK:OFF_MK + (h + 1) * MLSTM_DK] * (MLSTM_DK ** -0.5)
        v = z_ref[0, :, OFF_MV + h * MLSTM_DV:OFF_MV + (h + 1) * MLSTM_DV]
        mo = z_ref[0, :, OFF_MO + h * MLSTM_DV:OFF_MO + (h + 1) * MLSTM_DV]
        f = MLSTM_HEADS + h
        b_col = bcum[:, f:f + 1]
        b_row = bcum_t[f:f + 1, :]
        ig_col = gates[:, h:h + 1]
        ig_row = gates_t[h:h + 1, :]
        m_prev = m_ref[0, h:h + 1, 0:1]
        dlog = jnp.where(causal, b_col - b_row + ig_row, -jnp.inf)
        inter_log = b_col + m_prev
        m_t = jnp.maximum(inter_log, jnp.max(dlog, axis=-1, keepdims=True))
        dw = jnp.exp(dlog - m_t)
        inter_w = jnp.exp(inter_log - m_t)
        c_old = c_ref[0, h]
        n_old = n_ref[0, h:h + 1, :]
        sm = _mm_nt(q, k) * dw
        num = _mm(sm, v) + inter_w * _mm(q, c_old)
        den = jnp.sum(sm, axis=-1, keepdims=True) + inter_w * jnp.sum(q * n_old, axis=-1, keepdims=True)
        hh = num / jnp.maximum(jnp.abs(den), jnp.exp(-m_t))
        m_new = m_t[cl - 1:cl, :]
        b_last = b_col[cl - 1:cl, :]
        ws = jnp.exp(b_last - b_col + ig_col - m_new)
        carry = jnp.exp(b_last + m_prev - m_new)
        kw = k * ws
        c_ref[0, h] = carry * c_old + _mm_tn(kw, v)
        n_ref[0, h:h + 1, :] = carry * n_old + jnp.sum(kw, axis=0, keepdims=True)
        m_ref[0, h:h + 1, :] = jnp.broadcast_to(m_new, (1, LANES))
        y = hh * lax.rsqrt(jnp.mean(hh * hh, axis=-1, keepdims=True) + RMS_EPS) * gml_ref[:, h * MLSTM_DV:(h + 1) * MLSTM_DV]
        lo = RET_HEADS * RET_D + h * MLSTM_DV
        mix_ref[0, :, lo:lo + MLSTM_DV] = (y * _sigmoid(mo)).astype(BF16)


def _mixers(z, gz, bias, cosf, sinf, s0, c0, n0, m0, g_ret, g_ml):
    b, l, _ = z.shape
    per_b = lambda *tail: (lambda i, c: (i,) + tail)
    return pl.pallas_call(
        _mixers_kernel,
        grid=(b, l // CHUNK),
        in_specs=[
            pl.BlockSpec((1, CHUNK, Z_COLS), lambda i, c: (i, c, 0)),
            pl.BlockSpec((1, CHUNK, LANES), lambda i, c: (i, c, 0)),
            pl.BlockSpec((1, LANES), lambda i, c: (0, 0)),
            pl.BlockSpec((CHUNK, RET_D), lambda i, c: (c, 0)),
            pl.BlockSpec((CHUNK, RET_D), lambda i, c: (c, 0)),
            pl.BlockSpec((1, RET_HEADS, RET_D, RET_D), per_b(0, 0, 0)),
            pl.BlockSpec((1, MLSTM_HEADS, MLSTM_DK, MLSTM_DV), per_b(0, 0, 0)),
            pl.BlockSpec((1, MLSTM_HEADS, MLSTM_DK), per_b(0, 0)),
            pl.BlockSpec((1, MLSTM_HEADS, LANES), per_b(0, 0)),
            pl.BlockSpec((1, RET_HEADS * RET_D), lambda i, c: (0, 0)),
            pl.BlockSpec((1, MLSTM_HEADS * MLSTM_DV), lambda i, c: (0, 0)),
        ],
        out_specs=[
            pl.BlockSpec((1, CHUNK, D_MODEL), lambda i, c: (i, c, 0)),
            pl.BlockSpec((1, RET_HEADS, RET_D, RET_D), per_b(0, 0, 0)),
            pl.BlockSpec((1, MLSTM_HEADS, MLSTM_DK, MLSTM_DV), per_b(0, 0, 0)),
            pl.BlockSpec((1, MLSTM_HEADS, MLSTM_DK), per_b(0, 0)),
            pl.BlockSpec((1, MLSTM_HEADS, LANES), per_b(0, 0)),
        ],
        out_shape=[
            jax.ShapeDtypeStruct((b, l, D_MODEL), BF16),
            jax.ShapeDtypeStruct((b, RET_HEADS, RET_D, RET_D), F32),
            jax.ShapeDtypeStruct((b, MLSTM_HEADS, MLSTM_DK, MLSTM_DV), F32),
            jax.ShapeDtypeStruct((b, MLSTM_HEADS, MLSTM_DK), F32),
            jax.ShapeDtypeStruct((b, MLSTM_HEADS, LANES), F32),
        ],
        compiler_params=_params(("arbitrary", "arbitrary")),
        name="mixers",
    )(z, gz, bias, cosf, sinf, s0, c0, n0, m0, g_ret, g_ml)


def _out_proj_kernel(a_ref, w_ref, r_ref, o_ref):
    o_ref[...] = r_ref[...] + jnp.dot(a_ref[...], w_ref[...], preferred_element_type=F32)


def _out_proj(a, w, r, tm=512):
    t = a.shape[0]
    return pl.pallas_call(
        _out_proj_kernel,
        grid=(t // tm,),
        in_specs=[
            pl.BlockSpec((tm, D_MODEL), lambda i: (i, 0)),
            pl.BlockSpec((D_MODEL, D_MODEL), lambda i: (0, 0)),
            pl.BlockSpec((tm, D_MODEL), lambda i: (i, 0)),
        ],
        out_specs=pl.BlockSpec((tm, D_MODEL), lambda i: (i, 0)),
        out_shape=jax.ShapeDtypeStruct((t, D_MODEL), F32),
        compiler_params=_params(("arbitrary",)),
        name="out_proj",
    )(a, w, r)


def _top16(x_ref, payload_ref, val_ref, idx_ref):
    x = x_ref[...]
    rows = x.shape[0]
    iota = lax.broadcasted_iota(jnp.int32, x.shape, 0)
    for r in range(PEER_TOPK):
        m = jnp.max(x, axis=0, keepdims=True)
        am = jnp.min(jnp.where(x == m, iota, rows), axis=0, keepdims=True)
        sel = iota == am
        val_ref[r:r + 1, :] = m
        if payload_ref is None:
            idx_ref[r:r + 1, :] = am
        else:
            idx_ref[r:r + 1, :] = jnp.max(jnp.where(sel, payload_ref[...], -1), axis=0, keepdims=True)
        x = jnp.where(sel, -jnp.inf, x)


def _route_kernel(h_ref, g_ref, wq_ref, k1_ref, k2_ref, idx_ref, gate_ref,
                  q_scr, s_scr, t1_scr, i1_scr, t2_scr, i2_scr, cand_scr, cidx_scr, sc_scr, e_scr, et_scr, gt_scr):
    tm = h_ref.shape[0]
    c = _rms(h_ref[...], g_ref[...]).astype(BF16)
    q_scr[...] = jnp.dot(c, wq_ref[...], preferred_element_type=F32).astype(BF16)
    half = N_KEYS
    for h in range(PEER_HEADS):
        for sub in range(tm // LANES):
            tok = slice(sub * LANES, (sub + 1) * LANES)
            q1 = q_scr[tok, 2 * h * half:(2 * h + 1) * half]
            q2 = q_scr[tok, (2 * h + 1) * half:(2 * h + 2) * half]
            s_scr[...] = _mm_nt(k1_ref[...], q1)
            _top16(s_scr, None, t1_scr, i1_scr)
            s_scr[...] = _mm_nt(k2_ref[...], q2)
            _top16(s_scr, None, t2_scr, i2_scr)
            t2 = t2_scr[...]
            i2 = i2_scr[...]
            for a in range(PEER_TOPK):
                cand_scr[a * PEER_TOPK:(a + 1) * PEER_TOPK, :] = t1_scr[a:a + 1, :] + t2
                cidx_scr[a * PEER_TOPK:(a + 1) * PEER_TOPK, :] = i1_scr[a:a + 1, :] * N_KEYS + i2
            _top16(cand_scr, cidx_scr, sc_scr, e_scr)
            sc = sc_scr[...]
            p = jnp.exp(sc - sc[0:1, :])
            gt_scr[h * PEER_TOPK:(h + 1) * PEER_TOPK, tok] = p / jnp.sum(p, axis=0, keepdims=True)
            et_scr[h * PEER_TOPK:(h + 1) * PEER_TOPK, tok] = e_scr[...]
    idx_ref[...] = et_scr[...].T
    gate_ref[...] = gt_scr[...].T


def _route(h, g, wq, k1, k2, tm=256):
    t = h.shape[0]
    return pl.pallas_call(
        _route_kernel,
        grid=(t // tm,),
        in_specs=[
            pl.BlockSpec((tm, D_MODEL), lambda i: (i, 0)),
            pl.BlockSpec((1, D_MODEL), lambda i: (0, 0)),
            pl.BlockSpec((D_MODEL, D_MODEL), lambda i: (0, 0)),
            pl.BlockSpec((N_KEYS, N_KEYS), lambda i: (0, 0)),
            pl.BlockSpec((N_KEYS, N_KEYS), lambda i: (0, 0)),
        ],
        out_specs=[
            pl.BlockSpec((tm, PEER_K), lambda i: (i, 0)),
            pl.BlockSpec((tm, PEER_K), lambda i: (i, 0)),
        ],
        out_shape=[jax.ShapeDtypeStruct((t, PEER_K), jnp.int32), jax.ShapeDtypeStruct((t, PEER_K), F32)],
        scratch_shapes=[
            pltpu.VMEM((tm, D_MODEL), BF16),
            pltpu.VMEM((N_KEYS, LANES), F32),
            pltpu.VMEM((PEER_TOPK, LANES), F32),
            pltpu.VMEM((PEER_TOPK, LANES), jnp.int32),
            pltpu.VMEM((PEER_TOPK, LANES), F32),
            pltpu.VMEM((PEER_TOPK, LANES), jnp.int32),
            pltpu.VMEM((PEER_TOPK * PEER_TOPK, LANES), F32),
            pltpu.VMEM((PEER_TOPK * PEER_TOPK, LANES), jnp.int32),
            pltpu.VMEM((PEER_TOPK, LANES), F32),
            pltpu.VMEM((PEER_TOPK, LANES), jnp.int32),
            pltpu.VMEM((PEER_K, tm), jnp.int32),
            pltpu.VMEM((PEER_K, tm), F32),
        ],
        compiler_params=_params(("arbitrary",)),
        name="route",
    )(h, g, wq, k1, k2)


PEER_TB = 64
PEER_SLOTS = 4
INV_SQRT2 = 0.7071067811865476


def _peer_kernel(idx_hbm, h_ref, g_ref, gate_ref, uv_hbm, o_ref, idx_s, c_scr, rows, idx_sem, row_sems):
    tb = h_ref.shape[0]
    step = pl.program_id(0)
    idx_copy = pltpu.make_async_copy(idx_hbm.at[pl.ds(step * (tb * PEER_K), tb * PEER_K)], idx_s, idx_sem)
    idx_copy.start()
    c_scr[...] = _rms(h_ref[...], g_ref[...])
    idx_copy.wait()

    def issue(t, slot):
        for k in range(PEER_K):
            e = idx_s[t * PEER_K + k]
            pltpu.make_async_copy(uv_hbm.at[pl.ds(e, 1), :], rows.at[slot, pl.ds(k, 1), :], row_sems.at[slot]).start()

    def wait(slot):
        pltpu.make_async_copy(uv_hbm.at[pl.ds(0, PEER_K), :], rows.at[slot], row_sems.at[slot]).wait()

    def prologue(s, carry):
        issue(s, s)
        return carry

    lax.fori_loop(0, PEER_SLOTS - 1, prologue, 0)
    eye = lax.broadcasted_iota(jnp.int32, (PEER_K, LANES), 0) == lax.broadcasted_iota(jnp.int32, (PEER_K, LANES), 1)

    def body(t, carry):
        slot = t % PEER_SLOTS
        ahead = t + (PEER_SLOTS - 1)

        @pl.when(ahead < tb)
        def _():
            issue(ahead, ahead % PEER_SLOTS)

        wait(slot)
        x = c_scr[pl.ds(t, 1), :]
        acc = rows[slot, :, 0:LANES] * x[:, 0:LANES]
        for j in range(1, D_MODEL // LANES):
            acc = acc + rows[slot, :, j * LANES:(j + 1) * LANES] * x[:, j * LANES:(j + 1) * LANES]
        act = jnp.sum(acc, axis=1, keepdims=True)
        gate_col = jnp.sum(jnp.where(eye, gate_ref[pl.ds(t, 1), :], 0.0), axis=1, keepdims=True)
        coef = gate_col * (0.5 * act * (1.0 + lax.erf(act * INV_SQRT2)))
        mixed = []
        for j in range(D_MODEL // LANES):
            lo = D_MODEL + j * LANES
            mixed.append(jnp.sum(rows[slot, :, lo:lo + LANES] * coef, axis=0, keepdims=True))
        o_ref[pl.ds(t, 1), :] = h_ref[pl.ds(t, 1), :] + jnp.concatenate(mixed, axis=1)
        return carry

    lax.fori_loop(0, tb, body, 0)


def _peer(idx_flat, h, g, gate, uv):
    t = h.shape[0]
    tb = PEER_TB
    return pl.pallas_call(
        _peer_kernel,
        grid=(t // tb,),
        in_specs=[
            pl.BlockSpec(memory_space=pl.ANY),
            pl.BlockSpec((tb, D_MODEL), lambda i: (i, 0)),
            pl.BlockSpec((1, D_MODEL), lambda i: (0, 0)),
            pl.BlockSpec((tb, PEER_K), lambda i: (i, 0)),
            pl.BlockSpec(memory_space=pl.ANY),
        ],
        out_specs=pl.BlockSpec((tb, D_MODEL), lambda i: (i, 0)),
        out_shape=jax.ShapeDtypeStruct((t, D_MODEL), F32),
        scratch_shapes=[
            pltpu.SMEM((tb * PEER_K,), jnp.int32),
            pltpu.VMEM((tb, D_MODEL), F32),
            pltpu.VMEM((PEER_SLOTS, PEER_K, 2 * D_MODEL), F32),
            pltpu.SemaphoreType.DMA(()),
            pltpu.SemaphoreType.DMA((PEER_SLOTS,)),
        ],
        compiler_params=_params(("arbitrary",)),
        name="peer",
    )(idx_flat, h, g, gate, uv)


def _ple_kernel(h_ref, p_ref, gple_ref, wg_ref, wp_ref, gfin_ref, y_ref):
    h = h_ref[...]
    e = _rms(h, gple_ref[...]).astype(BF16)
    gate = _sigmoid(jnp.dot(e, wg_ref[...], preferred_element_type=F32))
    proj = jnp.dot(p_ref[...].astype(BF16), wp_ref[...], preferred_element_type=F32)
    y_ref[...] = _rms(h + gate * proj, gfin_ref[...])


def _ple(h, p, g_ple, wg, wp, g_fin, tm=256):
    t = h.shape[0]
    return pl.pallas_call(
        _ple_kernel,
        grid=(t // tm,),
        in_specs=[
            pl.BlockSpec((tm, D_MODEL), lambda i: (i, 0)),
            pl.BlockSpec((tm, PLE_DIM), lambda i: (i, 0)),
            pl.BlockSpec((1, D_MODEL), lambda i: (0, 0)),
            pl.BlockSpec((D_MODEL, D_MODEL), lambda i: (0, 0)),
            pl.BlockSpec((PLE_DIM, D_MODEL), lambda i: (0, 0)),
            pl.BlockSpec((1, D_MODEL), lambda i: (0, 0)),
        ],
        out_specs=pl.BlockSpec((tm, D_MODEL), lambda i: (i, 0)),
        out_shape=jax.ShapeDtypeStruct((t, D_MODEL), F32),
        compiler_params=_params(("arbitrary",)),
        name="ple",
    )(h, p, g_ple, wg, wp, g_fin)


def _rope_tables(pos):
    half = RET_D // 2
    inv = ROPE_BASE ** (-jnp.arange(half, dtype=F32) / half)
    ang = pos[:, None] * inv[None, :]
    cos, sin = jnp.cos(ang), jnp.sin(ang)
    return jnp.concatenate([cos, cos], axis=-1), jnp.concatenate([-sin, sin], axis=-1)


def _stream(x, p, pos, s0, c0, n0, m0, w):
    b, l, _ = x.shape
    t = b * l
    x2 = x.reshape(t, D_MODEL)
    z, gz = _in_proj(x2, w["g_mix"], w["w_in"], w["w_gates"])
    cosf, sinf = _rope_tables(pos)
    m0b = jnp.broadcast_to(m0[:, :, None], (b, MLSTM_HEADS, LANES))
    mix, s_new, c_new, n_new, m_new = _mixers(
        z.reshape(b, l, Z_COLS), gz.reshape(b, l, LANES), w["bias"], cosf, sinf, s0, c0, n0, m0b, w["g_ret"], w["g_ml"])
    h1 = _out_proj(mix.reshape(t, D_MODEL), w["w_out"], x2)
    idx, gate = _route(h1, w["g_ffn"], w["w_q"], w["k1"], w["k2"])
    h2 = _peer(idx.reshape(t * PEER_K), h1, w["g_ffn"], gate, w["uv"])
    y = _ple(h2, p.reshape(t, PLE_DIM), w["g_ple"], w["w_ple_gate"], w["w_ple_proj"], w["g_final"])
    return y.reshape(b, l, D_MODEL), s_new[None], c_new[None], n_new[None], m_new[None, :, :, 0]


def _prep_weights(g_mix, w_in, b_gates, g_ret, g_mlstm, w_out, g_ffn, w_peer_q, peer_keys1, peer_keys2, peer_u, peer_v,
                  g_ple, w_ple_gate, w_ple_proj, g_final):
    w_in0 = w_in[0]
    return {
        "g_mix": g_mix,
        "w_in": w_in0[:, :Z_COLS].astype(BF16),
        "w_gates": jnp.pad(w_in0[:, Z_COLS:], ((0, 0), (0, LANES - 2 * MLSTM_HEADS))).astype(BF16),
        "bias": jnp.pad(b_gates, ((0, 0), (0, LANES - 2 * MLSTM_HEADS))),
        "g_ret": g_ret,
        "g_ml": g_mlstm,
        "w_out": w_out[0].astype(BF16),
        "g_ffn": g_ffn,
        "w_q": w_peer_q[0].astype(BF16),
        "k1": peer_keys1[0].astype(BF16),
        "k2": peer_keys2[0].astype(BF16),
        "uv": jnp.concatenate([peer_u[0], peer_v[0]], axis=1),
        "g_ple": g_ple,
        "w_ple_gate": w_ple_gate[0].astype(BF16),
        "w_ple_proj": w_ple_proj[0].astype(BF16),
        "g_final": g_final[None, :],
    }


def kernel(x_prompt, x_sample, p_prompt, p_sample, state_ret, state_mlstm_C, state_mlstm_n, state_mlstm_m, g_mix, w_in, b_gates, g_ret, g_mlstm, w_out, g_ffn, w_peer_q, peer_keys1, peer_keys2, peer_u, peer_v, g_ple, w_ple_gate, w_ple_proj, g_final):
    w = _prep_weights(g_mix, w_in, b_gates, g_ret, g_mlstm, w_out, g_ffn, w_peer_q, peer_keys1, peer_keys2, peer_u, peer_v,
                      g_ple, w_ple_gate, w_ple_proj, g_final)
    bp, lp, _ = x_prompt.shape
    bs, ls, _ = x_sample.shape
    zeros = lambda *shape: jnp.zeros(shape, F32)
    past_len = 1024
    y_p, ret_p, c_p, n_p, m_p = _stream(
        x_prompt, p_prompt[0], jnp.arange(lp, dtype=F32),
        zeros(bp, RET_HEADS, RET_D, RET_D), zeros(bp, MLSTM_HEADS, MLSTM_DK, MLSTM_DV),
        zeros(bp, MLSTM_HEADS, MLSTM_DK), zeros(bp, MLSTM_HEADS), w)
    y_s, ret_s, c_s, n_s, m_s = _stream(
        x_sample, p_sample[0], past_len + jnp.arange(ls, dtype=F32),
        state_ret[0], state_mlstm_C[0], state_mlstm_n[0], state_mlstm_m[0], w)
    return (y_p, y_s, ret_p, c_p, n_p, m_p, ret_s, c_s, n_s, m_s)
```

```python
import functools
import math

import numpy as np
import jax
import jax.numpy as jnp
from jax import lax
from jax.experimental import pallas as pl
from jax.experimental.pallas import tpu as pltpu

F32 = jnp.float32
BF16 = jnp.bfloat16

D_MODEL = 2048
CHUNK = 64
RMS_EPS = 1e-6
ROPE_BASE = 10000.0
RET_HEADS = 8
RET_D = 128
MLSTM_HEADS = 4
MLSTM_DK = 128
MLSTM_DV = 256
Z_COLS = 7168
OFF_RQ, OFF_RK, OFF_RV, OFF_RG = 0, 1024, 2048, 3072
OFF_MQ, OFF_MK, OFF_MV, OFF_MO = 4096, 4608, 5120, 6144
PEER_HEADS = 8
N_KEYS = 128
PEER_TOPK = 16
PEER_K = PEER_HEADS * PEER_TOPK
PLE_DIM = 256
LANES = 128

LOG_GAMMA = [float(np.log(np.float32(1.0) - np.float32(2.0) ** np.float32(-5.0 - h))) for h in range(RET_HEADS)]

VMEM_LIMIT = 56 * 1024 * 1024


def _params(sem):
    return pltpu.CompilerParams(dimension_semantics=sem, vmem_limit_bytes=VMEM_LIMIT)


def _rms(x, g):
    return x * lax.rsqrt(jnp.mean(x * x, axis=-1, keepdims=True) + RMS_EPS) * g


def _mm(a, b):
    return jnp.dot(a.astype(BF16), b.astype(BF16), preferred_element_type=F32)


def _mm_nt(a, b):
    return lax.dot_general(a.astype(BF16), b.astype(BF16), (((1,), (1,)), ((), ())), preferred_element_type=F32)


def _mm_tn(a, b):
    return lax.dot_general(a.astype(BF16), b.astype(BF16), (((0,), (0,)), ((), ())), preferred_element_type=F32)


def _sigmoid(x):
    return 1.0 / (1.0 + jnp.exp(-x))


def _in_proj_kernel(x_ref, g_ref, w_ref, wg_ref, z_ref, gz_ref, a_scr):
    @pl.when(pl.program_id(1) == 0)
    def _():
        a_scr[...] = _rms(x_ref[...], g_ref[...]).astype(BF16)
        gz_ref[...] = jnp.dot(a_scr[...], wg_ref[...], preferred_element_type=F32)

    z_ref[...] = jnp.dot(a_scr[...], w_ref[...], preferred_element_type=F32)


def _in_proj(x, g, w, wg, tm=512, tn=1024):
    t = x.shape[0]
    return pl.pallas_call(
        _in_proj_kernel,
        grid=(t // tm, Z_COLS // tn),
        in_specs=[
            pl.BlockSpec((tm, D_MODEL), lambda i, j: (i, 0)),
            pl.BlockSpec((1, D_MODEL), lambda i, j: (0, 0)),
            pl.BlockSpec((D_MODEL, tn), lambda i, j: (0, j)),
            pl.BlockSpec((D_MODEL, LANES), lambda i, j: (0, 0)),
        ],
        out_specs=[
            pl.BlockSpec((tm, tn), lambda i, j: (i, j)),
            pl.BlockSpec((tm, LANES), lambda i, j: (i, 0)),
        ],
        out_shape=[jax.ShapeDtypeStruct((t, Z_COLS), F32), jax.ShapeDtypeStruct((t, LANES), F32)],
        scratch_shapes=[pltpu.VMEM((tm, D_MODEL), BF16)],
        compiler_params=_params(("arbitrary", "arbitrary")),
        name="in_proj",
    )(x, g, w, wg)


def _log_sigmoid(x):
    return -(jnp.maximum(-x, 0.0) + jnp.log1p(jnp.exp(-jnp.abs(x))))


def _mixers_kernel(z_ref, gz_ref, bias_ref, cos_ref, sin_ref, s0_ref, c0_ref, n0_ref, m0_ref, gret_ref, gml_ref,
                   mix_ref, s_ref, c_ref, n_ref, m_ref):
    @pl.when(pl.program_id(1) == 0)
    def _():
        s_ref[...] = s0_ref[...]
        c_ref[...] = c0_ref[...]
        n_ref[...] = n0_ref[...]
        m_ref[...] = m0_ref[...]

    cl = CHUNK
    row = lax.broadcasted_iota(jnp.int32, (cl, cl), 0)
    col = lax.broadcasted_iota(jnp.int32, (cl, cl), 1)
    causal = row >= col
    diff = jnp.where(causal, (row - col).astype(F32), 0.0)
    posc = lax.broadcasted_iota(jnp.int32, (cl, 1), 0).astype(F32)
    cosf = cos_ref[...]
    sinf = sin_ref[...]

    def rope(x):
        return x * cosf + pltpu.roll(x, RET_D // 2, axis=1) * sinf

    for h in range(RET_HEADS):
        lg = LOG_GAMMA[h]
        lo = h * RET_D
        q = rope(z_ref[0, :, OFF_RQ + lo:OFF_RQ + lo + RET_D])
        k = rope(z_ref[0, :, OFF_RK + lo:OFF_RK + lo + RET_D]) * (RET_D ** -0.5)
        v = z_ref[0, :, OFF_RV + lo:OFF_RV + lo + RET_D]
        rg = z_ref[0, :, OFF_RG + lo:OFF_RG + lo + RET_D]
        decay = jnp.where(causal, jnp.exp(diff * lg), 0.0)
        s_old = s_ref[0, h]
        scores = _mm_nt(q, k) * decay
        q_dec = q * jnp.exp((posc + 1.0) * lg)
        k_dec = k * jnp.exp((cl - 1.0 - posc) * lg)
        o = _mm(scores, v) + _mm(q_dec, s_old)
        s_ref[0, h] = math.exp(cl * lg) * s_old + _mm_tn(k_dec, v)
        y = o * lax.rsqrt(jnp.mean(o * o, axis=-1, keepdims=True) + RMS_EPS) * gret_ref[:, lo:lo + RET_D]
        mix_ref[0, :, lo:lo + RET_D] = (y * (rg * _sigmoid(rg))).astype(BF16)

    gates = gz_ref[0] + bias_ref[...]
    tri = causal.astype(F32)
    bcum = jnp.dot(tri, _log_sigmoid(gates), preferred_element_type=F32, precision=lax.Precision.HIGHEST)
    bcum_t = bcum.T
    gates_t = gates.T
    for h in range(MLSTM_HEADS):
        q = z_ref[0, :, OFF_MQ + h * MLSTM_DK:OFF_MQ + (h + 1) * MLSTM_DK]
        k = z_ref[0, :, OFF_MK + h * MLSTM_DK:OFF_MK + (h + 1) * MLSTM_DK] * (MLSTM_DK ** -0.5)
        v = z_ref[0, :, OFF_MV + h * MLSTM_DV:OFF_MV + (h + 1) * MLSTM_DV]
        mo = z_ref[0, :, OFF_MO + h * MLSTM_DV:OFF_MO + (h + 1) * MLSTM_DV]
        f = MLSTM_HEADS + h
        b_col = bcum[:, f:f + 1]
        b_row = bcum_t[f:f + 1, :]
        ig_col = gates[:, h:h + 1]
        ig_row = gates_t[h:h + 1, :]
        m_prev = m_ref[0, h:h + 1, 0:1]
        dlog = jnp.where(causal, b_col - b_row + ig_row, -jnp.inf)
        inter_log = b_col + m_prev
        m_t = jnp.maximum(inter_log, jnp.max(dlog, axis=-1, keepdims=True))
        dw = jnp.exp(dlog - m_t)
        inter_w = jnp.exp(inter_log - m_t)
        c_old = c_ref[0, h]
        n_old = n_ref[0, h:h + 1, :]
        sm = _mm_nt(q, k) * dw
        num = _mm(sm, v) + inter_w * _mm(q, c_old)
        den = jnp.sum(sm, axis=-1, keepdims=True) + inter_w * jnp.sum(q * n_old, axis=-1, keepdims=True)
        hh = num / jnp.maximum(jnp.abs(den), jnp.exp(-m_t))
        m_new = m_t[cl - 1:cl, :]
        b_last = b_col[cl - 1:cl, :]
        ws = jnp.exp(b_last - b_col + ig_col - m_new)
        carry = jnp.exp(b_last + m_prev - m_new)
        kw = k * ws
        c_ref[0, h] = carry * c_old + _mm_tn(kw, v)
        n_ref[0, h:h + 1, :] = carry * n_old + jnp.sum(kw, axis=0, keepdims=True)
        m_ref[0, h:h + 1, :] = jnp.broadcast_to(m_new, (1, LANES))
        y = hh * lax.rsqrt(jnp.mean(hh * hh, axis=-1, keepdims=True) + RMS_EPS) * gml_ref[:, h * MLSTM_DV:(h + 1) * MLSTM_DV]
        lo = RET_HEADS * RET_D + h * MLSTM_DV
        mix_ref[0, :, lo:lo + MLSTM_DV] = (y * _sigmoid(mo)).astype(BF16)


def _mixers(z, gz, bias, cosf, sinf, s0, c0, n0, m0, g_ret, g_ml):
    b, l, _ = z.shape
    per_b = lambda *tail: (lambda i, c: (i,) + tail)
    return pl.pallas_call(
        _mixers_kernel,
        grid=(b, l // CHUNK),
        in_specs=[
            pl.BlockSpec((1, CHUNK, Z_COLS), lambda i, c: (i, c, 0)),
            pl.BlockSpec((1, CHUNK, LANES), lambda i, c: (i, c, 0)),
            pl.BlockSpec((1, LANES), lambda i, c: (0, 0)),
            pl.BlockSpec((CHUNK, RET_D), lambda i, c: (c, 0)),
            pl.BlockSpec((CHUNK, RET_D), lambda i, c: (c, 0)),
            pl.BlockSpec((1, RET_HEADS, RET_D, RET_D), per_b(0, 0, 0)),
            pl.BlockSpec((1, MLSTM_HEADS, MLSTM_DK, MLSTM_DV), per_b(0, 0, 0)),
            pl.BlockSpec((1, MLSTM_HEADS, MLSTM_DK), per_b(0, 0)),
            pl.BlockSpec((1, MLSTM_HEADS, LANES), per_b(0, 0)),
            pl.BlockSpec((1, RET_HEADS * RET_D), lambda i, c: (0, 0)),
            pl.BlockSpec((1, MLSTM_HEADS * MLSTM_DV), lambda i, c: (0, 0)),
        ],
        out_specs=[
            pl.BlockSpec((1, CHUNK, D_MODEL), lambda i, c: (i, c, 0)),
            pl.BlockSpec((1, RET_HEADS, RET_D, RET_D), per_b(0, 0, 0)),
            pl.BlockSpec((1, MLSTM_HEADS, MLSTM_DK, MLSTM_DV), per_b(0, 0, 0)),
            pl.BlockSpec((1, MLSTM_HEADS, MLSTM_DK), per_b(0, 0)),
            pl.BlockSpec((1, MLSTM_HEADS, LANES), per_b(0, 0)),
        ],
        out_shape=[
            jax.ShapeDtypeStruct((b, l, D_MODEL), BF16),
            jax.ShapeDtypeStruct((b, RET_HEADS, RET_D, RET_D), F32),
            jax.ShapeDtypeStruct((b, MLSTM_HEADS, MLSTM_DK, MLSTM_DV), F32),
            jax.ShapeDtypeStruct((b, MLSTM_HEADS, MLSTM_DK), F32),
            jax.ShapeDtypeStruct((b, MLSTM_HEADS, LANES), F32),
        ],
        compiler_params=_params(("arbitrary", "arbitrary")),
        name="mixers",
    )(z, gz, bias, cosf, sinf, s0, c0, n0, m0, g_ret, g_ml)


def _out_proj_kernel(a_ref, w_ref, r_ref, o_ref):
    o_ref[...] = r_ref[...] + jnp.dot(a_ref[...], w_ref[...], preferred_element_type=F32)


def _out_proj(a, w, r, tm=512):
    t = a.shape[0]
    return pl.pallas_call(
        _out_proj_kernel,
        grid=(t // tm,),
        in_specs=[
            pl.BlockSpec((tm, D_MODEL), lambda i: (i, 0)),
            pl.BlockSpec((D_MODEL, D_MODEL), lambda i: (0, 0)),
            pl.BlockSpec((tm, D_MODEL), lambda i: (i, 0)),
        ],
        out_specs=pl.BlockSpec((tm, D_MODEL), lambda i: (i, 0)),
        out_shape=jax.ShapeDtypeStruct((t, D_MODEL), F32),
        compiler_params=_params(("arbitrary",)),
        name="out_proj",
    )(a, w, r)


def _top16(x_ref, payload_ref, val_ref, idx_ref):
    x = x_ref[...]
    rows = x.shape[0]
    iota = lax.broadcasted_iota(jnp.int32, x.shape, 0)
    for r in range(PEER_TOPK):
        m = jnp.max(x, axis=0, keepdims=True)
        am = jnp.min(jnp.where(x == m, iota, rows), axis=0, keepdims=True)
        sel = iota == am
        val_ref[r:r + 1, :] = m
        if payload_ref is None:
            idx_ref[r:r + 1, :] = am
        else:
            idx_ref[r:r + 1, :] = jnp.max(jnp.where(sel, payload_ref[...], -1), axis=0, keepdims=True)
        x = jnp.where(sel, -jnp.inf, x)


def _route_kernel(h_ref, g_ref, wq_ref, k1_ref, k2_ref, idx_ref, gate_ref,
                  q_scr, s_scr, t1_scr, i1_scr, t2_scr, i2_scr, cand_scr, cidx_scr, sc_scr, e_scr, et_scr, gt_scr):
    tm = h_ref.shape[0]
    c = _rms(h_ref[...], g_ref[...]).astype(BF16)
    q_scr[...] = jnp.dot(c, wq_ref[...], preferred_element_type=F32).astype(BF16)
    half = N_KEYS
    for h in range(PEER_HEADS):
        for sub in range(tm // LANES):
            tok = slice(sub * LANES, (sub + 1) * LANES)
            q1 = q_scr[tok, 2 * h * half:(2 * h + 1) * half]
            q2 = q_scr[tok, (2 * h + 1) * half:(2 * h + 2) * half]
            s_scr[...] = _mm_nt(k1_ref[...], q1)
            _top16(s_scr, None, t1_scr, i1_scr)
            s_scr[...] = _mm_nt(k2_ref[...], q2)
            _top16(s_scr, None, t2_scr, i2_scr)
            t2 = t2_scr[...]
            i2 = i2_scr[...]
            for a in range(PEER_TOPK):
                cand_scr[a * PEER_TOPK:(a + 1) * PEER_TOPK, :] = t1_scr[a:a + 1, :] + t2
                cidx_scr[a * PEER_TOPK:(a + 1) * PEER_TOPK, :] = i1_scr[a:a + 1, :] * N_KEYS + i2
            _top16(cand_scr, cidx_scr, sc_scr, e_scr)
            sc = sc_scr[...]
            p = jnp.exp(sc - sc[0:1, :])
            gt_scr[h * PEER_TOPK:(h + 1) * PEER_TOPK, tok] = p / jnp.sum(p, axis=0, keepdims=True)
            et_scr[h * PEER_TOPK:(h + 1) * PEER_TOPK, tok] = e_scr[...]
    idx_ref[...] = et_scr[...].T
    gate_ref[...] = gt_scr[...].T


def _route(h, g, wq, k1, k2, tm=256):
    t = h.shape[0]
    return pl.pallas_call(
        _route_kernel,
        grid=(t // tm,),
        in_specs=[
            pl.BlockSpec((tm, D_MODEL), lambda i: (i, 0)),
            pl.BlockSpec((1, D_MODEL), lambda i: (0, 0)),
            pl.BlockSpec((D_MODEL, D_MODEL), lambda i: (0, 0)),
            pl.BlockSpec((N_KEYS, N_KEYS), lambda i: (0, 0)),
            pl.BlockSpec((N_KEYS, N_KEYS), lambda i: (0, 0)),
        ],
        out_specs=[
            pl.BlockSpec((tm, PEER_K), lambda i: (i, 0)),
            pl.BlockSpec((tm, PEER_K), lambda i: (i, 0)),
        ],
        out_shape=[jax.ShapeDtypeStruct((t, PEER_K), jnp.int32), jax.ShapeDtypeStruct((t, PEER_K), F32)],
        scratch_shapes=[
            pltpu.VMEM((tm, D_MODEL), BF16),
            pltpu.VMEM((N_KEYS, LANES), F32),
            pltpu.VMEM((PEER_TOPK, LANES), F32),
            pltpu.VMEM((PEER_TOPK, LANES), jnp.int32),
            pltpu.VMEM((PEER_TOPK, LANES), F32),
            pltpu.VMEM((PEER_TOPK, LANES), jnp.int32),
            pltpu.VMEM((PEER_TOPK * PEER_TOPK, LANES), F32),
            pltpu.VMEM((PEER_TOPK * PEER_TOPK, LANES), jnp.int32),
            pltpu.VMEM((PEER_TOPK, LANES), F32),
            pltpu.VMEM((PEER_TOPK, LANES), jnp.int32),
            pltpu.VMEM((PEER_K, tm), jnp.int32),
            pltpu.VMEM((PEER_K, tm), F32),
        ],
        compiler_params=_params(("arbitrary",)),
        name="route",
    )(h, g, wq, k1, k2)


PEER_TB = 256
PEER_SLOTS = 4
PEER_LEAD = PEER_SLOTS - 1
N_TILES = D_MODEL // LANES
INV_SQRT2 = 0.7071067811865476


def _peer_kernel(idx_hbm, h_ref, g_ref, gate_ref, uv_hbm, o_ref, idx_s, c_scr, rows0, rows1, rows2, rows3, idx_sem, row_sems):
    rows = (rows0, rows1, rows2, rows3)
    tb = h_ref.shape[0]
    step = pl.program_id(0)
    idx_copy = pltpu.make_async_copy(idx_hbm.at[pl.ds(step * (tb * PEER_K), tb * PEER_K)], idx_s, idx_sem)
    idx_copy.start()
    c_scr[...] = _rms(h_ref[...], g_ref[...])
    idx_copy.wait()

    def issue(t, slot, k0, k1):
        for k in range(k0, k1):
            e = idx_s[t * PEER_K + k]
            pltpu.make_async_copy(uv_hbm.at[e], rows[slot].at[pl.ds(k, 1), :], row_sems.at[slot]).start()

    def wait(slot):
        pltpu.make_async_copy(uv_hbm.at[pl.ds(0, PEER_K), 0], rows[slot], row_sems.at[slot]).wait()

    eye = lax.broadcasted_iota(jnp.int32, (PEER_K, LANES), 0) == lax.broadcasted_iota(jnp.int32, (PEER_K, LANES), 1)
    per_tile = PEER_K // N_TILES

    def token(t, slot, prefetch):
        wait(slot)
        ahead_slot = (slot + PEER_LEAD) % PEER_SLOTS
        x = c_scr[pl.ds(t, 1), :]
        acc = None
        for j in range(N_TILES):
            if prefetch:
                issue(t + PEER_LEAD, ahead_slot, j * per_tile, (j + 1) * per_tile)
            p = rows[slot][:, j * LANES:(j + 1) * LANES] * x[:, j * LANES:(j + 1) * LANES]
            acc = p if acc is None else acc + p
        act = jnp.sum(acc, axis=1, keepdims=True)
        gate_col = jnp.sum(jnp.where(eye, gate_ref[pl.ds(t, 1), :], 0.0), axis=1, keepdims=True)
        coef = gate_col * (0.5 * act * (1.0 + lax.erf(act * INV_SQRT2)))
        mixed = []
        for j in range(N_TILES):
            lo = D_MODEL + j * LANES
            mixed.append(jnp.sum(rows[slot][:, lo:lo + LANES] * coef, axis=0, keepdims=True))
        o_ref[pl.ds(t, 1), :] = h_ref[pl.ds(t, 1), :] + jnp.concatenate(mixed, axis=1)

    for s in range(PEER_LEAD):
        issue(s, s, 0, PEER_K)

    def group(i, carry):
        for s in range(PEER_SLOTS):
            token(i * PEER_SLOTS + s, s, True)
        return carry

    lax.fori_loop(0, tb // PEER_SLOTS - 1, group, 0)
    last = tb - PEER_SLOTS
    token(last, 0, True)
    for s in range(1, PEER_SLOTS):
        token(last + s, s, False)


def _peer(idx_flat, h, g, gate, uv):
    t = h.shape[0]
    tb = PEER_TB
    return pl.pallas_call(
        _peer_kernel,
        grid=(t // tb,),
        in_specs=[
            pl.BlockSpec(memory_space=pl.ANY),
            pl.BlockSpec((tb, D_MODEL), lambda i: (i, 0)),
            pl.BlockSpec((1, D_MODEL), lambda i: (0, 0)),
            pl.BlockSpec((tb, PEER_K), lambda i: (i, 0)),
            pl.BlockSpec(memory_space=pl.ANY),
        ],
        out_specs=pl.BlockSpec((tb, D_MODEL), lambda i: (i, 0)),
        out_shape=jax.ShapeDtypeStruct((t, D_MODEL), F32),
        scratch_shapes=[
            pltpu.SMEM((tb * PEER_K,), jnp.int32),
            pltpu.VMEM((tb, D_MODEL), F32),
        ] + [pltpu.VMEM((PEER_K, 2 * D_MODEL), F32)] * PEER_SLOTS + [
            pltpu.SemaphoreType.DMA(()),
            pltpu.SemaphoreType.DMA((PEER_SLOTS,)),
        ],
        compiler_params=_params(("arbitrary",)),
        name="peer",
    )(idx_flat, h, g, gate, uv)


def _ple_kernel(h_ref, p_ref, gple_ref, wg_ref, wp_ref, gfin_ref, y_ref):
    h = h_ref[...]
    e = _rms(h, gple_ref[...]).astype(BF16)
    gate = _sigmoid(jnp.dot(e, wg_ref[...], preferred_element_type=F32))
    proj = jnp.dot(p_ref[...].astype(BF16), wp_ref[...], preferred_element_type=F32)
    y_ref[...] = _rms(h + gate * proj, gfin_ref[...])


def _ple(h, p, g_ple, wg, wp, g_fin, tm=256):
    t = h.shape[0]
    return pl.pallas_call(
        _ple_kernel,
        grid=(t // tm,),
        in_specs=[
            pl.BlockSpec((tm, D_MODEL), lambda i: (i, 0)),
            pl.BlockSpec((tm, PLE_DIM), lambda i: (i, 0)),
            pl.BlockSpec((1, D_MODEL), lambda i: (0, 0)),
            pl.BlockSpec((D_MODEL, D_MODEL), lambda i: (0, 0)),
            pl.BlockSpec((PLE_DIM, D_MODEL), lambda i: (0, 0)),
            pl.BlockSpec((1, D_MODEL), lambda i: (0, 0)),
        ],
        out_specs=pl.BlockSpec((tm, D_MODEL), lambda i: (i, 0)),
        out_shape=jax.ShapeDtypeStruct((t, D_MODEL), F32),
        compiler_params=_params(("arbitrary",)),
        name="ple",
    )(h, p, g_ple, wg, wp, g_fin)


def _rope_tables(pos):
    half = RET_D // 2
    inv = ROPE_BASE ** (-jnp.arange(half, dtype=F32) / half)
    ang = pos[:, None] * inv[None, :]
    cos, sin = jnp.cos(ang), jnp.sin(ang)
    return jnp.concatenate([cos, cos], axis=-1), jnp.concatenate([-sin, sin], axis=-1)


def _stream(x, p, pos, s0, c0, n0, m0, w):
    b, l, _ = x.shape
    t = b * l
    x2 = x.reshape(t, D_MODEL)
    z, gz = _in_proj(x2, w["g_mix"], w["w_in"], w["w_gates"])
    cosf, sinf = _rope_tables(pos)
    m0b = jnp.broadcast_to(m0[:, :, None], (b, MLSTM_HEADS, LANES))
    mix, s_new, c_new, n_new, m_new = _mixers(
        z.reshape(b, l, Z_COLS), gz.reshape(b, l, LANES), w["bias"], cosf, sinf, s0, c0, n0, m0b, w["g_ret"], w["g_ml"])
    h1 = _out_proj(mix.reshape(t, D_MODEL), w["w_out"], x2)
    idx, gate = _route(h1, w["g_ffn"], w["w_q"], w["k1"], w["k2"])
    h2 = _peer(idx.reshape(t * PEER_K), h1, w["g_ffn"], gate, w["uv"])
    y = _ple(h2, p.reshape(t, PLE_DIM), w["g_ple"], w["w_ple_gate"], w["w_ple_proj"], w["g_final"])
    return y.reshape(b, l, D_MODEL), s_new[None], c_new[None], n_new[None], m_new[None, :, :, 0]


def _prep_weights(g_mix, w_in, b_gates, g_ret, g_mlstm, w_out, g_ffn, w_peer_q, peer_keys1, peer_keys2, peer_u, peer_v,
                  g_ple, w_ple_gate, w_ple_proj, g_final):
    w_in0 = w_in[0]
    return {
        "g_mix": g_mix,
        "w_in": w_in0[:, :Z_COLS].astype(BF16),
        "w_gates": jnp.pad(w_in0[:, Z_COLS:], ((0, 0), (0, LANES - 2 * MLSTM_HEADS))).astype(BF16),
        "bias": jnp.pad(b_gates, ((0, 0), (0, LANES - 2 * MLSTM_HEADS))),
        "g_ret": g_ret,
        "g_ml": g_mlstm,
        "w_out": w_out[0].astype(BF16),
        "g_ffn": g_ffn,
        "w_q": w_peer_q[0].astype(BF16),
        "k1": peer_keys1[0].astype(BF16),
        "k2": peer_keys2[0].astype(BF16),
        "uv": jnp.concatenate([peer_u[0], peer_v[0]], axis=1)[:, None, :],
        "g_ple": g_ple,
        "w_ple_gate": w_ple_gate[0].astype(BF16),
        "w_ple_proj": w_ple_proj[0].astype(BF16),
        "g_final": g_final[None, :],
    }


def kernel(x_prompt, x_sample, p_prompt, p_sample, state_ret, state_mlstm_C, state_mlstm_n, state_mlstm_m, g_mix, w_in, b_gates, g_ret, g_mlstm, w_out, g_ffn, w_peer_q, peer_keys1, peer_keys2, peer_u, peer_v, g_ple, w_ple_gate, w_ple_proj, g_final):
    w = _prep_weights(g_mix, w_in, b_gates, g_ret, g_mlstm, w_out, g_ffn, w_peer_q, peer_keys1, peer_keys2, peer_u, peer_v,
                      g_ple, w_ple_gate, w_ple_proj, g_final)
    bp, lp, _ = x_prompt.shape
    bs, ls, _ = x_sample.shape
    zeros = lambda *shape: jnp.zeros(shape, F32)
    past_len = 1024
    y_s, ret_s, c_s, n_s, m_s = _stream(
        x_sample, p_sample[0], past_len + jnp.arange(ls, dtype=F32),
        state_ret[0], state_mlstm_C[0], state_mlstm_n[0], state_mlstm_m[0], w)
    y_p, ret_p, c_p, n_p, m_p = _stream(
        x_prompt, p_prompt[0], jnp.arange(lp, dtype=F32),
        zeros(bp, RET_HEADS, RET_D, RET_D), zeros(bp, MLSTM_HEADS, MLSTM_DK, MLSTM_DV),
        zeros(bp, MLSTM_HEADS, MLSTM_DK), zeros(bp, MLSTM_HEADS), w)
    return (y_p, y_s, ret_p, c_p, n_p, m_p, ret_s, c_s, n_s, m_s)
```

```python
import functools
import math

import numpy as np
import jax
import jax.numpy as jnp
from jax import lax
from jax.experimental import pallas as pl
from jax.experimental.pallas import tpu as pltpu

F32 = jnp.float32
BF16 = jnp.bfloat16

D_MODEL = 2048
CHUNK = 64
RMS_EPS = 1e-6
ROPE_BASE = 10000.0
RET_HEADS = 8
RET_D = 128
MLSTM_HEADS = 4
MLSTM_DK = 128
MLSTM_DV = 256
Z_COLS = 7168
OFF_RQ, OFF_RK, OFF_RV, OFF_RG = 0, 1024, 2048, 3072
OFF_MQ, OFF_MK, OFF_MV, OFF_MO = 4096, 4608, 5120, 6144
PEER_HEADS = 8
N_KEYS = 128
PEER_TOPK = 16
PEER_K = PEER_HEADS * PEER_TOPK
PLE_DIM = 256
LANES = 128

LOG_GAMMA = [float(np.log(np.float32(1.0) - np.float32(2.0) ** np.float32(-5.0 - h))) for h in range(RET_HEADS)]

VMEM_LIMIT = 56 * 1024 * 1024


def _params(sem):
    return pltpu.CompilerParams(dimension_semantics=sem, vmem_limit_bytes=VMEM_LIMIT)


def _rms(x, g):
    return x * lax.rsqrt(jnp.mean(x * x, axis=-1, keepdims=True) + RMS_EPS) * g


def _mm(a, b):
    return jnp.dot(a.astype(BF16), b.astype(BF16), preferred_element_type=F32)


def _mm_nt(a, b):
    return lax.dot_general(a.astype(BF16), b.astype(BF16), (((1,), (1,)), ((), ())), preferred_element_type=F32)


def _mm_tn(a, b):
    return lax.dot_general(a.astype(BF16), b.astype(BF16), (((0,), (0,)), ((), ())), preferred_element_type=F32)


def _sigmoid(x):
    return 1.0 / (1.0 + jnp.exp(-x))


def _in_proj_kernel(x_ref, g_ref, w_ref, wg_ref, z_ref, gz_ref, a_scr):
    @pl.when(pl.program_id(1) == 0)
    def _():
        a_scr[...] = _rms(x_ref[...], g_ref[...]).astype(BF16)
        gz_ref[...] = jnp.dot(a_scr[...], wg_ref[...], preferred_element_type=F32)

    z_ref[...] = jnp.dot(a_scr[...], w_ref[...], preferred_element_type=F32)


def _in_proj(x, g, w, wg, tm=512, tn=1024):
    t = x.shape[0]
    return pl.pallas_call(
        _in_proj_kernel,
        grid=(t // tm, Z_COLS // tn),
        in_specs=[
            pl.BlockSpec((tm, D_MODEL), lambda i, j: (i, 0)),
            pl.BlockSpec((1, D_MODEL), lambda i, j: (0, 0)),
            pl.BlockSpec((D_MODEL, tn), lambda i, j: (0, j)),
            pl.BlockSpec((D_MODEL, LANES), lambda i, j: (0, 0)),
        ],
        out_specs=[
            pl.BlockSpec((tm, tn), lambda i, j: (i, j)),
            pl.BlockSpec((tm, LANES), lambda i, j: (i, 0)),
        ],
        out_shape=[jax.ShapeDtypeStruct((t, Z_COLS), F32), jax.ShapeDtypeStruct((t, LANES), F32)],
        scratch_shapes=[pltpu.VMEM((tm, D_MODEL), BF16)],
        compiler_params=_params(("arbitrary", "arbitrary")),
        name="in_proj",
    )(x, g, w, wg)


def _log_sigmoid(x):
    return -(jnp.maximum(-x, 0.0) + jnp.log1p(jnp.exp(-jnp.abs(x))))


def _mixers_kernel(z_ref, gz_ref, bias_ref, cos_ref, sin_ref, s0_ref, c0_ref, n0_ref, m0_ref, gret_ref, gml_ref,
                   mix_ref, s_ref, c_ref, n_ref, m_ref):
    @pl.when(pl.program_id(1) == 0)
    def _():
        s_ref[...] = s0_ref[...]
        c_ref[...] = c0_ref[...]
        n_ref[...] = n0_ref[...]
        m_ref[...] = m0_ref[...]

    cl = CHUNK
    row = lax.broadcasted_iota(jnp.int32, (cl, cl), 0)
    col = lax.broadcasted_iota(jnp.int32, (cl, cl), 1)
    causal = row >= col
    diff = jnp.where(causal, (row - col).astype(F32), 0.0)
    posc = lax.broadcasted_iota(jnp.int32, (cl, 1), 0).astype(F32)
    cosf = cos_ref[...]
    sinf = sin_ref[...]

    def rope(x):
        return x * cosf + pltpu.roll(x, RET_D // 2, axis=1) * sinf

    for h in range(RET_HEADS):
        lg = LOG_GAMMA[h]
        lo = h * RET_D
        q = rope(z_ref[0, :, OFF_RQ + lo:OFF_RQ + lo + RET_D])
        k = rope(z_ref[0, :, OFF_RK + lo:OFF_RK + lo + RET_D]) * (RET_D ** -0.5)
        v = z_ref[0, :, OFF_RV + lo:OFF_RV + lo + RET_D]
        rg = z_ref[0, :, OFF_RG + lo:OFF_RG + lo + RET_D]
        decay = jnp.where(causal, jnp.exp(diff * lg), 0.0)
        s_old = s_ref[0, h]
        scores = _mm_nt(q, k) * decay
        q_dec = q * jnp.exp((posc + 1.0) * lg)
        k_dec = k * jnp.exp((cl - 1.0 - posc) * lg)
        o = _mm(scores, v) + _mm(q_dec, s_old)
        s_ref[0, h] = math.exp(cl * lg) * s_old + _mm_tn(k_dec, v)
        y = o * lax.rsqrt(jnp.mean(o * o, axis=-1, keepdims=True) + RMS_EPS) * gret_ref[:, lo:lo + RET_D]
        mix_ref[0, :, lo:lo + RET_D] = (y * (rg * _sigmoid(rg))).astype(BF16)

    gates = gz_ref[0] + bias_ref[...]
    tri = causal.astype(F32)
    bcum = jnp.dot(tri, _log_sigmoid(gates), preferred_element_type=F32, precision=lax.Precision.HIGHEST)
    bcum_t = bcum.T
    gates_t = gates.T
    for h in range(MLSTM_HEADS):
        q = z_ref[0, :, OFF_MQ + h * MLSTM_DK:OFF_MQ + (h + 1) * MLSTM_DK]
        k = z_ref[0, :, OFF_MK + h * MLSTM_DK:OFF_MK + (h + 1) * MLSTM_DK] * (MLSTM_DK ** -0.5)
        v = z_ref[0, :, OFF_MV + h * MLSTM_DV:OFF_MV + (h + 1) * MLSTM_DV]
        mo = z_ref[0, :, OFF_MO + h * MLSTM_DV:OFF_MO + (h + 1) * MLSTM_DV]
        f = MLSTM_HEADS + h
        b_col = bcum[:, f:f + 1]
        b_row = bcum_t[f:f + 1, :]
        ig_col = gates[:, h:h + 1]
        ig_row = gates_t[h:h + 1, :]
        m_prev = m_ref[0, h:h + 1, 0:1]
        dlog = jnp.where(causal, b_col - b_row + ig_row, -jnp.inf)
        inter_log = b_col + m_prev
        m_t = jnp.maximum(inter_log, jnp.max(dlog, axis=-1, keepdims=True))
        dw = jnp.exp(dlog - m_t)
        inter_w = jnp.exp(inter_log - m_t)
        c_old = c_ref[0, h]
        n_old = n_ref[0, h:h + 1, :]
        sm = _mm_nt(q, k) * dw
        num = _mm(sm, v) + inter_w * _mm(q, c_old)
        den = jnp.sum(sm, axis=-1, keepdims=True) + inter_w * jnp.sum(q * n_old, axis=-1, keepdims=True)
        hh = num / jnp.maximum(jnp.abs(den), jnp.exp(-m_t))
        m_new = m_t[cl - 1:cl, :]
        b_last = b_col[cl - 1:cl, :]
        ws = jnp.exp(b_last - b_col + ig_col - m_new)
        carry = jnp.exp(b_last + m_prev - m_new)
        kw = k * ws
        c_ref[0, h] = carry * c_old + _mm_tn(kw, v)
        n_ref[0, h:h + 1, :] = carry * n_old + jnp.sum(kw, axis=0, keepdims=True)
        m_ref[0, h:h + 1, :] = jnp.broadcast_to(m_new, (1, LANES))
        y = hh * lax.rsqrt(jnp.mean(hh * hh, axis=-1, keepdims=True) + RMS_EPS) * gml_ref[:, h * MLSTM_DV:(h + 1) * MLSTM_DV]
        lo = RET_HEADS * RET_D + h * MLSTM_DV
        mix_ref[0, :, lo:lo + MLSTM_DV] = (y * _sigmoid(mo)).astype(BF16)


def _mixers(z, gz, bias, cosf, sinf, s0, c0, n0, m0, g_ret, g_ml):
    b, l, _ = z.shape
    per_b = lambda *tail: (lambda i, c: (i,) + tail)
    return pl.pallas_call(
        _mixers_kernel,
        grid=(b, l // CHUNK),
        in_specs=[
            pl.BlockSpec((1, CHUNK, Z_COLS), lambda i, c: (i, c, 0)),
            pl.BlockSpec((1, CHUNK, LANES), lambda i, c: (i, c, 0)),
            pl.BlockSpec((1, LANES), lambda i, c: (0, 0)),
            pl.BlockSpec((CHUNK, RET_D), lambda i, c: (c, 0)),
            pl.BlockSpec((CHUNK, RET_D), lambda i, c: (c, 0)),
            pl.BlockSpec((1, RET_HEADS, RET_D, RET_D), per_b(0, 0, 0)),
            pl.BlockSpec((1, MLSTM_HEADS, MLSTM_DK, MLSTM_DV), per_b(0, 0, 0)),
            pl.BlockSpec((1, MLSTM_HEADS, MLSTM_DK), per_b(0, 0)),
            pl.BlockSpec((1, MLSTM_HEADS, LANES), per_b(0, 0)),
            pl.BlockSpec((1, RET_HEADS * RET_D), lambda i, c: (0, 0)),
            pl.BlockSpec((1, MLSTM_HEADS * MLSTM_DV), lambda i, c: (0, 0)),
        ],
        out_specs=[
            pl.BlockSpec((1, CHUNK, D_MODEL), lambda i, c: (i, c, 0)),
            pl.BlockSpec((1, RET_HEADS, RET_D, RET_D), per_b(0, 0, 0)),
            pl.BlockSpec((1, MLSTM_HEADS, MLSTM_DK, MLSTM_DV), per_b(0, 0, 0)),
            pl.BlockSpec((1, MLSTM_HEADS, MLSTM_DK), per_b(0, 0)),
            pl.BlockSpec((1, MLSTM_HEADS, LANES), per_b(0, 0)),
        ],
        out_shape=[
            jax.ShapeDtypeStruct((b, l, D_MODEL), BF16),
            jax.ShapeDtypeStruct((b, RET_HEADS, RET_D, RET_D), F32),
            jax.ShapeDtypeStruct((b, MLSTM_HEADS, MLSTM_DK, MLSTM_DV), F32),
            jax.ShapeDtypeStruct((b, MLSTM_HEADS, MLSTM_DK), F32),
            jax.ShapeDtypeStruct((b, MLSTM_HEADS, LANES), F32),
        ],
        compiler_params=_params(("arbitrary", "arbitrary")),
        name="mixers",
    )(z, gz, bias, cosf, sinf, s0, c0, n0, m0, g_ret, g_ml)


def _out_proj_kernel(a_ref, w_ref, r_ref, o_ref):
    o_ref[...] = r_ref[...] + jnp.dot(a_ref[...], w_ref[...], preferred_element_type=F32)


def _out_proj(a, w, r, tm=512):
    t = a.shape[0]
    return pl.pallas_call(
        _out_proj_kernel,
        grid=(t // tm,),
        in_specs=[
            pl.BlockSpec((tm, D_MODEL), lambda i: (i, 0)),
            pl.BlockSpec((D_MODEL, D_MODEL), lambda i: (0, 0)),
            pl.BlockSpec((tm, D_MODEL), lambda i: (i, 0)),
        ],
        out_specs=pl.BlockSpec((tm, D_MODEL), lambda i: (i, 0)),
        out_shape=jax.ShapeDtypeStruct((t, D_MODEL), F32),
        compiler_params=_params(("arbitrary",)),
        name="out_proj",
    )(a, w, r)


def _top16(x_ref, payload_ref, val_ref, idx_ref):
    x = x_ref[...]
    rows = x.shape[0]
    iota = lax.broadcasted_iota(jnp.int32, x.shape, 0)
    for r in range(PEER_TOPK):
        m = jnp.max(x, axis=0, keepdims=True)
        am = jnp.min(jnp.where(x == m, iota, rows), axis=0, keepdims=True)
        sel = iota == am
        val_ref[r:r + 1, :] = m
        if payload_ref is None:
            idx_ref[r:r + 1, :] = am
        else:
            idx_ref[r:r + 1, :] = jnp.max(jnp.where(sel, payload_ref[...], -1), axis=0, keepdims=True)
        x = jnp.where(sel, -jnp.inf, x)


def _route_kernel(h_ref, g_ref, wq_ref, k1_ref, k2_ref, idx_ref, gate_ref,
                  q_scr, s_scr, t1_scr, i1_scr, t2_scr, i2_scr, cand_scr, cidx_scr, sc_scr, e_scr, et_scr, gt_scr):
    tm = h_ref.shape[0]
    c = _rms(h_ref[...], g_ref[...]).astype(BF16)
    q_scr[...] = jnp.dot(c, wq_ref[...], preferred_element_type=F32).astype(BF16)
    half = N_KEYS
    for h in range(PEER_HEADS):
        for sub in range(tm // LANES):
            tok = slice(sub * LANES, (sub + 1) * LANES)
            q1 = q_scr[tok, 2 * h * half:(2 * h + 1) * half]
            q2 = q_scr[tok, (2 * h + 1) * half:(2 * h + 2) * half]
            s_scr[...] = _mm_nt(k1_ref[...], q1)
            _top16(s_scr, None, t1_scr, i1_scr)
            s_scr[...] = _mm_nt(k2_ref[...], q2)
            _top16(s_scr, None, t2_scr, i2_scr)
            t2 = t2_scr[...]
            i2 = i2_scr[...]
            for a in range(PEER_TOPK):
                cand_scr[a * PEER_TOPK:(a + 1) * PEER_TOPK, :] = t1_scr[a:a + 1, :] + t2
                cidx_scr[a * PEER_TOPK:(a + 1) * PEER_TOPK, :] = i1_scr[a:a + 1, :] * N_KEYS + i2
            _top16(cand_scr, cidx_scr, sc_scr, e_scr)
            sc = sc_scr[...]
            p = jnp.exp(sc - sc[0:1, :])
            gt_scr[h * PEER_TOPK:(h + 1) * PEER_TOPK, tok] = p / jnp.sum(p, axis=0, keepdims=True)
            et_scr[h * PEER_TOPK:(h + 1) * PEER_TOPK, tok] = e_scr[...]
    idx_ref[...] = et_scr[...].T
    gate_ref[...] = gt_scr[...].T


def _route(h, g, wq, k1, k2, tm=256):
    t = h.shape[0]
    return pl.pallas_call(
        _route_kernel,
        grid=(t // tm,),
        in_specs=[
            pl.BlockSpec((tm, D_MODEL), lambda i: (i, 0)),
            pl.BlockSpec((1, D_MODEL), lambda i: (0, 0)),
            pl.BlockSpec((D_MODEL, D_MODEL), lambda i: (0, 0)),
            pl.BlockSpec((N_KEYS, N_KEYS), lambda i: (0, 0)),
            pl.BlockSpec((N_KEYS, N_KEYS), lambda i: (0, 0)),
        ],
        out_specs=[
            pl.BlockSpec((tm, PEER_K), lambda i: (i, 0)),
            pl.BlockSpec((tm, PEER_K), lambda i: (i, 0)),
        ],
        out_shape=[jax.ShapeDtypeStruct((t, PEER_K), jnp.int32), jax.ShapeDtypeStruct((t, PEER_K), F32)],
        scratch_shapes=[
            pltpu.VMEM((tm, D_MODEL), BF16),
            pltpu.VMEM((N_KEYS, LANES), F32),
            pltpu.VMEM((PEER_TOPK, LANES), F32),
            pltpu.VMEM((PEER_TOPK, LANES), jnp.int32),
            pltpu.VMEM((PEER_TOPK, LANES), F32),
            pltpu.VMEM((PEER_TOPK, LANES), jnp.int32),
            pltpu.VMEM((PEER_TOPK * PEER_TOPK, LANES), F32),
            pltpu.VMEM((PEER_TOPK * PEER_TOPK, LANES), jnp.int32),
            pltpu.VMEM((PEER_TOPK, LANES), F32),
            pltpu.VMEM((PEER_TOPK, LANES), jnp.int32),
            pltpu.VMEM((PEER_K, tm), jnp.int32),
            pltpu.VMEM((PEER_K, tm), F32),
        ],
        compiler_params=_params(("arbitrary",)),
        name="route",
    )(h, g, wq, k1, k2)


PEER_TB = 256
PEER_SLOTS = 4
PEER_LEAD = PEER_SLOTS - 1
HALF_D = D_MODEL // 2
PACK_TILES = HALF_D // LANES
INV_SQRT2 = 0.7071067811865476


def _pack_bf16_pairs(a):
    bits = lax.bitcast_convert_type(a.astype(BF16), jnp.uint16).astype(jnp.uint32)
    return bits[:, :HALF_D] | (bits[:, HALF_D:] << 16)


def _peer_kernel(idx_hbm, h_ref, g_ref, gate_ref, uv_hbm, o_ref, idx_s, c_scr, rows0, rows1, rows2, rows3, idx_sem, row_sems):
    rows = (rows0, rows1, rows2, rows3)
    tb = h_ref.shape[0]
    step = pl.program_id(0)
    idx_copy = pltpu.make_async_copy(idx_hbm.at[pl.ds(step * (tb * PEER_K), tb * PEER_K)], idx_s, idx_sem)
    idx_copy.start()
    c_scr[...] = _rms(h_ref[...], g_ref[...])
    idx_copy.wait()

    def issue(t, slot, k0, k1):
        for k in range(k0, k1):
            e = idx_s[t * PEER_K + k]
            pltpu.make_async_copy(uv_hbm.at[e], rows[slot].at[pl.ds(k, 1), :], row_sems.at[slot]).start(priority=k % 2)

    def wait(slot):
        pltpu.make_async_copy(uv_hbm.at[pl.ds(0, PEER_K), 0], rows[slot], row_sems.at[slot]).wait()

    def unpack(words):
        lo = lax.bitcast_convert_type(jnp.left_shift(words, jnp.uint32(16)), F32)
        hi = lax.bitcast_convert_type(jnp.bitwise_and(words, jnp.uint32(0xFFFF0000)), F32)
        return lo, hi

    eye = lax.broadcasted_iota(jnp.int32, (PEER_K, LANES), 0) == lax.broadcasted_iota(jnp.int32, (PEER_K, LANES), 1)
    per_tile = PEER_K // PACK_TILES

    def token(t, slot, prefetch):
        wait(slot)
        ahead_slot = (slot + PEER_LEAD) % PEER_SLOTS
        x = c_scr[pl.ds(t, 1), :]
        acc = None
        for j in range(PACK_TILES):
            if prefetch:
                issue(t + PEER_LEAD, ahead_slot, j * per_tile, (j + 1) * per_tile)
            lo, hi = unpack(rows[slot][:, j * LANES:(j + 1) * LANES])
            p = lo * x[:, j * LANES:(j + 1) * LANES] + hi * x[:, HALF_D + j * LANES:HALF_D + (j + 1) * LANES]
            acc = p if acc is None else acc + p
        act = jnp.sum(acc, axis=1, keepdims=True)
        gate_col = jnp.sum(jnp.where(eye, gate_ref[pl.ds(t, 1), :], 0.0), axis=1, keepdims=True)
        coef = gate_col * (0.5 * act * (1.0 + lax.erf(act * INV_SQRT2)))
        mixed_lo, mixed_hi = [], []
        for j in range(PACK_TILES):
            lo, hi = unpack(rows[slot][:, HALF_D + j * LANES:HALF_D + (j + 1) * LANES])
            mixed_lo.append(jnp.sum(lo * coef, axis=0, keepdims=True))
            mixed_hi.append(jnp.sum(hi * coef, axis=0, keepdims=True))
        o_ref[pl.ds(t, 1), :] = h_ref[pl.ds(t, 1), :] + jnp.concatenate(mixed_lo + mixed_hi, axis=1)

    for s in range(PEER_LEAD):
        issue(s, s, 0, PEER_K)

    def group(i, carry):
        for s in range(PEER_SLOTS):
            token(i * PEER_SLOTS + s, s, True)
        return carry

    lax.fori_loop(0, tb // PEER_SLOTS - 1, group, 0)
    last = tb - PEER_SLOTS
    token(last, 0, True)
    for s in range(1, PEER_SLOTS):
        token(last + s, s, False)


def _peer(idx_flat, h, g, gate, uv):
    t = h.shape[0]
    tb = PEER_TB
    return pl.pallas_call(
        _peer_kernel,
        grid=(t // tb,),
        in_specs=[
            pl.BlockSpec(memory_space=pl.ANY),
            pl.BlockSpec((tb, D_MODEL), lambda i: (i, 0)),
            pl.BlockSpec((1, D_MODEL), lambda i: (0, 0)),
            pl.BlockSpec((tb, PEER_K), lambda i: (i, 0)),
            pl.BlockSpec(memory_space=pl.ANY),
        ],
        out_specs=pl.BlockSpec((tb, D_MODEL), lambda i: (i, 0)),
        out_shape=jax.ShapeDtypeStruct((t, D_MODEL), F32),
        scratch_shapes=[
            pltpu.SMEM((tb * PEER_K,), jnp.int32),
            pltpu.VMEM((tb, D_MODEL), F32),
        ] + [pltpu.VMEM((PEER_K, D_MODEL), jnp.uint32)] * PEER_SLOTS + [
            pltpu.SemaphoreType.DMA(()),
            pltpu.SemaphoreType.DMA((PEER_SLOTS,)),
        ],
        compiler_params=_params(("arbitrary",)),
        name="peer",
    )(idx_flat, h, g, gate, uv)


def _ple_kernel(h_ref, p_ref, gple_ref, wg_ref, wp_ref, gfin_ref, y_ref):
    h = h_ref[...]
    e = _rms(h, gple_ref[...]).astype(BF16)
    gate = _sigmoid(jnp.dot(e, wg_ref[...], preferred_element_type=F32))
    proj = jnp.dot(p_ref[...].astype(BF16), wp_ref[...], preferred_element_type=F32)
    y_ref[...] = _rms(h + gate * proj, gfin_ref[...])


def _ple(h, p, g_ple, wg, wp, g_fin, tm=256):
    t = h.shape[0]
    return pl.pallas_call(
        _ple_kernel,
        grid=(t // tm,),
        in_specs=[
            pl.BlockSpec((tm, D_MODEL), lambda i: (i, 0)),
            pl.BlockSpec((tm, PLE_DIM), lambda i: (i, 0)),
            pl.BlockSpec((1, D_MODEL), lambda i: (0, 0)),
            pl.BlockSpec((D_MODEL, D_MODEL), lambda i: (0, 0)),
            pl.BlockSpec((PLE_DIM, D_MODEL), lambda i: (0, 0)),
            pl.BlockSpec((1, D_MODEL), lambda i: (0, 0)),
        ],
        out_specs=pl.BlockSpec((tm, D_MODEL), lambda i: (i, 0)),
        out_shape=jax.ShapeDtypeStruct((t, D_MODEL), F32),
        compiler_params=_params(("arbitrary",)),
        name="ple",
    )(h, p, g_ple, wg, wp, g_fin)


def _rope_tables(pos):
    half = RET_D // 2
    inv = ROPE_BASE ** (-jnp.arange(half, dtype=F32) / half)
    ang = pos[:, None] * inv[None, :]
    cos, sin = jnp.cos(ang), jnp.sin(ang)
    return jnp.concatenate([cos, cos], axis=-1), jnp.concatenate([-sin, sin], axis=-1)


def _stream(x, p, pos, s0, c0, n0, m0, w):
    b, l, _ = x.shape
    t = b * l
    x2 = x.reshape(t, D_MODEL)
    z, gz = _in_proj(x2, w["g_mix"], w["w_in"], w["w_gates"])
    cosf, sinf = _rope_tables(pos)
    m0b = jnp.broadcast_to(m0[:, :, None], (b, MLSTM_HEADS, LANES))
    mix, s_new, c_new, n_new, m_new = _mixers(
        z.reshape(b, l, Z_COLS), gz.reshape(b, l, LANES), w["bias"], cosf, sinf, s0, c0, n0, m0b, w["g_ret"], w["g_ml"])
    h1 = _out_proj(mix.reshape(t, D_MODEL), w["w_out"], x2)
    idx, gate = _route(h1, w["g_ffn"], w["w_q"], w["k1"], w["k2"])
    h2 = _peer(idx.reshape(t * PEER_K), h1, w["g_ffn"], gate, w["uv"])
    y = _ple(h2, p.reshape(t, PLE_DIM), w["g_ple"], w["w_ple_gate"], w["w_ple_proj"], w["g_final"])
    return y.reshape(b, l, D_MODEL), s_new[None], c_new[None], n_new[None], m_new[None, :, :, 0]


def _prep_weights(g_mix, w_in, b_gates, g_ret, g_mlstm, w_out, g_ffn, w_peer_q, peer_keys1, peer_keys2, peer_u, peer_v,
                  g_ple, w_ple_gate, w_ple_proj, g_final):
    w_in0 = w_in[0]
    return {
        "g_mix": g_mix,
        "w_in": w_in0[:, :Z_COLS].astype(BF16),
        "w_gates": jnp.pad(w_in0[:, Z_COLS:], ((0, 0), (0, LANES - 2 * MLSTM_HEADS))).astype(BF16),
        "bias": jnp.pad(b_gates, ((0, 0), (0, LANES - 2 * MLSTM_HEADS))),
        "g_ret": g_ret,
        "g_ml": g_mlstm,
        "w_out": w_out[0].astype(BF16),
        "g_ffn": g_ffn,
        "w_q": w_peer_q[0].astype(BF16),
        "k1": peer_keys1[0].astype(BF16),
        "k2": peer_keys2[0].astype(BF16),
        "uv": jnp.concatenate([_pack_bf16_pairs(peer_u[0]), _pack_bf16_pairs(peer_v[0])], axis=1)[:, None, :],
        "g_ple": g_ple,
        "w_ple_gate": w_ple_gate[0].astype(BF16),
        "w_ple_proj": w_ple_proj[0].astype(BF16),
        "g_final": g_final[None, :],
    }


def kernel(x_prompt, x_sample, p_prompt, p_sample, state_ret, state_mlstm_C, state_mlstm_n, state_mlstm_m, g_mix, w_in, b_gates, g_ret, g_mlstm, w_out, g_ffn, w_peer_q, peer_keys1, peer_keys2, peer_u, peer_v, g_ple, w_ple_gate, w_ple_proj, g_final):
    w = _prep_weights(g_mix, w_in, b_gates, g_ret, g_mlstm, w_out, g_ffn, w_peer_q, peer_keys1, peer_keys2, peer_u, peer_v,
                      g_ple, w_ple_gate, w_ple_proj, g_final)
    bp, lp, _ = x_prompt.shape
    bs, ls, _ = x_sample.shape
    zeros = lambda *shape: jnp.zeros(shape, F32)
    past_len = 1024
    y_s, ret_s, c_s, n_s, m_s = _stream(
        x_sample, p_sample[0], past_len + jnp.arange(ls, dtype=F32),
        state_ret[0], state_mlstm_C[0], state_mlstm_n[0], state_mlstm_m[0], w)
    y_p, ret_p, c_p, n_p, m_p = _stream(
        x_prompt, p_prompt[0], jnp.arange(lp, dtype=F32),
        zeros(bp, RET_HEADS, RET_D, RET_D), zeros(bp, MLSTM_HEADS, MLSTM_DK, MLSTM_DV),
        zeros(bp, MLSTM_HEADS, MLSTM_DK), zeros(bp, MLSTM_HEADS), w)
    return (y_p, y_s, ret_p, c_p, n_p, m_p, ret_s, c_s, n_s, m_s)
```

```python
import functools
import math

import numpy as np
import jax
import jax.numpy as jnp
from jax import lax
from jax.experimental import pallas as pl
from jax.experimental.pallas import tpu as pltpu

F32 = jnp.float32
BF16 = jnp.bfloat16

D_MODEL = 2048
CHUNK = 64
RMS_EPS = 1e-6
ROPE_BASE = 10000.0
RET_HEADS = 8
RET_D = 128
MLSTM_HEADS = 4
MLSTM_DK = 128
MLSTM_DV = 256
Z_COLS = 7168
OFF_RQ, OFF_RK, OFF_RV, OFF_RG = 0, 1024, 2048, 3072
OFF_MQ, OFF_MK, OFF_MV, OFF_MO = 4096, 4608, 5120, 6144
PEER_HEADS = 8
N_KEYS = 128
PEER_TOPK = 16
PEER_K = PEER_HEADS * PEER_TOPK
PLE_DIM = 256
LANES = 128

LOG_GAMMA = [float(np.log(np.float32(1.0) - np.float32(2.0) ** np.float32(-5.0 - h))) for h in range(RET_HEADS)]

VMEM_LIMIT = 56 * 1024 * 1024


def _params(sem):
    return pltpu.CompilerParams(dimension_semantics=sem, vmem_limit_bytes=VMEM_LIMIT)


def _rms(x, g):
    return x * lax.rsqrt(jnp.mean(x * x, axis=-1, keepdims=True) + RMS_EPS) * g


def _mm(a, b):
    return jnp.dot(a.astype(BF16), b.astype(BF16), preferred_element_type=F32)


def _mm_nt(a, b):
    return lax.dot_general(a.astype(BF16), b.astype(BF16), (((1,), (1,)), ((), ())), preferred_element_type=F32)


def _mm_tn(a, b):
    return lax.dot_general(a.astype(BF16), b.astype(BF16), (((0,), (0,)), ((), ())), preferred_element_type=F32)


def _sigmoid(x):
    return 1.0 / (1.0 + jnp.exp(-x))


def _in_proj_kernel(x_ref, g_ref, w_ref, wg_ref, z_ref, gz_ref, a_scr):
    @pl.when(pl.program_id(1) == 0)
    def _():
        a_scr[...] = _rms(x_ref[...], g_ref[...]).astype(BF16)
        gz_ref[...] = jnp.dot(a_scr[...], wg_ref[...], preferred_element_type=F32)

    z_ref[...] = jnp.dot(a_scr[...], w_ref[...], preferred_element_type=F32)


def _in_proj(x, g, w, wg, tm=512, tn=1024):
    t = x.shape[0]
    return pl.pallas_call(
        _in_proj_kernel,
        grid=(t // tm, Z_COLS // tn),
        in_specs=[
            pl.BlockSpec((tm, D_MODEL), lambda i, j: (i, 0)),
            pl.BlockSpec((1, D_MODEL), lambda i, j: (0, 0)),
            pl.BlockSpec((D_MODEL, tn), lambda i, j: (0, j)),
            pl.BlockSpec((D_MODEL, LANES), lambda i, j: (0, 0)),
        ],
        out_specs=[
            pl.BlockSpec((tm, tn), lambda i, j: (i, j)),
            pl.BlockSpec((tm, LANES), lambda i, j: (i, 0)),
        ],
        out_shape=[jax.ShapeDtypeStruct((t, Z_COLS), F32), jax.ShapeDtypeStruct((t, LANES), F32)],
        scratch_shapes=[pltpu.VMEM((tm, D_MODEL), BF16)],
        compiler_params=_params(("arbitrary", "arbitrary")),
        name="in_proj",
    )(x, g, w, wg)


def _log_sigmoid(x):
    return -(jnp.maximum(-x, 0.0) + jnp.log1p(jnp.exp(-jnp.abs(x))))


def _mixers_kernel(z_ref, gz_ref, bias_ref, cos_ref, sin_ref, s0_ref, c0_ref, n0_ref, m0_ref, gret_ref, gml_ref,
                   mix_ref, s_ref, c_ref, n_ref, m_ref):
    @pl.when(pl.program_id(1) == 0)
    def _():
        s_ref[...] = s0_ref[...]
        c_ref[...] = c0_ref[...]
        n_ref[...] = n0_ref[...]
        m_ref[...] = m0_ref[...]

    cl = CHUNK
    row = lax.broadcasted_iota(jnp.int32, (cl, cl), 0)
    col = lax.broadcasted_iota(jnp.int32, (cl, cl), 1)
    causal = row >= col
    diff = jnp.where(causal, (row - col).astype(F32), 0.0)
    posc = lax.broadcasted_iota(jnp.int32, (cl, 1), 0).astype(F32)
    cosf = cos_ref[...]
    sinf = sin_ref[...]

    def rope(x):
        return x * cosf + pltpu.roll(x, RET_D // 2, axis=1) * sinf

    for h in range(RET_HEADS):
        lg = LOG_GAMMA[h]
        lo = h * RET_D
        q = rope(z_ref[0, :, OFF_RQ + lo:OFF_RQ + lo + RET_D])
        k = rope(z_ref[0, :, OFF_RK + lo:OFF_RK + lo + RET_D]) * (RET_D ** -0.5)
        v = z_ref[0, :, OFF_RV + lo:OFF_RV + lo + RET_D]
        rg = z_ref[0, :, OFF_RG + lo:OFF_RG + lo + RET_D]
        decay = jnp.where(causal, jnp.exp(diff * lg), 0.0)
        s_old = s_ref[0, h]
        scores = _mm_nt(q, k) * decay
        q_dec = q * jnp.exp((posc + 1.0) * lg)
        k_dec = k * jnp.exp((cl - 1.0 - posc) * lg)
        o = _mm(scores, v) + _mm(q_dec, s_old)
        s_ref[0, h] = math.exp(cl * lg) * s_old + _mm_tn(k_dec, v)
        y = o * lax.rsqrt(jnp.mean(o * o, axis=-1, keepdims=True) + RMS_EPS) * gret_ref[:, lo:lo + RET_D]
        mix_ref[0, :, lo:lo + RET_D] = (y * (rg * _sigmoid(rg))).astype(BF16)

    gates = gz_ref[0] + bias_ref[...]
    tri = causal.astype(F32)
    bcum = jnp.dot(tri, _log_sigmoid(gates), preferred_element_type=F32, precision=lax.Precision.HIGHEST)
    bcum_t = bcum.T
    gates_t = gates.T
    for h in range(MLSTM_HEADS):
        q = z_ref[0, :, OFF_MQ + h * MLSTM_DK:OFF_MQ + (h + 1) * MLSTM_DK]
        k = z_ref[0, :, OFF_MK + h * MLSTM_DK:OFF_MK + (h + 1) * MLSTM_DK] * (MLSTM_DK ** -0.5)
        v = z_ref[0, :, OFF_MV + h * MLSTM_DV:OFF_MV + (h + 1) * MLSTM_DV]
        mo = z_ref[0, :, OFF_MO + h * MLSTM_DV:OFF_MO + (h + 1) * MLSTM_DV]
        f = MLSTM_HEADS + h
        b_col = bcum[:, f:f + 1]
        b_row = bcum_t[f:f + 1, :]
        ig_col = gates[:, h:h + 1]
        ig_row = gates_t[h:h + 1, :]
        m_prev = m_ref[0, h:h + 1, 0:1]
        dlog = jnp.where(causal, b_col - b_row + ig_row, -jnp.inf)
        inter_log = b_col + m_prev
        m_t = jnp.maximum(inter_log, jnp.max(dlog, axis=-1, keepdims=True))
        dw = jnp.exp(dlog - m_t)
        inter_w = jnp.exp(inter_log - m_t)
        c_old = c_ref[0, h]
        n_old = n_ref[0, h:h + 1, :]
        sm = _mm_nt(q, k) * dw
        num = _mm(sm, v) + inter_w * _mm(q, c_old)
        den = jnp.sum(sm, axis=-1, keepdims=True) + inter_w * jnp.sum(q * n_old, axis=-1, keepdims=True)
        hh = num / jnp.maximum(jnp.abs(den), jnp.exp(-m_t))
        m_new = m_t[cl - 1:cl, :]
        b_last = b_col[cl - 1:cl, :]
        ws = jnp.exp(b_last - b_col + ig_col - m_new)
        carry = jnp.exp(b_last + m_prev - m_new)
        kw = k * ws
        c_ref[0, h] = carry * c_old + _mm_tn(kw, v)
        n_ref[0, h:h + 1, :] = carry * n_old + jnp.sum(kw, axis=0, keepdims=True)
        m_ref[0, h:h + 1, :] = jnp.broadcast_to(m_new, (1, LANES))
        y = hh * lax.rsqrt(jnp.mean(hh * hh, axis=-1, keepdims=True) + RMS_EPS) * gml_ref[:, h * MLSTM_DV:(h + 1) * MLSTM_DV]
        lo = RET_HEADS * RET_D + h * MLSTM_DV
        mix_ref[0, :, lo:lo + MLSTM_DV] = (y * _sigmoid(mo)).astype(BF16)


def _mixers(z, gz, bias, cosf, sinf, s0, c0, n0, m0, g_ret, g_ml):
    b, l, _ = z.shape
    per_b = lambda *tail: (lambda i, c: (i,) + tail)
    return pl.pallas_call(
        _mixers_kernel,
        grid=(b, l // CHUNK),
        in_specs=[
            pl.BlockSpec((1, CHUNK, Z_COLS), lambda i, c: (i, c, 0)),
            pl.BlockSpec((1, CHUNK, LANES), lambda i, c: (i, c, 0)),
            pl.BlockSpec((1, LANES), lambda i, c: (0, 0)),
            pl.BlockSpec((CHUNK, RET_D), lambda i, c: (c, 0)),
            pl.BlockSpec((CHUNK, RET_D), lambda i, c: (c, 0)),
            pl.BlockSpec((1, RET_HEADS, RET_D, RET_D), per_b(0, 0, 0)),
            pl.BlockSpec((1, MLSTM_HEADS, MLSTM_DK, MLSTM_DV), per_b(0, 0, 0)),
            pl.BlockSpec((1, MLSTM_HEADS, MLSTM_DK), per_b(0, 0)),
            pl.BlockSpec((1, MLSTM_HEADS, LANES), per_b(0, 0)),
            pl.BlockSpec((1, RET_HEADS * RET_D), lambda i, c: (0, 0)),
            pl.BlockSpec((1, MLSTM_HEADS * MLSTM_DV), lambda i, c: (0, 0)),
        ],
        out_specs=[
            pl.BlockSpec((1, CHUNK, D_MODEL), lambda i, c: (i, c, 0)),
            pl.BlockSpec((1, RET_HEADS, RET_D, RET_D), per_b(0, 0, 0)),
            pl.BlockSpec((1, MLSTM_HEADS, MLSTM_DK, MLSTM_DV), per_b(0, 0, 0)),
            pl.BlockSpec((1, MLSTM_HEADS, MLSTM_DK), per_b(0, 0)),
            pl.BlockSpec((1, MLSTM_HEADS, LANES), per_b(0, 0)),
        ],
        out_shape=[
            jax.ShapeDtypeStruct((b, l, D_MODEL), BF16),
            jax.ShapeDtypeStruct((b, RET_HEADS, RET_D, RET_D), F32),
            jax.ShapeDtypeStruct((b, MLSTM_HEADS, MLSTM_DK, MLSTM_DV), F32),
            jax.ShapeDtypeStruct((b, MLSTM_HEADS, MLSTM_DK), F32),
            jax.ShapeDtypeStruct((b, MLSTM_HEADS, LANES), F32),
        ],
        compiler_params=_params(("arbitrary", "arbitrary")),
        name="mixers",
    )(z, gz, bias, cosf, sinf, s0, c0, n0, m0, g_ret, g_ml)


def _out_proj_kernel(a_ref, w_ref, r_ref, o_ref):
    o_ref[...] = r_ref[...] + jnp.dot(a_ref[...], w_ref[...], preferred_element_type=F32)


def _out_proj(a, w, r, tm=512):
    t = a.shape[0]
    return pl.pallas_call(
        _out_proj_kernel,
        grid=(t // tm,),
        in_specs=[
            pl.BlockSpec((tm, D_MODEL), lambda i: (i, 0)),
            pl.BlockSpec((D_MODEL, D_MODEL), lambda i: (0, 0)),
            pl.BlockSpec((tm, D_MODEL), lambda i: (i, 0)),
        ],
        out_specs=pl.BlockSpec((tm, D_MODEL), lambda i: (i, 0)),
        out_shape=jax.ShapeDtypeStruct((t, D_MODEL), F32),
        compiler_params=_params(("arbitrary",)),
        name="out_proj",
    )(a, w, r)


def _top16(x_ref, payload_ref, val_ref, idx_ref):
    x = x_ref[...]
    rows = x.shape[0]
    iota = lax.broadcasted_iota(jnp.int32, x.shape, 0)
    for r in range(PEER_TOPK):
        m = jnp.max(x, axis=0, keepdims=True)
        am = jnp.min(jnp.where(x == m, iota, rows), axis=0, keepdims=True)
        sel = iota == am
        val_ref[r:r + 1, :] = m
        if payload_ref is None:
            idx_ref[r:r + 1, :] = am
        else:
            idx_ref[r:r + 1, :] = jnp.max(jnp.where(sel, payload_ref[...], -1), axis=0, keepdims=True)
        x = jnp.where(sel, -jnp.inf, x)


def _route_kernel(h_ref, g_ref, wq_ref, k1_ref, k2_ref, idx_ref, gate_ref,
                  q_scr, s_scr, t1_scr, i1_scr, t2_scr, i2_scr, cand_scr, cidx_scr, sc_scr, e_scr, et_scr, gt_scr):
    tm = h_ref.shape[0]
    c = _rms(h_ref[...], g_ref[...]).astype(BF16)
    q_scr[...] = jnp.dot(c, wq_ref[...], preferred_element_type=F32).astype(BF16)
    half = N_KEYS
    for h in range(PEER_HEADS):
        for sub in range(tm // LANES):
            tok = slice(sub * LANES, (sub + 1) * LANES)
            q1 = q_scr[tok, 2 * h * half:(2 * h + 1) * half]
            q2 = q_scr[tok, (2 * h + 1) * half:(2 * h + 2) * half]
            s_scr[...] = _mm_nt(k1_ref[...], q1)
            _top16(s_scr, None, t1_scr, i1_scr)
            s_scr[...] = _mm_nt(k2_ref[...], q2)
            _top16(s_scr, None, t2_scr, i2_scr)
            t2 = t2_scr[...]
            i2 = i2_scr[...]
            for a in range(PEER_TOPK):
                cand_scr[a * PEER_TOPK:(a + 1) * PEER_TOPK, :] = t1_scr[a:a + 1, :] + t2
                cidx_scr[a * PEER_TOPK:(a + 1) * PEER_TOPK, :] = i1_scr[a:a + 1, :] * N_KEYS + i2
            _top16(cand_scr, cidx_scr, sc_scr, e_scr)
            sc = sc_scr[...]
            p = jnp.exp(sc - sc[0:1, :])
            gt_scr[h * PEER_TOPK:(h + 1) * PEER_TOPK, tok] = p / jnp.sum(p, axis=0, keepdims=True)
            et_scr[h * PEER_TOPK:(h + 1) * PEER_TOPK, tok] = e_scr[...]
    idx_ref[...] = et_scr[...].T
    gate_ref[...] = gt_scr[...].T


def _route(h, g, wq, k1, k2, tm=256):
    t = h.shape[0]
    return pl.pallas_call(
        _route_kernel,
        grid=(t // tm,),
        in_specs=[
            pl.BlockSpec((tm, D_MODEL), lambda i: (i, 0)),
            pl.BlockSpec((1, D_MODEL), lambda i: (0, 0)),
            pl.BlockSpec((D_MODEL, D_MODEL), lambda i: (0, 0)),
            pl.BlockSpec((N_KEYS, N_KEYS), lambda i: (0, 0)),
            pl.BlockSpec((N_KEYS, N_KEYS), lambda i: (0, 0)),
        ],
        out_specs=[
            pl.BlockSpec((tm, PEER_K), lambda i: (i, 0)),
            pl.BlockSpec((tm, PEER_K), lambda i: (i, 0)),
        ],
        out_shape=[jax.ShapeDtypeStruct((t, PEER_K), jnp.int32), jax.ShapeDtypeStruct((t, PEER_K), F32)],
        scratch_shapes=[
            pltpu.VMEM((tm, D_MODEL), BF16),
            pltpu.VMEM((N_KEYS, LANES), F32),
            pltpu.VMEM((PEER_TOPK, LANES), F32),
            pltpu.VMEM((PEER_TOPK, LANES), jnp.int32),
            pltpu.VMEM((PEER_TOPK, LANES), F32),
            pltpu.VMEM((PEER_TOPK, LANES), jnp.int32),
            pltpu.VMEM((PEER_TOPK * PEER_TOPK, LANES), F32),
            pltpu.VMEM((PEER_TOPK * PEER_TOPK, LANES), jnp.int32),
            pltpu.VMEM((PEER_TOPK, LANES), F32),
            pltpu.VMEM((PEER_TOPK, LANES), jnp.int32),
            pltpu.VMEM((PEER_K, tm), jnp.int32),
            pltpu.VMEM((PEER_K, tm), F32),
        ],
        compiler_params=_params(("arbitrary",)),
        name="route",
    )(h, g, wq, k1, k2)


PEER_TB = 256
PEER_SLOTS = 8
PEER_LEAD = PEER_SLOTS - 1
HALF_D = D_MODEL // 2
PACK_TILES = HALF_D // LANES
INV_SQRT2 = 0.7071067811865476


def _pack_bf16_pairs(a):
    bits = lax.bitcast_convert_type(a.astype(BF16), jnp.uint16).astype(jnp.uint32)
    return bits[:, :HALF_D] | (bits[:, HALF_D:] << 16)


def _peer_kernel(idx_hbm, h_ref, g_ref, gate_ref, uv_hbm, o_ref, idx_s, c_scr, *scratch):
    rows = scratch[:PEER_SLOTS]
    idx_sem, row_sems = scratch[PEER_SLOTS:]
    tb = h_ref.shape[0]
    step = pl.program_id(0)
    idx_copy = pltpu.make_async_copy(idx_hbm.at[pl.ds(step * (tb * PEER_K), tb * PEER_K)], idx_s, idx_sem)
    idx_copy.start()
    c_scr[...] = _rms(h_ref[...], g_ref[...])
    idx_copy.wait()

    def issue(t, slot, k0, k1):
        for k in range(k0, k1):
            e = idx_s[t * PEER_K + k]
            pltpu.make_async_copy(uv_hbm.at[e], rows[slot].at[pl.ds(k, 1), :], row_sems.at[slot]).start(priority=k % 2)

    def wait(slot):
        pltpu.make_async_copy(uv_hbm.at[pl.ds(0, PEER_K), 0], rows[slot], row_sems.at[slot]).wait()

    def unpack(words):
        lo = lax.bitcast_convert_type(jnp.left_shift(words, jnp.uint32(16)), F32)
        hi = lax.bitcast_convert_type(jnp.bitwise_and(words, jnp.uint32(0xFFFF0000)), F32)
        return lo, hi

    eye = lax.broadcasted_iota(jnp.int32, (PEER_K, LANES), 0) == lax.broadcasted_iota(jnp.int32, (PEER_K, LANES), 1)
    per_tile = PEER_K // PACK_TILES

    def token(t, slot, prefetch):
        wait(slot)
        ahead_slot = (slot + PEER_LEAD) % PEER_SLOTS
        x = c_scr[pl.ds(t, 1), :]
        acc = None
        for j in range(PACK_TILES):
            if prefetch:
                issue(t + PEER_LEAD, ahead_slot, j * per_tile, (j + 1) * per_tile)
            lo, hi = unpack(rows[slot][:, j * LANES:(j + 1) * LANES])
            p = lo * x[:, j * LANES:(j + 1) * LANES] + hi * x[:, HALF_D + j * LANES:HALF_D + (j + 1) * LANES]
            acc = p if acc is None else acc + p
        act = jnp.sum(acc, axis=1, keepdims=True)
        gate_col = jnp.sum(jnp.where(eye, gate_ref[pl.ds(t, 1), :], 0.0), axis=1, keepdims=True)
        coef = gate_col * (0.5 * act * (1.0 + lax.erf(act * INV_SQRT2)))
        mixed_lo, mixed_hi = [], []
        for j in range(PACK_TILES):
            lo, hi = unpack(rows[slot][:, HALF_D + j * LANES:HALF_D + (j + 1) * LANES])
            mixed_lo.append(jnp.sum(lo * coef, axis=0, keepdims=True))
            mixed_hi.append(jnp.sum(hi * coef, axis=0, keepdims=True))
        o_ref[pl.ds(t, 1), :] = h_ref[pl.ds(t, 1), :] + jnp.concatenate(mixed_lo + mixed_hi, axis=1)

    for s in range(PEER_LEAD):
        issue(s, s, 0, PEER_K)

    def group(i, carry):
        for s in range(PEER_SLOTS):
            token(i * PEER_SLOTS + s, s, True)
        return carry

    lax.fori_loop(0, tb // PEER_SLOTS - 1, group, 0)
    last = tb - PEER_SLOTS
    token(last, 0, True)
    for s in range(1, PEER_SLOTS):
        token(last + s, s, False)


def _peer(idx_flat, h, g, gate, uv):
    t = h.shape[0]
    tb = PEER_TB
    return pl.pallas_call(
        _peer_kernel,
        grid=(t // tb,),
        in_specs=[
            pl.BlockSpec(memory_space=pl.ANY),
            pl.BlockSpec((tb, D_MODEL), lambda i: (i, 0)),
            pl.BlockSpec((1, D_MODEL), lambda i: (0, 0)),
            pl.BlockSpec((tb, PEER_K), lambda i: (i, 0)),
            pl.BlockSpec(memory_space=pl.ANY),
        ],
        out_specs=pl.BlockSpec((tb, D_MODEL), lambda i: (i, 0)),
        out_shape=jax.ShapeDtypeStruct((t, D_MODEL), F32),
        scratch_shapes=[
            pltpu.SMEM((tb * PEER_K,), jnp.int32),
            pltpu.VMEM((tb, D_MODEL), F32),
        ] + [pltpu.VMEM((PEER_K, D_MODEL), jnp.uint32)] * PEER_SLOTS + [
            pltpu.SemaphoreType.DMA(()),
            pltpu.SemaphoreType.DMA((PEER_SLOTS,)),
        ],
        compiler_params=_params(("arbitrary",)),
        name="peer",
    )(idx_flat, h, g, gate, uv)


def _ple_kernel(h_ref, p_ref, gple_ref, wg_ref, wp_ref, gfin_ref, y_ref):
    h = h_ref[...]
    e = _rms(h, gple_ref[...]).astype(BF16)
    gate = _sigmoid(jnp.dot(e, wg_ref[...], preferred_element_type=F32))
    proj = jnp.dot(p_ref[...].astype(BF16), wp_ref[...], preferred_element_type=F32)
    y_ref[...] = _rms(h + gate * proj, gfin_ref[...])


def _ple(h, p, g_ple, wg, wp, g_fin, tm=256):
    t = h.shape[0]
    return pl.pallas_call(
        _ple_kernel,
        grid=(t // tm,),
        in_specs=[
            pl.BlockSpec((tm, D_MODEL), lambda i: (i, 0)),
            pl.BlockSpec((tm, PLE_DIM), lambda i: (i, 0)),
            pl.BlockSpec((1, D_MODEL), lambda i: (0, 0)),
            pl.BlockSpec((D_MODEL, D_MODEL), lambda i: (0, 0)),
            pl.BlockSpec((PLE_DIM, D_MODEL), lambda i: (0, 0)),
            pl.BlockSpec((1, D_MODEL), lambda i: (0, 0)),
        ],
        out_specs=pl.BlockSpec((tm, D_MODEL), lambda i: (i, 0)),
        out_shape=jax.ShapeDtypeStruct((t, D_MODEL), F32),
        compiler_params=_params(("arbitrary",)),
        name="ple",
    )(h, p, g_ple, wg, wp, g_fin)


def _rope_tables(pos):
    half = RET_D // 2
    inv = ROPE_BASE ** (-jnp.arange(half, dtype=F32) / half)
    ang = pos[:, None] * inv[None, :]
    cos, sin = jnp.cos(ang), jnp.sin(ang)
    return jnp.concatenate([cos, cos], axis=-1), jnp.concatenate([-sin, sin], axis=-1)


def _stream(x, p, pos, s0, c0, n0, m0, w):
    b, l, _ = x.shape
    t = b * l
    x2 = x.reshape(t, D_MODEL)
    z, gz = _in_proj(x2, w["g_mix"], w["w_in"], w["w_gates"])
    cosf, sinf = _rope_tables(pos)
    m0b = jnp.broadcast_to(m0[:, :, None], (b, MLSTM_HEADS, LANES))
    mix, s_new, c_new, n_new, m_new = _mixers(
        z.reshape(b, l, Z_COLS), gz.reshape(b, l, LANES), w["bias"], cosf, sinf, s0, c0, n0, m0b, w["g_ret"], w["g_ml"])
    h1 = _out_proj(mix.reshape(t, D_MODEL), w["w_out"], x2)
    idx, gate = _route(h1, w["g_ffn"], w["w_q"], w["k1"], w["k2"])
    h2 = _peer(idx.reshape(t * PEER_K), h1, w["g_ffn"], gate, w["uv"])
    y = _ple(h2, p.reshape(t, PLE_DIM), w["g_ple"], w["w_ple_gate"], w["w_ple_proj"], w["g_final"])
    return y.reshape(b, l, D_MODEL), s_new[None], c_new[None], n_new[None], m_new[None, :, :, 0]


def _prep_weights(g_mix, w_in, b_gates, g_ret, g_mlstm, w_out, g_ffn, w_peer_q, peer_keys1, peer_keys2, peer_u, peer_v,
                  g_ple, w_ple_gate, w_ple_proj, g_final):
    w_in0 = w_in[0]
    return {
        "g_mix": g_mix,
        "w_in": w_in0[:, :Z_COLS].astype(BF16),
        "w_gates": jnp.pad(w_in0[:, Z_COLS:], ((0, 0), (0, LANES - 2 * MLSTM_HEADS))).astype(BF16),
        "bias": jnp.pad(b_gates, ((0, 0), (0, LANES - 2 * MLSTM_HEADS))),
        "g_ret": g_ret,
        "g_ml": g_mlstm,
        "w_out": w_out[0].astype(BF16),
        "g_ffn": g_ffn,
        "w_q": w_peer_q[0].astype(BF16),
        "k1": peer_keys1[0].astype(BF16),
        "k2": peer_keys2[0].astype(BF16),
        "uv": jnp.concatenate([_pack_bf16_pairs(peer_u[0]), _pack_bf16_pairs(peer_v[0])], axis=1)[:, None, :],
        "g_ple": g_ple,
        "w_ple_gate": w_ple_gate[0].astype(BF16),
        "w_ple_proj": w_ple_proj[0].astype(BF16),
        "g_final": g_final[None, :],
    }


def kernel(x_prompt, x_sample, p_prompt, p_sample, state_ret, state_mlstm_C, state_mlstm_n, state_mlstm_m, g_mix, w_in, b_gates, g_ret, g_mlstm, w_out, g_ffn, w_peer_q, peer_keys1, peer_keys2, peer_u, peer_v, g_ple, w_ple_gate, w_ple_proj, g_final):
    w = _prep_weights(g_mix, w_in, b_gates, g_ret, g_mlstm, w_out, g_ffn, w_peer_q, peer_keys1, peer_keys2, peer_u, peer_v,
                      g_ple, w_ple_gate, w_ple_proj, g_final)
    bp, lp, _ = x_prompt.shape
    bs, ls, _ = x_sample.shape
    zeros = lambda *shape: jnp.zeros(shape, F32)
    past_len = 1024
    y_s, ret_s, c_s, n_s, m_s = _stream(
        x_sample, p_sample[0], past_len + jnp.arange(ls, dtype=F32),
        state_ret[0], state_mlstm_C[0], state_mlstm_n[0], state_mlstm_m[0], w)
    y_p, ret_p, c_p, n_p, m_p = _stream(
        x_prompt, p_prompt[0], jnp.arange(lp, dtype=F32),
        zeros(bp, RET_HEADS, RET_D, RET_D), zeros(bp, MLSTM_HEADS, MLSTM_DK, MLSTM_DV),
        zeros(bp, MLSTM_HEADS, MLSTM_DK), zeros(bp, MLSTM_HEADS), w)
    return (y_p, y_s, ret_p, c_p, n_p, m_p, ret_s, c_s, n_s, m_s)
```

```python
import functools
import math

import numpy as np
import jax
import jax.numpy as jnp
from jax import lax
from jax.experimental import pallas as pl
from jax.experimental.pallas import tpu as pltpu

F32 = jnp.float32
BF16 = jnp.bfloat16

D_MODEL = 2048
CHUNK = 64
RMS_EPS = 1e-6
ROPE_BASE = 10000.0
RET_HEADS = 8
RET_D = 128
MLSTM_HEADS = 4
MLSTM_DK = 128
MLSTM_DV = 256
Z_COLS = 7168
OFF_RQ, OFF_RK, OFF_RV, OFF_RG = 0, 1024, 2048, 3072
OFF_MQ, OFF_MK, OFF_MV, OFF_MO = 4096, 4608, 5120, 6144
PEER_HEADS = 8
N_KEYS = 128
PEER_TOPK = 16
PEER_K = PEER_HEADS * PEER_TOPK
PLE_DIM = 256
LANES = 128
SUBLANES = 8
CAND_ROWS = PEER_TOPK + (SUBLANES - 1) * SUBLANES + SUBLANES

LOG_GAMMA = [float(np.log(np.float32(1.0) - np.float32(2.0) ** np.float32(-5.0 - h))) for h in range(RET_HEADS)]

VMEM_LIMIT = 56 * 1024 * 1024


def _params(sem):
    return pltpu.CompilerParams(dimension_semantics=sem, vmem_limit_bytes=VMEM_LIMIT)


def _rms(x, g):
    return x * lax.rsqrt(jnp.mean(x * x, axis=-1, keepdims=True) + RMS_EPS) * g


def _mm(a, b):
    return jnp.dot(a.astype(BF16), b.astype(BF16), preferred_element_type=F32)


def _mm_nt(a, b):
    return lax.dot_general(a.astype(BF16), b.astype(BF16), (((1,), (1,)), ((), ())), preferred_element_type=F32)


def _mm_tn(a, b):
    return lax.dot_general(a.astype(BF16), b.astype(BF16), (((0,), (0,)), ((), ())), preferred_element_type=F32)


def _sigmoid(x):
    return 1.0 / (1.0 + jnp.exp(-x))


def _in_proj_kernel(x_ref, g_ref, w_ref, wg_ref, z_ref, gz_ref, a_scr):
    @pl.when(pl.program_id(1) == 0)
    def _():
        a_scr[...] = _rms(x_ref[...], g_ref[...]).astype(BF16)
        gz_ref[...] = jnp.dot(a_scr[...], wg_ref[...], preferred_element_type=F32)

    z_ref[...] = jnp.dot(a_scr[...], w_ref[...], preferred_element_type=F32)


def _in_proj(x, g, w, wg, tm=512, tn=1024):
    t = x.shape[0]
    return pl.pallas_call(
        _in_proj_kernel,
        grid=(t // tm, Z_COLS // tn),
        in_specs=[
            pl.BlockSpec((tm, D_MODEL), lambda i, j: (i, 0)),
            pl.BlockSpec((1, D_MODEL), lambda i, j: (0, 0)),
            pl.BlockSpec((D_MODEL, tn), lambda i, j: (0, j)),
            pl.BlockSpec((D_MODEL, LANES), lambda i, j: (0, 0)),
        ],
        out_specs=[
            pl.BlockSpec((tm, tn), lambda i, j: (i, j)),
            pl.BlockSpec((tm, LANES), lambda i, j: (i, 0)),
        ],
        out_shape=[jax.ShapeDtypeStruct((t, Z_COLS), F32), jax.ShapeDtypeStruct((t, LANES), F32)],
        scratch_shapes=[pltpu.VMEM((tm, D_MODEL), BF16)],
        compiler_params=_params(("arbitrary", "arbitrary")),
        name="in_proj",
    )(x, g, w, wg)


def _log_sigmoid(x):
    return -(jnp.maximum(-x, 0.0) + jnp.log1p(jnp.exp(-jnp.abs(x))))


def _mixers_kernel(z_ref, gz_ref, bias_ref, cos_ref, sin_ref, s0_ref, c0_ref, n0_ref, m0_ref, gret_ref, gml_ref,
                   mix_ref, s_ref, c_ref, n_ref, m_ref):
    @pl.when(pl.program_id(1) == 0)
    def _():
        s_ref[...] = s0_ref[...]
        c_ref[...] = c0_ref[...]
        n_ref[...] = n0_ref[...]
        m_ref[...] = m0_ref[...]

    cl = CHUNK
    row = lax.broadcasted_iota(jnp.int32, (cl, cl), 0)
    col = lax.broadcasted_iota(jnp.int32, (cl, cl), 1)
    causal = row >= col
    diff = jnp.where(causal, (row - col).astype(F32), 0.0)
    posc = lax.broadcasted_iota(jnp.int32, (cl, 1), 0).astype(F32)
    cosf = cos_ref[...]
    sinf = sin_ref[...]

    def rope(x):
        return x * cosf + pltpu.roll(x, RET_D // 2, axis=1) * sinf

    for h in range(RET_HEADS):
        lg = LOG_GAMMA[h]
        lo = h * RET_D
        q = rope(z_ref[0, :, OFF_RQ + lo:OFF_RQ + lo + RET_D])
        k = rope(z_ref[0, :, OFF_RK + lo:OFF_RK + lo + RET_D]) * (RET_D ** -0.5)
        v = z_ref[0, :, OFF_RV + lo:OFF_RV + lo + RET_D]
        rg = z_ref[0, :, OFF_RG + lo:OFF_RG + lo + RET_D]
        decay = jnp.where(causal, jnp.exp(diff * lg), 0.0)
        s_old = s_ref[0, h]
        scores = _mm_nt(q, k) * decay
        q_dec = q * jnp.exp((posc + 1.0) * lg)
        k_dec = k * jnp.exp((cl - 1.0 - posc) * lg)
        o = _mm(scores, v) + _mm(q_dec, s_old)
        s_ref[0, h] = math.exp(cl * lg) * s_old + _mm_tn(k_dec, v)
        y = o * lax.rsqrt(jnp.mean(o * o, axis=-1, keepdims=True) + RMS_EPS) * gret_ref[:, lo:lo + RET_D]
        mix_ref[0, :, lo:lo + RET_D] = (y * (rg * _sigmoid(rg))).astype(BF16)

    gates = gz_ref[0] + bias_ref[...]
    tri = causal.astype(F32)
    bcum = jnp.dot(tri, _log_sigmoid(gates), preferred_element_type=F32, precision=lax.Precision.HIGHEST)
    bcum_t = bcum.T
    gates_t = gates.T
    for h in range(MLSTM_HEADS):
        q = z_ref[0, :, OFF_MQ + h * MLSTM_DK:OFF_MQ + (h + 1) * MLSTM_DK]
        k = z_ref[0, :, OFF_MK + h * MLSTM_DK:OFF_MK + (h + 1) * MLSTM_DK] * (MLSTM_DK ** -0.5)
        v = z_ref[0, :, OFF_MV + h * MLSTM_DV:OFF_MV + (h + 1) * MLSTM_DV]
        mo = z_ref[0, :, OFF_MO + h * MLSTM_DV:OFF_MO + (h + 1) * MLSTM_DV]
        f = MLSTM_HEADS + h
        b_col = bcum[:, f:f + 1]
        b_row = bcum_t[f:f + 1, :]
        ig_col = gates[:, h:h + 1]
        ig_row = gates_t[h:h + 1, :]
        m_prev = m_ref[0, h:h + 1, 0:1]
        dlog = jnp.where(causal, b_col - b_row + ig_row, -jnp.inf)
        inter_log = b_col + m_prev
        m_t = jnp.maximum(inter_log, jnp.max(dlog, axis=-1, keepdims=True))
        dw = jnp.exp(dlog - m_t)
        inter_w = jnp.exp(inter_log - m_t)
        c_old = c_ref[0, h]
        n_old = n_ref[0, h:h + 1, :]
        sm = _mm_nt(q, k) * dw
        num = _mm(sm, v) + inter_w * _mm(q, c_old)
        den = jnp.sum(sm, axis=-1, keepdims=True) + inter_w * jnp.sum(q * n_old, axis=-1, keepdims=True)
        hh = num / jnp.maximum(jnp.abs(den), jnp.exp(-m_t))
        m_new = m_t[cl - 1:cl, :]
        b_last = b_col[cl - 1:cl, :]
        ws = jnp.exp(b_last - b_col + ig_col - m_new)
        carry = jnp.exp(b_last + m_prev - m_new)
        kw = k * ws
        c_ref[0, h] = carry * c_old + _mm_tn(kw, v)
        n_ref[0, h:h + 1, :] = carry * n_old + jnp.sum(kw, axis=0, keepdims=True)
        m_ref[0, h:h + 1, :] = jnp.broadcast_to(m_new, (1, LANES))
        y = hh * lax.rsqrt(jnp.mean(hh * hh, axis=-1, keepdims=True) + RMS_EPS) * gml_ref[:, h * MLSTM_DV:(h + 1) * MLSTM_DV]
        lo = RET_HEADS * RET_D + h * MLSTM_DV
        mix_ref[0, :, lo:lo + MLSTM_DV] = (y * _sigmoid(mo)).astype(BF16)


def _mixers(z, gz, bias, cosf, sinf, s0, c0, n0, m0, g_ret, g_ml):
    b, l, _ = z.shape
    per_b = lambda *tail: (lambda i, c: (i,) + tail)
    return pl.pallas_call(
        _mixers_kernel,
        grid=(b, l // CHUNK),
        in_specs=[
            pl.BlockSpec((1, CHUNK, Z_COLS), lambda i, c: (i, c, 0)),
            pl.BlockSpec((1, CHUNK, LANES), lambda i, c: (i, c, 0)),
            pl.BlockSpec((1, LANES), lambda i, c: (0, 0)),
            pl.BlockSpec((CHUNK, RET_D), lambda i, c: (c, 0)),
            pl.BlockSpec((CHUNK, RET_D), lambda i, c: (c, 0)),
            pl.BlockSpec((1, RET_HEADS, RET_D, RET_D), per_b(0, 0, 0)),
            pl.BlockSpec((1, MLSTM_HEADS, MLSTM_DK, MLSTM_DV), per_b(0, 0, 0)),
            pl.BlockSpec((1, MLSTM_HEADS, MLSTM_DK), per_b(0, 0)),
            pl.BlockSpec((1, MLSTM_HEADS, LANES), per_b(0, 0)),
            pl.BlockSpec((1, RET_HEADS * RET_D), lambda i, c: (0, 0)),
            pl.BlockSpec((1, MLSTM_HEADS * MLSTM_DV), lambda i, c: (0, 0)),
        ],
        out_specs=[
            pl.BlockSpec((1, CHUNK, D_MODEL), lambda i, c: (i, c, 0)),
            pl.BlockSpec((1, RET_HEADS, RET_D, RET_D), per_b(0, 0, 0)),
            pl.BlockSpec((1, MLSTM_HEADS, MLSTM_DK, MLSTM_DV), per_b(0, 0, 0)),
            pl.BlockSpec((1, MLSTM_HEADS, MLSTM_DK), per_b(0, 0)),
            pl.BlockSpec((1, MLSTM_HEADS, LANES), per_b(0, 0)),
        ],
        out_shape=[
            jax.ShapeDtypeStruct((b, l, D_MODEL), BF16),
            jax.ShapeDtypeStruct((b, RET_HEADS, RET_D, RET_D), F32),
            jax.ShapeDtypeStruct((b, MLSTM_HEADS, MLSTM_DK, MLSTM_DV), F32),
            jax.ShapeDtypeStruct((b, MLSTM_HEADS, MLSTM_DK), F32),
            jax.ShapeDtypeStruct((b, MLSTM_HEADS, LANES), F32),
        ],
        compiler_params=_params(("arbitrary", "arbitrary")),
        name="mixers",
    )(z, gz, bias, cosf, sinf, s0, c0, n0, m0, g_ret, g_ml)


def _out_proj_kernel(a_ref, w_ref, r_ref, o_ref):
    o_ref[...] = r_ref[...] + jnp.dot(a_ref[...], w_ref[...], preferred_element_type=F32)


def _out_proj(a, w, r, tm=512):
    t = a.shape[0]
    return pl.pallas_call(
        _out_proj_kernel,
        grid=(t // tm,),
        in_specs=[
            pl.BlockSpec((tm, D_MODEL), lambda i: (i, 0)),
            pl.BlockSpec((D_MODEL, D_MODEL), lambda i: (0, 0)),
            pl.BlockSpec((tm, D_MODEL), lambda i: (i, 0)),
        ],
        out_specs=pl.BlockSpec((tm, D_MODEL), lambda i: (i, 0)),
        out_shape=jax.ShapeDtypeStruct((t, D_MODEL), F32),
        compiler_params=_params(("arbitrary",)),
        name="out_proj",
    )(a, w, r)


def _top16(x_ref, payload_ref, val_ref, idx_ref):
    x = x_ref[...]
    rows = x.shape[0]
    iota = lax.broadcasted_iota(jnp.int32, x.shape, 0)
    for r in range(PEER_TOPK):
        m = jnp.max(x, axis=0, keepdims=True)
        am = jnp.min(jnp.where(x == m, iota, rows), axis=0, keepdims=True)
        sel = iota == am
        val_ref[r:r + 1, :] = m
        if payload_ref is None:
            idx_ref[r:r + 1, :] = am
        else:
            idx_ref[r:r + 1, :] = jnp.max(jnp.where(sel, payload_ref[...], -1), axis=0, keepdims=True)
        x = jnp.where(sel, -jnp.inf, x)


def _route_kernel(h_ref, g_ref, wq_ref, k1_ref, k2_ref, idx_ref, gate_ref,
                  q_scr, s_scr, t1_scr, i1_scr, t2_scr, i2_scr, cand_scr, cidx_scr, sc_scr, e_scr, et_scr, gt_scr):
    tm = h_ref.shape[0]
    c = _rms(h_ref[...], g_ref[...]).astype(BF16)
    q_scr[...] = jnp.dot(c, wq_ref[...], preferred_element_type=F32).astype(BF16)
    half = N_KEYS
    for h in range(PEER_HEADS):
        for sub in range(tm // LANES):
            tok = slice(sub * LANES, (sub + 1) * LANES)
            q1 = q_scr[tok, 2 * h * half:(2 * h + 1) * half]
            q2 = q_scr[tok, (2 * h + 1) * half:(2 * h + 2) * half]
            s_scr[...] = _mm_nt(k1_ref[...], q1)
            _top16(s_scr, None, t1_scr, i1_scr)
            s_scr[...] = _mm_nt(k2_ref[...], q2)
            _top16(s_scr, None, t2_scr, i2_scr)
            cand_scr[0:PEER_TOPK, :] = t1_scr[0:1, :] + t2_scr[...]
            cidx_scr[0:PEER_TOPK, :] = i1_scr[0:1, :] * N_KEYS + i2_scr[...]
            t2 = t2_scr[0:SUBLANES, :]
            i2 = i2_scr[0:SUBLANES, :]
            sub_iota = lax.broadcasted_iota(jnp.int32, (SUBLANES, LANES), 0)
            for a in range(1, SUBLANES):
                lo = PEER_TOPK + (a - 1) * SUBLANES
                cand_scr[lo:lo + SUBLANES, :] = jnp.where(sub_iota < PEER_TOPK // (a + 1), t1_scr[a:a + 1, :] + t2, -jnp.inf)
                cidx_scr[lo:lo + SUBLANES, :] = i1_scr[a:a + 1, :] * N_KEYS + i2
            cand_scr[CAND_ROWS - SUBLANES:CAND_ROWS, :] = t1_scr[SUBLANES:PEER_TOPK, :] + t2_scr[0:1, :]
            cidx_scr[CAND_ROWS - SUBLANES:CAND_ROWS, :] = i1_scr[SUBLANES:PEER_TOPK, :] * N_KEYS + i2_scr[0:1, :]
            _top16(cand_scr, cidx_scr, sc_scr, e_scr)
            sc = sc_scr[...]
            p = jnp.exp(sc - sc[0:1, :])
            gt_scr[h * PEER_TOPK:(h + 1) * PEER_TOPK, tok] = p / jnp.sum(p, axis=0, keepdims=True)
            et_scr[h * PEER_TOPK:(h + 1) * PEER_TOPK, tok] = e_scr[...]
    idx_ref[...] = et_scr[...].T
    gate_ref[...] = gt_scr[...].T


def _route(h, g, wq, k1, k2, tm=256):
    t = h.shape[0]
    return pl.pallas_call(
        _route_kernel,
        grid=(t // tm,),
        in_specs=[
            pl.BlockSpec((tm, D_MODEL), lambda i: (i, 0)),
            pl.BlockSpec((1, D_MODEL), lambda i: (0, 0)),
            pl.BlockSpec((D_MODEL, D_MODEL), lambda i: (0, 0)),
            pl.BlockSpec((N_KEYS, N_KEYS), lambda i: (0, 0)),
            pl.BlockSpec((N_KEYS, N_KEYS), lambda i: (0, 0)),
        ],
        out_specs=[
            pl.BlockSpec((tm, PEER_K), lambda i: (i, 0)),
            pl.BlockSpec((tm, PEER_K), lambda i: (i, 0)),
        ],
        out_shape=[jax.ShapeDtypeStruct((t, PEER_K), jnp.int32), jax.ShapeDtypeStruct((t, PEER_K), F32)],
        scratch_shapes=[
            pltpu.VMEM((tm, D_MODEL), BF16),
            pltpu.VMEM((N_KEYS, LANES), F32),
            pltpu.VMEM((PEER_TOPK, LANES), F32),
            pltpu.VMEM((PEER_TOPK, LANES), jnp.int32),
            pltpu.VMEM((PEER_TOPK, LANES), F32),
            pltpu.VMEM((PEER_TOPK, LANES), jnp.int32),
            pltpu.VMEM((CAND_ROWS, LANES), F32),
            pltpu.VMEM((CAND_ROWS, LANES), jnp.int32),
            pltpu.VMEM((PEER_TOPK, LANES), F32),
            pltpu.VMEM((PEER_TOPK, LANES), jnp.int32),
            pltpu.VMEM((PEER_K, tm), jnp.int32),
            pltpu.VMEM((PEER_K, tm), F32),
        ],
        compiler_params=_params(("arbitrary",)),
        name="route",
    )(h, g, wq, k1, k2)


PEER_TB = 256
PEER_SLOTS = 16
PEER_LEAD = PEER_SLOTS - 1
HALF_D = D_MODEL // 2
PACK_TILES = HALF_D // LANES
INV_SQRT2 = 0.7071067811865476


def _pack_bf16_pairs(a):
    bits = lax.bitcast_convert_type(a.astype(BF16), jnp.uint16).astype(jnp.uint32)
    return bits[:, :HALF_D] | (bits[:, HALF_D:] << 16)


def _peer_kernel(idx_hbm, h_ref, g_ref, gate_ref, uv_hbm, o_ref, idx_s, c_scr, *scratch):
    rows = scratch[:PEER_SLOTS]
    idx_sem, row_sems = scratch[PEER_SLOTS:]
    tb = h_ref.shape[0]
    step = pl.program_id(0)
    idx_copy = pltpu.make_async_copy(idx_hbm.at[pl.ds(step * (tb * PEER_K), tb * PEER_K)], idx_s, idx_sem)
    idx_copy.start()
    c_scr[...] = _rms(h_ref[...], g_ref[...])
    idx_copy.wait()

    def issue(t, slot, k0, k1):
        for k in range(k0, k1):
            e = idx_s[t * PEER_K + k]
            pltpu.make_async_copy(uv_hbm.at[e], rows[slot].at[pl.ds(k, 1), :], row_sems.at[slot]).start(priority=k % 2)

    def wait(slot):
        pltpu.make_async_copy(uv_hbm.at[pl.ds(0, PEER_K), 0], rows[slot], row_sems.at[slot]).wait()

    def unpack(words):
        lo = lax.bitcast_convert_type(jnp.left_shift(words, jnp.uint32(16)), F32)
        hi = lax.bitcast_convert_type(jnp.bitwise_and(words, jnp.uint32(0xFFFF0000)), F32)
        return lo, hi

    eye = lax.broadcasted_iota(jnp.int32, (PEER_K, LANES), 0) == lax.broadcasted_iota(jnp.int32, (PEER_K, LANES), 1)
    per_tile = PEER_K // PACK_TILES

    def token(t, slot, prefetch):
        wait(slot)
        ahead_slot = (slot + PEER_LEAD) % PEER_SLOTS
        x = c_scr[pl.ds(t, 1), :]
        acc = None
        for j in range(PACK_TILES):
            if prefetch:
                issue(t + PEER_LEAD, ahead_slot, j * per_tile, (j + 1) * per_tile)
            lo, hi = unpack(rows[slot][:, j * LANES:(j + 1) * LANES])
            p = lo * x[:, j * LANES:(j + 1) * LANES] + hi * x[:, HALF_D + j * LANES:HALF_D + (j + 1) * LANES]
            acc = p if acc is None else acc + p
        act = jnp.sum(acc, axis=1, keepdims=True)
        gate_col = jnp.sum(jnp.where(eye, gate_ref[pl.ds(t, 1), :], 0.0), axis=1, keepdims=True)
        coef = gate_col * (0.5 * act * (1.0 + lax.erf(act * INV_SQRT2)))
        mixed_lo, mixed_hi = [], []
        for j in range(PACK_TILES):
            lo, hi = unpack(rows[slot][:, HALF_D + j * LANES:HALF_D + (j + 1) * LANES])
            mixed_lo.append(jnp.sum(lo * coef, axis=0, keepdims=True))
            mixed_hi.append(jnp.sum(hi * coef, axis=0, keepdims=True))
        o_ref[pl.ds(t, 1), :] = h_ref[pl.ds(t, 1), :] + jnp.concatenate(mixed_lo + mixed_hi, axis=1)

    for s in range(PEER_LEAD):
        issue(s, s, 0, PEER_K)

    def group(i, carry):
        for s in range(PEER_SLOTS):
            token(i * PEER_SLOTS + s, s, True)
        return carry

    lax.fori_loop(0, tb // PEER_SLOTS - 1, group, 0)
    last = tb - PEER_SLOTS
    token(last, 0, True)
    for s in range(1, PEER_SLOTS):
        token(last + s, s, False)


def _peer(idx_flat, h, g, gate, uv):
    t = h.shape[0]
    tb = PEER_TB
    return pl.pallas_call(
        _peer_kernel,
        grid=(t // tb,),
        in_specs=[
            pl.BlockSpec(memory_space=pl.ANY),
            pl.BlockSpec((tb, D_MODEL), lambda i: (i, 0)),
            pl.BlockSpec((1, D_MODEL), lambda i: (0, 0)),
            pl.BlockSpec((tb, PEER_K), lambda i: (i, 0)),
            pl.BlockSpec(memory_space=pl.ANY),
        ],
        out_specs=pl.BlockSpec((tb, D_MODEL), lambda i: (i, 0)),
        out_shape=jax.ShapeDtypeStruct((t, D_MODEL), F32),
        scratch_shapes=[
            pltpu.SMEM((tb * PEER_K,), jnp.int32),
            pltpu.VMEM((tb, D_MODEL), F32),
        ] + [pltpu.VMEM((PEER_K, D_MODEL), jnp.uint32)] * PEER_SLOTS + [
            pltpu.SemaphoreType.DMA(()),
            pltpu.SemaphoreType.DMA((PEER_SLOTS,)),
        ],
        compiler_params=_params(("arbitrary",)),
        name="peer",
    )(idx_flat, h, g, gate, uv)


def _ple_kernel(h_ref, p_ref, gple_ref, wg_ref, wp_ref, gfin_ref, y_ref):
    h = h_ref[...]
    e = _rms(h, gple_ref[...]).astype(BF16)
    gate = _sigmoid(jnp.dot(e, wg_ref[...], preferred_element_type=F32))
    proj = jnp.dot(p_ref[...].astype(BF16), wp_ref[...], preferred_element_type=F32)
    y_ref[...] = _rms(h + gate * proj, gfin_ref[...])


def _ple(h, p, g_ple, wg, wp, g_fin, tm=256):
    t = h.shape[0]
    return pl.pallas_call(
        _ple_kernel,
        grid=(t // tm,),
        in_specs=[
            pl.BlockSpec((tm, D_MODEL), lambda i: (i, 0)),
            pl.BlockSpec((tm, PLE_DIM), lambda i: (i, 0)),
            pl.BlockSpec((1, D_MODEL), lambda i: (0, 0)),
            pl.BlockSpec((D_MODEL, D_MODEL), lambda i: (0, 0)),
            pl.BlockSpec((PLE_DIM, D_MODEL), lambda i: (0, 0)),
            pl.BlockSpec((1, D_MODEL), lambda i: (0, 0)),
        ],
        out_specs=pl.BlockSpec((tm, D_MODEL), lambda i: (i, 0)),
        out_shape=jax.ShapeDtypeStruct((t, D_MODEL), F32),
        compiler_params=_params(("arbitrary",)),
        name="ple",
    )(h, p, g_ple, wg, wp, g_fin)


def _rope_tables(pos):
    half = RET_D // 2
    inv = ROPE_BASE ** (-jnp.arange(half, dtype=F32) / half)
    ang = pos[:, None] * inv[None, :]
    cos, sin = jnp.cos(ang), jnp.sin(ang)
    return jnp.concatenate([cos, cos], axis=-1), jnp.concatenate([-sin, sin], axis=-1)


def _stream(x, p, pos, s0, c0, n0, m0, w):
    b, l, _ = x.shape
    t = b * l
    x2 = x.reshape(t, D_MODEL)
    z, gz = _in_proj(x2, w["g_mix"], w["w_in"], w["w_gates"])
    cosf, sinf = _rope_tables(pos)
    m0b = jnp.broadcast_to(m0[:, :, None], (b, MLSTM_HEADS, LANES))
    mix, s_new, c_new, n_new, m_new = _mixers(
        z.reshape(b, l, Z_COLS), gz.reshape(b, l, LANES), w["bias"], cosf, sinf, s0, c0, n0, m0b, w["g_ret"], w["g_ml"])
    h1 = _out_proj(mix.reshape(t, D_MODEL), w["w_out"], x2)
    idx, gate = _route(h1, w["g_ffn"], w["w_q"], w["k1"], w["k2"])
    h2 = _peer(idx.reshape(t * PEER_K), h1, w["g_ffn"], gate, w["uv"])
    y = _ple(h2, p.reshape(t, PLE_DIM), w["g_ple"], w["w_ple_gate"], w["w_ple_proj"], w["g_final"])
    return y.reshape(b, l, D_MODEL), s_new[None], c_new[None], n_new[None], m_new[None, :, :, 0]


def _prep_weights(g_mix, w_in, b_gates, g_ret, g_mlstm, w_out, g_ffn, w_peer_q, peer_keys1, peer_keys2, peer_u, peer_v,
                  g_ple, w_ple_gate, w_ple_proj, g_final):
    w_in0 = w_in[0]
    return {
        "g_mix": g_mix,
        "w_in": w_in0[:, :Z_COLS].astype(BF16),
        "w_gates": jnp.pad(w_in0[:, Z_COLS:], ((0, 0), (0, LANES - 2 * MLSTM_HEADS))).astype(BF16),
        "bias": jnp.pad(b_gates, ((0, 0), (0, LANES - 2 * MLSTM_HEADS))),
        "g_ret": g_ret,
        "g_ml": g_mlstm,
        "w_out": w_out[0].astype(BF16),
        "g_ffn": g_ffn,
        "w_q": w_peer_q[0].astype(BF16),
        "k1": peer_keys1[0].astype(BF16),
        "k2": peer_keys2[0].astype(BF16),
        "uv": jnp.concatenate([_pack_bf16_pairs(peer_u[0]), _pack_bf16_pairs(peer_v[0])], axis=1)[:, None, :],
        "g_ple": g_ple,
        "w_ple_gate": w_ple_gate[0].astype(BF16),
        "w_ple_proj": w_ple_proj[0].astype(BF16),
        "g_final": g_final[None, :],
    }


def kernel(x_prompt, x_sample, p_prompt, p_sample, state_ret, state_mlstm_C, state_mlstm_n, state_mlstm_m, g_mix, w_in, b_gates, g_ret, g_mlstm, w_out, g_ffn, w_peer_q, peer_keys1, peer_keys2, peer_u, peer_v, g_ple, w_ple_gate, w_ple_proj, g_final):
    w = _prep_weights(g_mix, w_in, b_gates, g_ret, g_mlstm, w_out, g_ffn, w_peer_q, peer_keys1, peer_keys2, peer_u, peer_v,
                      g_ple, w_ple_gate, w_ple_proj, g_final)
    bp, lp, _ = x_prompt.shape
    bs, ls, _ = x_sample.shape
    zeros = lambda *shape: jnp.zeros(shape, F32)
    past_len = 1024
    y_s, ret_s, c_s, n_s, m_s = _stream(
        x_sample, p_sample[0], past_len + jnp.arange(ls, dtype=F32),
        state_ret[0], state_mlstm_C[0], state_mlstm_n[0], state_mlstm_m[0], w)
    y_p, ret_p, c_p, n_p, m_p = _stream(
        x_prompt, p_prompt[0], jnp.arange(lp, dtype=F32),
        zeros(bp, RET_HEADS, RET_D, RET_D), zeros(bp, MLSTM_HEADS, MLSTM_DK, MLSTM_DV),
        zeros(bp, MLSTM_HEADS, MLSTM_DK), zeros(bp, MLSTM_HEADS), w)
    return (y_p, y_s, ret_p, c_p, n_p, m_p, ret_s, c_s, n_s, m_s)
```

```python
import functools
import math

import numpy as np
import jax
import jax.numpy as jnp
from jax import lax
from jax.experimental import pallas as pl
from jax.experimental.pallas import tpu as pltpu

F32 = jnp.float32
BF16 = jnp.bfloat16

D_MODEL = 2048
CHUNK = 64
RMS_EPS = 1e-6
ROPE_BASE = 10000.0
RET_HEADS = 8
RET_D = 128
MLSTM_HEADS = 4
MLSTM_DK = 128
MLSTM_DV = 256
Z_COLS = 7168
OFF_RQ, OFF_RK, OFF_RV, OFF_RG = 0, 1024, 2048, 3072
OFF_MQ, OFF_MK, OFF_MV, OFF_MO = 4096, 4608, 5120, 6144
PEER_HEADS = 8
N_KEYS = 128
PEER_TOPK = 16
PEER_K = PEER_HEADS * PEER_TOPK
PLE_DIM = 256
LANES = 128
SUBLANES = 8
CAND_ROWS = PEER_TOPK + (SUBLANES - 1) * SUBLANES + SUBLANES

LOG_GAMMA = [float(np.log(np.float32(1.0) - np.float32(2.0) ** np.float32(-5.0 - h))) for h in range(RET_HEADS)]

VMEM_LIMIT = 56 * 1024 * 1024


def _params(sem):
    return pltpu.CompilerParams(dimension_semantics=sem, vmem_limit_bytes=VMEM_LIMIT)


def _rms(x, g):
    return x * lax.rsqrt(jnp.mean(x * x, axis=-1, keepdims=True) + RMS_EPS) * g


def _mm(a, b):
    return jnp.dot(a.astype(BF16), b.astype(BF16), preferred_element_type=F32)


def _mm_nt(a, b):
    return lax.dot_general(a.astype(BF16), b.astype(BF16), (((1,), (1,)), ((), ())), preferred_element_type=F32)


def _mm_tn(a, b):
    return lax.dot_general(a.astype(BF16), b.astype(BF16), (((0,), (0,)), ((), ())), preferred_element_type=F32)


def _sigmoid(x):
    return 1.0 / (1.0 + jnp.exp(-x))


def _in_proj_kernel(x_ref, g_ref, w_ref, wg_ref, z_ref, gz_ref, a_scr):
    @pl.when(pl.program_id(1) == 0)
    def _():
        a_scr[...] = _rms(x_ref[...], g_ref[...]).astype(BF16)
        gz_ref[...] = jnp.dot(a_scr[...], wg_ref[...], preferred_element_type=F32)

    z_ref[...] = jnp.dot(a_scr[...], w_ref[...], preferred_element_type=F32)


def _in_proj(x, g, w, wg, tm=1024, tn=1024):
    t = x.shape[0]
    tm = min(tm, t)
    return pl.pallas_call(
        _in_proj_kernel,
        grid=(t // tm, Z_COLS // tn),
        in_specs=[
            pl.BlockSpec((tm, D_MODEL), lambda i, j: (i, 0)),
            pl.BlockSpec((1, D_MODEL), lambda i, j: (0, 0)),
            pl.BlockSpec((D_MODEL, tn), lambda i, j: (0, j)),
            pl.BlockSpec((D_MODEL, LANES), lambda i, j: (0, 0)),
        ],
        out_specs=[
            pl.BlockSpec((tm, tn), lambda i, j: (i, j)),
            pl.BlockSpec((tm, LANES), lambda i, j: (i, 0)),
        ],
        out_shape=[jax.ShapeDtypeStruct((t, Z_COLS), F32), jax.ShapeDtypeStruct((t, LANES), F32)],
        scratch_shapes=[pltpu.VMEM((tm, D_MODEL), BF16)],
        compiler_params=_params(("arbitrary", "arbitrary")),
        name="in_proj",
    )(x, g, w, wg)


def _log_sigmoid(x):
    return -(jnp.maximum(-x, 0.0) + jnp.log1p(jnp.exp(-jnp.abs(x))))


def _mixers_kernel(z_ref, gz_ref, bias_ref, cos_ref, sin_ref, s0_ref, c0_ref, n0_ref, m0_ref, gret_ref, gml_ref,
                   mix_ref, s_ref, c_ref, n_ref, m_ref):
    @pl.when(pl.program_id(1) == 0)
    def _():
        s_ref[...] = s0_ref[...]
        c_ref[...] = c0_ref[...]
        n_ref[...] = n0_ref[...]
        m_ref[...] = m0_ref[...]

    cl = CHUNK
    row = lax.broadcasted_iota(jnp.int32, (cl, cl), 0)
    col = lax.broadcasted_iota(jnp.int32, (cl, cl), 1)
    causal = row >= col
    diff = jnp.where(causal, (row - col).astype(F32), 0.0)
    posc = lax.broadcasted_iota(jnp.int32, (cl, 1), 0).astype(F32)
    cosf = cos_ref[...]
    sinf = sin_ref[...]

    def rope(x):
        return x * cosf + pltpu.roll(x, RET_D // 2, axis=1) * sinf

    for h in range(RET_HEADS):
        lg = LOG_GAMMA[h]
        lo = h * RET_D
        q = rope(z_ref[0, :, OFF_RQ + lo:OFF_RQ + lo + RET_D])
        k = rope(z_ref[0, :, OFF_RK + lo:OFF_RK + lo + RET_D]) * (RET_D ** -0.5)
        v = z_ref[0, :, OFF_RV + lo:OFF_RV + lo + RET_D]
        rg = z_ref[0, :, OFF_RG + lo:OFF_RG + lo + RET_D]
        decay = jnp.where(causal, jnp.exp(diff * lg), 0.0)
        s_old = s_ref[0, h]
        scores = _mm_nt(q, k) * decay
        q_dec = q * jnp.exp((posc + 1.0) * lg)
        k_dec = k * jnp.exp((cl - 1.0 - posc) * lg)
        o = _mm(scores, v) + _mm(q_dec, s_old)
        s_ref[0, h] = math.exp(cl * lg) * s_old + _mm_tn(k_dec, v)
        y = o * lax.rsqrt(jnp.mean(o * o, axis=-1, keepdims=True) + RMS_EPS) * gret_ref[:, lo:lo + RET_D]
        mix_ref[0, :, lo:lo + RET_D] = (y * (rg * _sigmoid(rg))).astype(BF16)

    gates = gz_ref[0] + bias_ref[...]
    tri = causal.astype(F32)
    bcum = jnp.dot(tri, _log_sigmoid(gates), preferred_element_type=F32, precision=lax.Precision.HIGHEST)
    bcum_t = bcum.T
    gates_t = gates.T
    for h in range(MLSTM_HEADS):
        q = z_ref[0, :, OFF_MQ + h * MLSTM_DK:OFF_MQ + (h + 1) * MLSTM_DK]
        k = z_ref[0, :, OFF_MK + h * MLSTM_DK:OFF_MK + (h + 1) * MLSTM_DK] * (MLSTM_DK ** -0.5)
        v = z_ref[0, :, OFF_MV + h * MLSTM_DV:OFF_MV + (h + 1) * MLSTM_DV]
        mo = z_ref[0, :, OFF_MO + h * MLSTM_DV:OFF_MO + (h + 1) * MLSTM_DV]
        f = MLSTM_HEADS + h
        b_col = bcum[:, f:f + 1]
        b_row = bcum_t[f:f + 1, :]
        ig_col = gates[:, h:h + 1]
        ig_row = gates_t[h:h + 1, :]
        m_prev = m_ref[0, h:h + 1, 0:1]
        dlog = jnp.where(causal, b_col - b_row + ig_row, -jnp.inf)
        inter_log = b_col + m_prev
        m_t = jnp.maximum(inter_log, jnp.max(dlog, axis=-1, keepdims=True))
        dw = jnp.exp(dlog - m_t)
        inter_w = jnp.exp(inter_log - m_t)
        c_old = c_ref[0, h]
        n_old = n_ref[0, h:h + 1, :]
        sm = _mm_nt(q, k) * dw
        num = _mm(sm, v) + inter_w * _mm(q, c_old)
        den = jnp.sum(sm, axis=-1, keepdims=True) + inter_w * jnp.sum(q * n_old, axis=-1, keepdims=True)
        hh = num / jnp.maximum(jnp.abs(den), jnp.exp(-m_t))
        m_new = m_t[cl - 1:cl, :]
        b_last = b_col[cl - 1:cl, :]
        ws = jnp.exp(b_last - b_col + ig_col - m_new)
        carry = jnp.exp(b_last + m_prev - m_new)
        kw = k * ws
        c_ref[0, h] = carry * c_old + _mm_tn(kw, v)
        n_ref[0, h:h + 1, :] = carry * n_old + jnp.sum(kw, axis=0, keepdims=True)
        m_ref[0, h:h + 1, :] = jnp.broadcast_to(m_new, (1, LANES))
        y = hh * lax.rsqrt(jnp.mean(hh * hh, axis=-1, keepdims=True) + RMS_EPS) * gml_ref[:, h * MLSTM_DV:(h + 1) * MLSTM_DV]
        lo = RET_HEADS * RET_D + h * MLSTM_DV
        mix_ref[0, :, lo:lo + MLSTM_DV] = (y * _sigmoid(mo)).astype(BF16)


def _mixers(z, gz, bias, cosf, sinf, s0, c0, n0, m0, g_ret, g_ml):
    b, l, _ = z.shape
    per_b = lambda *tail: (lambda i, c: (i,) + tail)
    return pl.pallas_call(
        _mixers_kernel,
        grid=(b, l // CHUNK),
        in_specs=[
            pl.BlockSpec((1, CHUNK, Z_COLS), lambda i, c: (i, c, 0)),
            pl.BlockSpec((1, CHUNK, LANES), lambda i, c: (i, c, 0)),
            pl.BlockSpec((1, LANES), lambda i, c: (0, 0)),
            pl.BlockSpec((CHUNK, RET_D), lambda i, c: (c, 0)),
            pl.BlockSpec((CHUNK, RET_D), lambda i, c: (c, 0)),
            pl.BlockSpec((1, RET_HEADS, RET_D, RET_D), per_b(0, 0, 0)),
            pl.BlockSpec((1, MLSTM_HEADS, MLSTM_DK, MLSTM_DV), per_b(0, 0, 0)),
            pl.BlockSpec((1, MLSTM_HEADS, MLSTM_DK), per_b(0, 0)),
            pl.BlockSpec((1, MLSTM_HEADS, LANES), per_b(0, 0)),
            pl.BlockSpec((1, RET_HEADS * RET_D), lambda i, c: (0, 0)),
            pl.BlockSpec((1, MLSTM_HEADS * MLSTM_DV), lambda i, c: (0, 0)),
        ],
        out_specs=[
            pl.BlockSpec((1, CHUNK, D_MODEL), lambda i, c: (i, c, 0)),
            pl.BlockSpec((1, RET_HEADS, RET_D, RET_D), per_b(0, 0, 0)),
            pl.BlockSpec((1, MLSTM_HEADS, MLSTM_DK, MLSTM_DV), per_b(0, 0, 0)),
            pl.BlockSpec((1, MLSTM_HEADS, MLSTM_DK), per_b(0, 0)),
            pl.BlockSpec((1, MLSTM_HEADS, LANES), per_b(0, 0)),
        ],
        out_shape=[
            jax.ShapeDtypeStruct((b, l, D_MODEL), BF16),
            jax.ShapeDtypeStruct((b, RET_HEADS, RET_D, RET_D), F32),
            jax.ShapeDtypeStruct((b, MLSTM_HEADS, MLSTM_DK, MLSTM_DV), F32),
            jax.ShapeDtypeStruct((b, MLSTM_HEADS, MLSTM_DK), F32),
            jax.ShapeDtypeStruct((b, MLSTM_HEADS, LANES), F32),
        ],
        compiler_params=_params(("arbitrary", "arbitrary")),
        name="mixers",
    )(z, gz, bias, cosf, sinf, s0, c0, n0, m0, g_ret, g_ml)


def _out_proj_kernel(a_ref, w_ref, r_ref, o_ref):
    o_ref[...] = r_ref[...] + jnp.dot(a_ref[...], w_ref[...], preferred_element_type=F32)


def _out_proj(a, w, r, tm=512):
    t = a.shape[0]
    return pl.pallas_call(
        _out_proj_kernel,
        grid=(t // tm,),
        in_specs=[
            pl.BlockSpec((tm, D_MODEL), lambda i: (i, 0)),
            pl.BlockSpec((D_MODEL, D_MODEL), lambda i: (0, 0)),
            pl.BlockSpec((tm, D_MODEL), lambda i: (i, 0)),
        ],
        out_specs=pl.BlockSpec((tm, D_MODEL), lambda i: (i, 0)),
        out_shape=jax.ShapeDtypeStruct((t, D_MODEL), F32),
        compiler_params=_params(("arbitrary",)),
        name="out_proj",
    )(a, w, r)


def _top16(x_ref, payload_ref, val_ref, idx_ref):
    x = x_ref[...]
    rows = x.shape[0]
    iota = lax.broadcasted_iota(jnp.int32, x.shape, 0)
    for r in range(PEER_TOPK):
        m = jnp.max(x, axis=0, keepdims=True)
        am = jnp.min(jnp.where(x == m, iota, rows), axis=0, keepdims=True)
        sel = iota == am
        val_ref[r:r + 1, :] = m
        if payload_ref is None:
            idx_ref[r:r + 1, :] = am
        else:
            idx_ref[r:r + 1, :] = jnp.max(jnp.where(sel, payload_ref[...], -1), axis=0, keepdims=True)
        x = jnp.where(sel, -jnp.inf, x)


def _route_kernel(h_ref, g_ref, wq_ref, k1_ref, k2_ref, idx_ref, gate_ref,
                  q_scr, s_scr, t1_scr, i1_scr, t2_scr, i2_scr, cand_scr, cidx_scr, sc_scr, e_scr, et_scr, gt_scr):
    tm = h_ref.shape[0]
    c = _rms(h_ref[...], g_ref[...]).astype(BF16)
    q_scr[...] = jnp.dot(c, wq_ref[...], preferred_element_type=F32).astype(BF16)
    half = N_KEYS
    for h in range(PEER_HEADS):
        for sub in range(tm // LANES):
            tok = slice(sub * LANES, (sub + 1) * LANES)
            q1 = q_scr[tok, 2 * h * half:(2 * h + 1) * half]
            q2 = q_scr[tok, (2 * h + 1) * half:(2 * h + 2) * half]
            s_scr[...] = _mm_nt(k1_ref[...], q1)
            _top16(s_scr, None, t1_scr, i1_scr)
            s_scr[...] = _mm_nt(k2_ref[...], q2)
            _top16(s_scr, None, t2_scr, i2_scr)
            cand_scr[0:PEER_TOPK, :] = t1_scr[0:1, :] + t2_scr[...]
            cidx_scr[0:PEER_TOPK, :] = i1_scr[0:1, :] * N_KEYS + i2_scr[...]
            t2 = t2_scr[0:SUBLANES, :]
            i2 = i2_scr[0:SUBLANES, :]
            sub_iota = lax.broadcasted_iota(jnp.int32, (SUBLANES, LANES), 0)
            for a in range(1, SUBLANES):
                lo = PEER_TOPK + (a - 1) * SUBLANES
                cand_scr[lo:lo + SUBLANES, :] = jnp.where(sub_iota < PEER_TOPK // (a + 1), t1_scr[a:a + 1, :] + t2, -jnp.inf)
                cidx_scr[lo:lo + SUBLANES, :] = i1_scr[a:a + 1, :] * N_KEYS + i2
            cand_scr[CAND_ROWS - SUBLANES:CAND_ROWS, :] = t1_scr[SUBLANES:PEER_TOPK, :] + t2_scr[0:1, :]
            cidx_scr[CAND_ROWS - SUBLANES:CAND_ROWS, :] = i1_scr[SUBLANES:PEER_TOPK, :] * N_KEYS + i2_scr[0:1, :]
            _top16(cand_scr, cidx_scr, sc_scr, e_scr)
            sc = sc_scr[...]
            p = jnp.exp(sc - sc[0:1, :])
            gt_scr[h * PEER_TOPK:(h + 1) * PEER_TOPK, tok] = p / jnp.sum(p, axis=0, keepdims=True)
            et_scr[h * PEER_TOPK:(h + 1) * PEER_TOPK, tok] = e_scr[...]
    idx_ref[...] = et_scr[...].T
    gate_ref[...] = gt_scr[...].T


def _route(h, g, wq, k1, k2, tm=256):
    t = h.shape[0]
    return pl.pallas_call(
        _route_kernel,
        grid=(t // tm,),
        in_specs=[
            pl.BlockSpec((tm, D_MODEL), lambda i: (i, 0)),
            pl.BlockSpec((1, D_MODEL), lambda i: (0, 0)),
            pl.BlockSpec((D_MODEL, D_MODEL), lambda i: (0, 0)),
            pl.BlockSpec((N_KEYS, N_KEYS), lambda i: (0, 0)),
            pl.BlockSpec((N_KEYS, N_KEYS), lambda i: (0, 0)),
        ],
        out_specs=[
            pl.BlockSpec((tm, PEER_K), lambda i: (i, 0)),
            pl.BlockSpec((tm, PEER_K), lambda i: (i, 0)),
        ],
        out_shape=[jax.ShapeDtypeStruct((t, PEER_K), jnp.int32), jax.ShapeDtypeStruct((t, PEER_K), F32)],
        scratch_shapes=[
            pltpu.VMEM((tm, D_MODEL), BF16),
            pltpu.VMEM((N_KEYS, LANES), F32),
            pltpu.VMEM((PEER_TOPK, LANES), F32),
            pltpu.VMEM((PEER_TOPK, LANES), jnp.int32),
            pltpu.VMEM((PEER_TOPK, LANES), F32),
            pltpu.VMEM((PEER_TOPK, LANES), jnp.int32),
            pltpu.VMEM((CAND_ROWS, LANES), F32),
            pltpu.VMEM((CAND_ROWS, LANES), jnp.int32),
            pltpu.VMEM((PEER_TOPK, LANES), F32),
            pltpu.VMEM((PEER_TOPK, LANES), jnp.int32),
            pltpu.VMEM((PEER_K, tm), jnp.int32),
            pltpu.VMEM((PEER_K, tm), F32),
        ],
        compiler_params=_params(("arbitrary",)),
        name="route",
    )(h, g, wq, k1, k2)


PEER_TB = 512
PEER_SLOTS = 8
PEER_LEAD = PEER_SLOTS - 1
HALF_D = D_MODEL // 2
PACK_TILES = HALF_D // LANES
INV_SQRT2 = 0.7071067811865476


def _pack_bf16_pairs(a):
    bits = lax.bitcast_convert_type(a.astype(BF16), jnp.uint16).astype(jnp.uint32)
    return bits[:, :HALF_D] | (bits[:, HALF_D:] << 16)


def _peer_kernel(idx_hbm, h_ref, g_ref, gate_ref, uv_hbm, o_ref, idx_s, c_scr, *scratch):
    rows = scratch[:PEER_SLOTS]
    idx_sem, row_sems = scratch[PEER_SLOTS:]
    tb = h_ref.shape[0]
    step = pl.program_id(0)
    idx_copy = pltpu.make_async_copy(idx_hbm.at[pl.ds(step * (tb * PEER_K), tb * PEER_K)], idx_s, idx_sem)
    idx_copy.start()
    c_scr[...] = _rms(h_ref[...], g_ref[...])
    idx_copy.wait()

    def issue(t, slot, k0, k1):
        for k in range(k0, k1):
            e = idx_s[t * PEER_K + k]
            pltpu.make_async_copy(uv_hbm.at[e], rows[slot].at[pl.ds(k, 1), :], row_sems.at[slot]).start(priority=k % 2)

    def wait(slot):
        pltpu.make_async_copy(uv_hbm.at[pl.ds(0, PEER_K), 0], rows[slot], row_sems.at[slot]).wait()

    def unpack(words):
        lo = lax.bitcast_convert_type(jnp.left_shift(words, jnp.uint32(16)), F32)
        hi = lax.bitcast_convert_type(jnp.bitwise_and(words, jnp.uint32(0xFFFF0000)), F32)
        return lo, hi

    eye = lax.broadcasted_iota(jnp.int32, (PEER_K, LANES), 0) == lax.broadcasted_iota(jnp.int32, (PEER_K, LANES), 1)
    per_tile = PEER_K // PACK_TILES

    def token(t, slot, prefetch):
        wait(slot)
        ahead_slot = (slot + PEER_LEAD) % PEER_SLOTS
        x = c_scr[pl.ds(t, 1), :]
        acc = None
        for j in range(PACK_TILES):
            if prefetch:
                issue(t + PEER_LEAD, ahead_slot, j * per_tile, (j + 1) * per_tile)
            lo, hi = unpack(rows[slot][:, j * LANES:(j + 1) * LANES])
            p = lo * x[:, j * LANES:(j + 1) * LANES] + hi * x[:, HALF_D + j * LANES:HALF_D + (j + 1) * LANES]
            acc = p if acc is None else acc + p
        act = jnp.sum(acc, axis=1, keepdims=True)
        gate_col = jnp.sum(jnp.where(eye, gate_ref[pl.ds(t, 1), :], 0.0), axis=1, keepdims=True)
        coef = gate_col * (0.5 * act * (1.0 + lax.erf(act * INV_SQRT2)))
        mixed_lo, mixed_hi = [], []
        for j in range(PACK_TILES):
            lo, hi = unpack(rows[slot][:, HALF_D + j * LANES:HALF_D + (j + 1) * LANES])
            mixed_lo.append(jnp.sum(lo * coef, axis=0, keepdims=True))
            mixed_hi.append(jnp.sum(hi * coef, axis=0, keepdims=True))
        o_ref[pl.ds(t, 1), :] = h_ref[pl.ds(t, 1), :] + jnp.concatenate(mixed_lo + mixed_hi, axis=1)

    for s in range(PEER_LEAD):
        issue(s, s, 0, PEER_K)

    def group(i, carry):
        for s in range(PEER_SLOTS):
            token(i * PEER_SLOTS + s, s, True)
        return carry

    lax.fori_loop(0, tb // PEER_SLOTS - 1, group, 0)
    last = tb - PEER_SLOTS
    token(last, 0, True)
    for s in range(1, PEER_SLOTS):
        token(last + s, s, False)


def _peer(idx_flat, h, g, gate, uv):
    t = h.shape[0]
    tb = PEER_TB
    return pl.pallas_call(
        _peer_kernel,
        grid=(t // tb,),
        in_specs=[
            pl.BlockSpec(memory_space=pl.ANY),
            pl.BlockSpec((tb, D_MODEL), lambda i: (i, 0)),
            pl.BlockSpec((1, D_MODEL), lambda i: (0, 0)),
            pl.BlockSpec((tb, PEER_K), lambda i: (i, 0)),
            pl.BlockSpec(memory_space=pl.ANY),
        ],
        out_specs=pl.BlockSpec((tb, D_MODEL), lambda i: (i, 0)),
        out_shape=jax.ShapeDtypeStruct((t, D_MODEL), F32),
        scratch_shapes=[
            pltpu.SMEM((tb * PEER_K,), jnp.int32),
            pltpu.VMEM((tb, D_MODEL), F32),
        ] + [pltpu.VMEM((PEER_K, D_MODEL), jnp.uint32)] * PEER_SLOTS + [
            pltpu.SemaphoreType.DMA(()),
            pltpu.SemaphoreType.DMA((PEER_SLOTS,)),
        ],
        compiler_params=_params(("arbitrary",)),
        name="peer",
    )(idx_flat, h, g, gate, uv)


def _ple_kernel(h_ref, p_ref, gple_ref, wg_ref, wp_ref, gfin_ref, y_ref):
    h = h_ref[...]
    e = _rms(h, gple_ref[...]).astype(BF16)
    gate = _sigmoid(jnp.dot(e, wg_ref[...], preferred_element_type=F32))
    proj = jnp.dot(p_ref[...].astype(BF16), wp_ref[...], preferred_element_type=F32)
    y_ref[...] = _rms(h + gate * proj, gfin_ref[...])


def _ple(h, p, g_ple, wg, wp, g_fin, tm=256):
    t = h.shape[0]
    return pl.pallas_call(
        _ple_kernel,
        grid=(t // tm,),
        in_specs=[
            pl.BlockSpec((tm, D_MODEL), lambda i: (i, 0)),
            pl.BlockSpec((tm, PLE_DIM), lambda i: (i, 0)),
            pl.BlockSpec((1, D_MODEL), lambda i: (0, 0)),
            pl.BlockSpec((D_MODEL, D_MODEL), lambda i: (0, 0)),
            pl.BlockSpec((PLE_DIM, D_MODEL), lambda i: (0, 0)),
            pl.BlockSpec((1, D_MODEL), lambda i: (0, 0)),
        ],
        out_specs=pl.BlockSpec((tm, D_MODEL), lambda i: (i, 0)),
        out_shape=jax.ShapeDtypeStruct((t, D_MODEL), F32),
        compiler_params=_params(("arbitrary",)),
        name="ple",
    )(h, p, g_ple, wg, wp, g_fin)


def _rope_tables(pos):
    half = RET_D // 2
    inv = ROPE_BASE ** (-jnp.arange(half, dtype=F32) / half)
    ang = pos[:, None] * inv[None, :]
    cos, sin = jnp.cos(ang), jnp.sin(ang)
    return jnp.concatenate([cos, cos], axis=-1), jnp.concatenate([-sin, sin], axis=-1)


def _stream(x, p, pos, s0, c0, n0, m0, w):
    b, l, _ = x.shape
    t = b * l
    x2 = x.reshape(t, D_MODEL)
    z, gz = _in_proj(x2, w["g_mix"], w["w_in"], w["w_gates"])
    cosf, sinf = _rope_tables(pos)
    m0b = jnp.broadcast_to(m0[:, :, None], (b, MLSTM_HEADS, LANES))
    mix, s_new, c_new, n_new, m_new = _mixers(
        z.reshape(b, l, Z_COLS), gz.reshape(b, l, LANES), w["bias"], cosf, sinf, s0, c0, n0, m0b, w["g_ret"], w["g_ml"])
    h1 = _out_proj(mix.reshape(t, D_MODEL), w["w_out"], x2)
    idx, gate = _route(h1, w["g_ffn"], w["w_q"], w["k1"], w["k2"])
    h2 = _peer(idx.reshape(t * PEER_K), h1, w["g_ffn"], gate, w["uv"])
    y = _ple(h2, p.reshape(t, PLE_DIM), w["g_ple"], w["w_ple_gate"], w["w_ple_proj"], w["g_final"])
    return y.reshape(b, l, D_MODEL), s_new[None], c_new[None], n_new[None], m_new[None, :, :, 0]


def _prep_weights(g_mix, w_in, b_gates, g_ret, g_mlstm, w_out, g_ffn, w_peer_q, peer_keys1, peer_keys2, peer_u, peer_v,
                  g_ple, w_ple_gate, w_ple_proj, g_final):
    w_in0 = w_in[0]
    return {
        "g_mix": g_mix,
        "w_in": w_in0[:, :Z_COLS].astype(BF16),
        "w_gates": jnp.pad(w_in0[:, Z_COLS:], ((0, 0), (0, LANES - 2 * MLSTM_HEADS))).astype(BF16),
        "bias": jnp.pad(b_gates, ((0, 0), (0, LANES - 2 * MLSTM_HEADS))),
        "g_ret": g_ret,
        "g_ml": g_mlstm,
        "w_out": w_out[0].astype(BF16),
        "g_ffn": g_ffn,
        "w_q": w_peer_q[0].astype(BF16),
        "k1": peer_keys1[0].astype(BF16),
        "k2": peer_keys2[0].astype(BF16),
        "uv": jnp.concatenate([_pack_bf16_pairs(peer_u[0]), _pack_bf16_pairs(peer_v[0])], axis=1)[:, None, :],
        "g_ple": g_ple,
        "w_ple_gate": w_ple_gate[0].astype(BF16),
        "w_ple_proj": w_ple_proj[0].astype(BF16),
        "g_final": g_final[None, :],
    }


def kernel(x_prompt, x_sample, p_prompt, p_sample, state_ret, state_mlstm_C, state_mlstm_n, state_mlstm_m, g_mix, w_in, b_gates, g_ret, g_mlstm, w_out, g_ffn, w_peer_q, peer_keys1, peer_keys2, peer_u, peer_v, g_ple, w_ple_gate, w_ple_proj, g_final):
    w = _prep_weights(g_mix, w_in, b_gates, g_ret, g_mlstm, w_out, g_ffn, w_peer_q, peer_keys1, peer_keys2, peer_u, peer_v,
                      g_ple, w_ple_gate, w_ple_proj, g_final)
    bp, lp, _ = x_prompt.shape
    bs, ls, _ = x_sample.shape
    zeros = lambda *shape: jnp.zeros(shape, F32)
    past_len = 1024
    y_s, ret_s, c_s, n_s, m_s = _stream(
        x_sample, p_sample[0], past_len + jnp.arange(ls, dtype=F32),
        state_ret[0], state_mlstm_C[0], state_mlstm_n[0], state_mlstm_m[0], w)
    y_p, ret_p, c_p, n_p, m_p = _stream(
        x_prompt, p_prompt[0], jnp.arange(lp, dtype=F32),
        zeros(bp, RET_HEADS, RET_D, RET_D), zeros(bp, MLSTM_HEADS, MLSTM_DK, MLSTM_DV),
        zeros(bp, MLSTM_HEADS, MLSTM_DK), zeros(bp, MLSTM_HEADS), w)
    return (y_p, y_s, ret_p, c_p, n_p, m_p, ret_s, c_s, n_s, m_s)
```

```python
import functools
import math

import numpy as np
import jax
import jax.numpy as jnp
from jax import lax
from jax.experimental import pallas as pl
from jax.experimental.pallas import tpu as pltpu

F32 = jnp.float32
BF16 = jnp.bfloat16

D_MODEL = 2048
CHUNK = 64
RMS_EPS = 1e-6
ROPE_BASE = 10000.0
RET_HEADS = 8
RET_D = 128
MLSTM_HEADS = 4
MLSTM_DK = 128
MLSTM_DV = 256
Z_COLS = 7168
OFF_RQ, OFF_RK, OFF_RV, OFF_RG = 0, 1024, 2048, 3072
OFF_MQ, OFF_MK, OFF_MV, OFF_MO = 4096, 4608, 5120, 6144
PEER_HEADS = 8
N_KEYS = 128
PEER_TOPK = 16
PEER_K = PEER_HEADS * PEER_TOPK
PLE_DIM = 256
LANES = 128
SUBLANES = 8
CAND_ROWS = PEER_TOPK + (SUBLANES - 1) * SUBLANES + SUBLANES

LOG_GAMMA = [float(np.log(np.float32(1.0) - np.float32(2.0) ** np.float32(-5.0 - h))) for h in range(RET_HEADS)]

VMEM_LIMIT = 56 * 1024 * 1024


def _params(sem):
    return pltpu.CompilerParams(dimension_semantics=sem, vmem_limit_bytes=VMEM_LIMIT)


def _rms(x, g):
    return x * lax.rsqrt(jnp.mean(x * x, axis=-1, keepdims=True) + RMS_EPS) * g


def _mm(a, b):
    return jnp.dot(a.astype(BF16), b.astype(BF16), preferred_element_type=F32)


def _mm_nt(a, b):
    return lax.dot_general(a.astype(BF16), b.astype(BF16), (((1,), (1,)), ((), ())), preferred_element_type=F32)


def _sigmoid(x):
    return 1.0 / (1.0 + jnp.exp(-x))


def _in_proj_kernel(x_ref, g_ref, w_ref, wg_ref, z_ref, gz_ref, a_scr):
    @pl.when(pl.program_id(1) == 0)
    def _():
        a_scr[...] = _rms(x_ref[...], g_ref[...]).astype(BF16)
        gz_ref[...] = jnp.dot(a_scr[...], wg_ref[...], preferred_element_type=F32)

    z_ref[...] = jnp.dot(a_scr[...], w_ref[...], preferred_element_type=F32)


def _in_proj(x, g, w, wg, tm=1024, tn=1024):
    t = x.shape[0]
    tm = min(tm, t)
    return pl.pallas_call(
        _in_proj_kernel,
        grid=(t // tm, Z_COLS // tn),
        in_specs=[
            pl.BlockSpec((tm, D_MODEL), lambda i, j: (i, 0)),
            pl.BlockSpec((1, D_MODEL), lambda i, j: (0, 0)),
            pl.BlockSpec((D_MODEL, tn), lambda i, j: (0, j)),
            pl.BlockSpec((D_MODEL, LANES), lambda i, j: (0, 0)),
        ],
        out_specs=[
            pl.BlockSpec((tm, tn), lambda i, j: (i, j)),
            pl.BlockSpec((tm, LANES), lambda i, j: (i, 0)),
        ],
        out_shape=[jax.ShapeDtypeStruct((t, Z_COLS), F32), jax.ShapeDtypeStruct((t, LANES), F32)],
        scratch_shapes=[pltpu.VMEM((tm, D_MODEL), BF16)],
        compiler_params=_params(("arbitrary", "arbitrary")),
        name="in_proj",
    )(x, g, w, wg)


def _log_sigmoid(x):
    return -(jnp.maximum(-x, 0.0) + jnp.log1p(jnp.exp(-jnp.abs(x))))


def _mixers_kernel(z_ref, gz_ref, bias_ref, cos_ref, sin_ref, s0_ref, c0_ref, n0_ref, m0_ref, gret_ref, gml_ref,
                   mix_ref, s_ref, c_ref, n_ref, m_ref):
    @pl.when(pl.program_id(1) == 0)
    def _():
        s_ref[...] = s0_ref[...]
        c_ref[...] = c0_ref[...]
        n_ref[...] = n0_ref[...]
        m_ref[...] = m0_ref[...]

    cl = CHUNK
    row = lax.broadcasted_iota(jnp.int32, (cl, cl), 0)
    col = lax.broadcasted_iota(jnp.int32, (cl, cl), 1)
    causal = row >= col
    diff = jnp.where(causal, (row - col).astype(F32), 0.0)
    posc = lax.broadcasted_iota(jnp.int32, (cl, 1), 0).astype(F32)
    cosf = cos_ref[...]
    sinf = sin_ref[...]

    def rope(x):
        return x * cosf + pltpu.roll(x, RET_D // 2, axis=1) * sinf

    ret = []
    for h in range(RET_HEADS):
        lg = LOG_GAMMA[h]
        lo = h * RET_D
        q = rope(z_ref[0, :, OFF_RQ + lo:OFF_RQ + lo + RET_D])
        k = rope(z_ref[0, :, OFF_RK + lo:OFF_RK + lo + RET_D]) * (RET_D ** -0.5)
        q_dec = (q * jnp.exp((posc + 1.0) * lg)).astype(BF16)
        k_dec_t = (k * jnp.exp((cl - 1.0 - posc) * lg)).T.astype(BF16)
        ret.append((q.astype(BF16), k.astype(BF16), q_dec, k_dec_t))

    gates = gz_ref[0] + bias_ref[...]
    tri = causal.astype(F32)
    bcum = jnp.dot(tri, _log_sigmoid(gates), preferred_element_type=F32, precision=lax.Precision.HIGHEST)
    bcum_t = bcum.T
    gates_t = gates.T
    mls = []
    for h in range(MLSTM_HEADS):
        k = z_ref[0, :, OFF_MK + h * MLSTM_DK:OFF_MK + (h + 1) * MLSTM_DK] * (MLSTM_DK ** -0.5)
        f = MLSTM_HEADS + h
        b_col = bcum[:, f:f + 1]
        b_row = bcum_t[f:f + 1, :]
        ig_col = gates[:, h:h + 1]
        ig_row = gates_t[h:h + 1, :]
        m_prev = m_ref[0, h:h + 1, 0:1]
        dlog = jnp.where(causal, b_col - b_row + ig_row, -jnp.inf)
        inter_log = b_col + m_prev
        m_t = jnp.maximum(inter_log, jnp.max(dlog, axis=-1, keepdims=True))
        dw = jnp.exp(dlog - m_t)
        inter_w = jnp.exp(inter_log - m_t)
        m_new = m_t[cl - 1:cl, :]
        b_last = b_col[cl - 1:cl, :]
        ws = jnp.exp(b_last - b_col + ig_col - m_new)
        carry = jnp.exp(b_last + m_prev - m_new)
        kw = k * ws
        n_old = n_ref[0, h:h + 1, :]
        n_ref[0, h:h + 1, :] = carry * n_old + jnp.sum(kw, axis=0, keepdims=True)
        m_ref[0, h:h + 1, :] = jnp.broadcast_to(m_new, (1, LANES))
        mls.append((k.astype(BF16), kw.T.astype(BF16), dw, inter_w, m_t, carry, n_old))

    for h in range(RET_HEADS):
        lg = LOG_GAMMA[h]
        lo = h * RET_D
        q, k, q_dec, k_dec_t = ret[h]
        v = z_ref[0, :, OFF_RV + lo:OFF_RV + lo + RET_D].astype(BF16)
        rg = z_ref[0, :, OFF_RG + lo:OFF_RG + lo + RET_D]
        decay = jnp.where(causal, jnp.exp(diff * lg), 0.0)
        s_old = s_ref[0, h]
        scores = _mm_nt(q, k) * decay
        o = _mm(scores, v) + _mm(q_dec, s_old)
        s_ref[0, h] = math.exp(cl * lg) * s_old + _mm(k_dec_t, v)
        y = o * lax.rsqrt(jnp.mean(o * o, axis=-1, keepdims=True) + RMS_EPS) * gret_ref[:, lo:lo + RET_D]
        mix_ref[0, :, lo:lo + RET_D] = (y * (rg * _sigmoid(rg))).astype(BF16)

    for h in range(MLSTM_HEADS):
        k, kw_t, dw, inter_w, m_t, carry, n_old = mls[h]
        q = z_ref[0, :, OFF_MQ + h * MLSTM_DK:OFF_MQ + (h + 1) * MLSTM_DK]
        v = z_ref[0, :, OFF_MV + h * MLSTM_DV:OFF_MV + (h + 1) * MLSTM_DV].astype(BF16)
        mo = z_ref[0, :, OFF_MO + h * MLSTM_DV:OFF_MO + (h + 1) * MLSTM_DV]
        c_old = c_ref[0, h]
        sm = _mm_nt(q, k) * dw
        num = _mm(sm, v) + inter_w * _mm(q, c_old)
        den = jnp.sum(sm, axis=-1, keepdims=True) + inter_w * jnp.sum(q * n_old, axis=-1, keepdims=True)
        hh = num / jnp.maximum(jnp.abs(den), jnp.exp(-m_t))
        c_ref[0, h] = carry * c_old + _mm(kw_t, v)
        y = hh * lax.rsqrt(jnp.mean(hh * hh, axis=-1, keepdims=True) + RMS_EPS) * gml_ref[:, h * MLSTM_DV:(h + 1) * MLSTM_DV]
        lo = RET_HEADS * RET_D + h * MLSTM_DV
        mix_ref[0, :, lo:lo + MLSTM_DV] = (y * _sigmoid(mo)).astype(BF16)


def _mixers(z, gz, bias, cosf, sinf, s0, c0, n0, m0, g_ret, g_ml):
    b, l, _ = z.shape
    per_b = lambda *tail: (lambda i, c: (i,) + tail)
    return pl.pallas_call(
        _mixers_kernel,
        grid=(b, l // CHUNK),
        in_specs=[
            pl.BlockSpec((1, CHUNK, Z_COLS), lambda i, c: (i, c, 0)),
            pl.BlockSpec((1, CHUNK, LANES), lambda i, c: (i, c, 0)),
            pl.BlockSpec((1, LANES), lambda i, c: (0, 0)),
            pl.BlockSpec((CHUNK, RET_D), lambda i, c: (c, 0)),
            pl.BlockSpec((CHUNK, RET_D), lambda i, c: (c, 0)),
            pl.BlockSpec((1, RET_HEADS, RET_D, RET_D), per_b(0, 0, 0)),
            pl.BlockSpec((1, MLSTM_HEADS, MLSTM_DK, MLSTM_DV), per_b(0, 0, 0)),
            pl.BlockSpec((1, MLSTM_HEADS, MLSTM_DK), per_b(0, 0)),
            pl.BlockSpec((1, MLSTM_HEADS, LANES), per_b(0, 0)),
            pl.BlockSpec((1, RET_HEADS * RET_D), lambda i, c: (0, 0)),
            pl.BlockSpec((1, MLSTM_HEADS * MLSTM_DV), lambda i, c: (0, 0)),
        ],
        out_specs=[
            pl.BlockSpec((1, CHUNK, D_MODEL), lambda i, c: (i, c, 0)),
            pl.BlockSpec((1, RET_HEADS, RET_D, RET_D), per_b(0, 0, 0)),
            pl.BlockSpec((1, MLSTM_HEADS, MLSTM_DK, MLSTM_DV), per_b(0, 0, 0)),
            pl.BlockSpec((1, MLSTM_HEADS, MLSTM_DK), per_b(0, 0)),
            pl.BlockSpec((1, MLSTM_HEADS, LANES), per_b(0, 0)),
        ],
        out_shape=[
            jax.ShapeDtypeStruct((b, l, D_MODEL), BF16),
            jax.ShapeDtypeStruct((b, RET_HEADS, RET_D, RET_D), F32),
            jax.ShapeDtypeStruct((b, MLSTM_HEADS, MLSTM_DK, MLSTM_DV), F32),
            jax.ShapeDtypeStruct((b, MLSTM_HEADS, MLSTM_DK), F32),
            jax.ShapeDtypeStruct((b, MLSTM_HEADS, LANES), F32),
        ],
        compiler_params=_params(("arbitrary", "arbitrary")),
        name="mixers",
    )(z, gz, bias, cosf, sinf, s0, c0, n0, m0, g_ret, g_ml)


def _out_proj_kernel(a_ref, w_ref, r_ref, o_ref):
    o_ref[...] = r_ref[...] + jnp.dot(a_ref[...], w_ref[...], preferred_element_type=F32)


def _out_proj(a, w, r, tm=512):
    t = a.shape[0]
    return pl.pallas_call(
        _out_proj_kernel,
        grid=(t // tm,),
        in_specs=[
            pl.BlockSpec((tm, D_MODEL), lambda i: (i, 0)),
            pl.BlockSpec((D_MODEL, D_MODEL), lambda i: (0, 0)),
            pl.BlockSpec((tm, D_MODEL), lambda i: (i, 0)),
        ],
        out_specs=pl.BlockSpec((tm, D_MODEL), lambda i: (i, 0)),
        out_shape=jax.ShapeDtypeStruct((t, D_MODEL), F32),
        compiler_params=_params(("arbitrary",)),
        name="out_proj",
    )(a, w, r)


def _top16(x_ref, payload_ref, val_ref, idx_ref):
    x = x_ref[...]
    rows = x.shape[0]
    iota = lax.broadcasted_iota(jnp.int32, x.shape, 0)
    for r in range(PEER_TOPK):
        m = jnp.max(x, axis=0, keepdims=True)
        am = jnp.min(jnp.where(x == m, iota, rows), axis=0, keepdims=True)
        sel = iota == am
        val_ref[r:r + 1, :] = m
        if payload_ref is None:
            idx_ref[r:r + 1, :] = am
        else:
            idx_ref[r:r + 1, :] = jnp.max(jnp.where(sel, payload_ref[...], -1), axis=0, keepdims=True)
        x = jnp.where(sel, -jnp.inf, x)


def _route_kernel(h_ref, g_ref, wq_ref, k1_ref, k2_ref, idx_ref, gate_ref,
                  q_scr, s_scr, t1_scr, i1_scr, t2_scr, i2_scr, cand_scr, cidx_scr, sc_scr, e_scr, et_scr, gt_scr):
    tm = h_ref.shape[0]
    c = _rms(h_ref[...], g_ref[...]).astype(BF16)
    q_scr[...] = jnp.dot(c, wq_ref[...], preferred_element_type=F32).astype(BF16)
    half = N_KEYS
    for h in range(PEER_HEADS):
        for sub in range(tm // LANES):
            tok = slice(sub * LANES, (sub + 1) * LANES)
            q1 = q_scr[tok, 2 * h * half:(2 * h + 1) * half]
            q2 = q_scr[tok, (2 * h + 1) * half:(2 * h + 2) * half]
            s_scr[...] = _mm_nt(k1_ref[...], q1)
            _top16(s_scr, None, t1_scr, i1_scr)
            s_scr[...] = _mm_nt(k2_ref[...], q2)
            _top16(s_scr, None, t2_scr, i2_scr)
            cand_scr[0:PEER_TOPK, :] = t1_scr[0:1, :] + t2_scr[...]
            cidx_scr[0:PEER_TOPK, :] = i1_scr[0:1, :] * N_KEYS + i2_scr[...]
            t2 = t2_scr[0:SUBLANES, :]
            i2 = i2_scr[0:SUBLANES, :]
            sub_iota = lax.broadcasted_iota(jnp.int32, (SUBLANES, LANES), 0)
            for a in range(1, SUBLANES):
                lo = PEER_TOPK + (a - 1) * SUBLANES
                cand_scr[lo:lo + SUBLANES, :] = jnp.where(sub_iota < PEER_TOPK // (a + 1), t1_scr[a:a + 1, :] + t2, -jnp.inf)
                cidx_scr[lo:lo + SUBLANES, :] = i1_scr[a:a + 1, :] * N_KEYS + i2
            cand_scr[CAND_ROWS - SUBLANES:CAND_ROWS, :] = t1_scr[SUBLANES:PEER_TOPK, :] + t2_scr[0:1, :]
            cidx_scr[CAND_ROWS - SUBLANES:CAND_ROWS, :] = i1_scr[SUBLANES:PEER_TOPK, :] * N_KEYS + i2_scr[0:1, :]
            _top16(cand_scr, cidx_scr, sc_scr, e_scr)
            sc = sc_scr[...]
            p = jnp.exp(sc - sc[0:1, :])
            gt_scr[h * PEER_TOPK:(h + 1) * PEER_TOPK, tok] = p / jnp.sum(p, axis=0, keepdims=True)
            et_scr[h * PEER_TOPK:(h + 1) * PEER_TOPK, tok] = e_scr[...]
    idx_ref[...] = et_scr[...].T
    gate_ref[...] = gt_scr[...].T


def _route(h, g, wq, k1, k2, tm=256):
    t = h.shape[0]
    return pl.pallas_call(
        _route_kernel,
        grid=(t // tm,),
        in_specs=[
            pl.BlockSpec((tm, D_MODEL), lambda i: (i, 0)),
            pl.BlockSpec((1, D_MODEL), lambda i: (0, 0)),
            pl.BlockSpec((D_MODEL, D_MODEL), lambda i: (0, 0)),
            pl.BlockSpec((N_KEYS, N_KEYS), lambda i: (0, 0)),
            pl.BlockSpec((N_KEYS, N_KEYS), lambda i: (0, 0)),
        ],
        out_specs=[
            pl.BlockSpec((tm, PEER_K), lambda i: (i, 0)),
            pl.BlockSpec((tm, PEER_K), lambda i: (i, 0)),
        ],
        out_shape=[jax.ShapeDtypeStruct((t, PEER_K), jnp.int32), jax.ShapeDtypeStruct((t, PEER_K), F32)],
        scratch_shapes=[
            pltpu.VMEM((tm, D_MODEL), BF16),
            pltpu.VMEM((N_KEYS, LANES), F32),
            pltpu.VMEM((PEER_TOPK, LANES), F32),
            pltpu.VMEM((PEER_TOPK, LANES), jnp.int32),
            pltpu.VMEM((PEER_TOPK, LANES), F32),
            pltpu.VMEM((PEER_TOPK, LANES), jnp.int32),
            pltpu.VMEM((CAND_ROWS, LANES), F32),
            pltpu.VMEM((CAND_ROWS, LANES), jnp.int32),
            pltpu.VMEM((PEER_TOPK, LANES), F32),
            pltpu.VMEM((PEER_TOPK, LANES), jnp.int32),
            pltpu.VMEM((PEER_K, tm), jnp.int32),
            pltpu.VMEM((PEER_K, tm), F32),
        ],
        compiler_params=_params(("arbitrary",)),
        name="route",
    )(h, g, wq, k1, k2)


PEER_TB = 512
PEER_SLOTS = 8
PEER_LEAD = PEER_SLOTS - 1
HALF_D = D_MODEL // 2
PACK_TILES = HALF_D // LANES
INV_SQRT2 = 0.7071067811865476


def _pack_bf16_pairs(a):
    bits = lax.bitcast_convert_type(a.astype(BF16), jnp.uint16).astype(jnp.uint32)
    return bits[:, :HALF_D] | (bits[:, HALF_D:] << 16)


def _peer_kernel(idx_hbm, h_ref, g_ref, gate_ref, uv_hbm, o_ref, idx_s, c_scr, *scratch):
    rows = scratch[:PEER_SLOTS]
    idx_sem, row_sems = scratch[PEER_SLOTS:]
    tb = h_ref.shape[0]
    step = pl.program_id(0)
    idx_copy = pltpu.make_async_copy(idx_hbm.at[pl.ds(step * (tb * PEER_K), tb * PEER_K)], idx_s, idx_sem)
    idx_copy.start()
    c_scr[...] = _rms(h_ref[...], g_ref[...])
    idx_copy.wait()

    def issue(t, slot, k0, k1):
        for k in range(k0, k1):
            e = idx_s[t * PEER_K + k]
            pltpu.make_async_copy(uv_hbm.at[e], rows[slot].at[pl.ds(k, 1), :], row_sems.at[slot]).start(priority=k % 2)

    def wait(slot):
        pltpu.make_async_copy(uv_hbm.at[pl.ds(0, PEER_K), 0], rows[slot], row_sems.at[slot]).wait()

    def unpack(words):
        lo = lax.bitcast_convert_type(jnp.left_shift(words, jnp.uint32(16)), F32)
        hi = lax.bitcast_convert_type(jnp.bitwise_and(words, jnp.uint32(0xFFFF0000)), F32)
        return lo, hi

    eye = lax.broadcasted_iota(jnp.int32, (PEER_K, LANES), 0) == lax.broadcasted_iota(jnp.int32, (PEER_K, LANES), 1)
    per_tile = PEER_K // PACK_TILES

    def token(t, slot, prefetch):
        wait(slot)
        ahead_slot = (slot + PEER_LEAD) % PEER_SLOTS
        x = c_scr[pl.ds(t, 1), :]
        acc = None
        for j in range(PACK_TILES):
            if prefetch:
                issue(t + PEER_LEAD, ahead_slot, j * per_tile, (j + 1) * per_tile)
            lo, hi = unpack(rows[slot][:, j * LANES:(j + 1) * LANES])
            p = lo * x[:, j * LANES:(j + 1) * LANES] + hi * x[:, HALF_D + j * LANES:HALF_D + (j + 1) * LANES]
            acc = p if acc is None else acc + p
        act = jnp.sum(acc, axis=1, keepdims=True)
        gate_col = jnp.sum(jnp.where(eye, gate_ref[pl.ds(t, 1), :], 0.0), axis=1, keepdims=True)
        coef = gate_col * (0.5 * act * (1.0 + lax.erf(act * INV_SQRT2)))
        mixed_lo, mixed_hi = [], []
        for j in range(PACK_TILES):
            lo, hi = unpack(rows[slot][:, HALF_D + j * LANES:HALF_D + (j + 1) * LANES])
            mixed_lo.append(jnp.sum(lo * coef, axis=0, keepdims=True))
            mixed_hi.append(jnp.sum(hi * coef, axis=0, keepdims=True))
        o_ref[pl.ds(t, 1), :] = h_ref[pl.ds(t, 1), :] + jnp.concatenate(mixed_lo + mixed_hi, axis=1)

    for s in range(PEER_LEAD):
        issue(s, s, 0, PEER_K)

    def group(i, carry):
        for s in range(PEER_SLOTS):
            token(i * PEER_SLOTS + s, s, True)
        return carry

    lax.fori_loop(0, tb // PEER_SLOTS - 1, group, 0)
    last = tb - PEER_SLOTS
    token(last, 0, True)
    for s in range(1, PEER_SLOTS):
        token(last + s, s, False)


def _peer(idx_flat, h, g, gate, uv):
    t = h.shape[0]
    tb = PEER_TB
    return pl.pallas_call(
        _peer_kernel,
        grid=(t // tb,),
        in_specs=[
            pl.BlockSpec(memory_space=pl.ANY),
            pl.BlockSpec((tb, D_MODEL), lambda i: (i, 0)),
            pl.BlockSpec((1, D_MODEL), lambda i: (0, 0)),
            pl.BlockSpec((tb, PEER_K), lambda i: (i, 0)),
            pl.BlockSpec(memory_space=pl.ANY),
        ],
        out_specs=pl.BlockSpec((tb, D_MODEL), lambda i: (i, 0)),
        out_shape=jax.ShapeDtypeStruct((t, D_MODEL), F32),
        scratch_shapes=[
            pltpu.SMEM((tb * PEER_K,), jnp.int32),
            pltpu.VMEM((tb, D_MODEL), F32),
        ] + [pltpu.VMEM((PEER_K, D_MODEL), jnp.uint32)] * PEER_SLOTS + [
            pltpu.SemaphoreType.DMA(()),
            pltpu.SemaphoreType.DMA((PEER_SLOTS,)),
        ],
        compiler_params=_params(("arbitrary",)),
        name="peer",
    )(idx_flat, h, g, gate, uv)


def _ple_kernel(h_ref, p_ref, gple_ref, wg_ref, wp_ref, gfin_ref, y_ref):
    h = h_ref[...]
    e = _rms(h, gple_ref[...]).astype(BF16)
    gate = _sigmoid(jnp.dot(e, wg_ref[...], preferred_element_type=F32))
    proj = jnp.dot(p_ref[...].astype(BF16), wp_ref[...], preferred_element_type=F32)
    y_ref[...] = _rms(h + gate * proj, gfin_ref[...])


def _ple(h, p, g_ple, wg, wp, g_fin, tm=256):
    t = h.shape[0]
    return pl.pallas_call(
        _ple_kernel,
        grid=(t // tm,),
        in_specs=[
            pl.BlockSpec((tm, D_MODEL), lambda i: (i, 0)),
            pl.BlockSpec((tm, PLE_DIM), lambda i: (i, 0)),
            pl.BlockSpec((1, D_MODEL), lambda i: (0, 0)),
            pl.BlockSpec((D_MODEL, D_MODEL), lambda i: (0, 0)),
            pl.BlockSpec((PLE_DIM, D_MODEL), lambda i: (0, 0)),
            pl.BlockSpec((1, D_MODEL), lambda i: (0, 0)),
        ],
        out_specs=pl.BlockSpec((tm, D_MODEL), lambda i: (i, 0)),
        out_shape=jax.ShapeDtypeStruct((t, D_MODEL), F32),
        compiler_params=_params(("arbitrary",)),
        name="ple",
    )(h, p, g_ple, wg, wp, g_fin)


def _rope_tables(pos):
    half = RET_D // 2
    inv = ROPE_BASE ** (-jnp.arange(half, dtype=F32) / half)
    ang = pos[:, None] * inv[None, :]
    cos, sin = jnp.cos(ang), jnp.sin(ang)
    return jnp.concatenate([cos, cos], axis=-1), jnp.concatenate([-sin, sin], axis=-1)


def _stream(x, p, pos, s0, c0, n0, m0, w):
    b, l, _ = x.shape
    t = b * l
    x2 = x.reshape(t, D_MODEL)
    z, gz = _in_proj(x2, w["g_mix"], w["w_in"], w["w_gates"])
    cosf, sinf = _rope_tables(pos)
    m0b = jnp.broadcast_to(m0[:, :, None], (b, MLSTM_HEADS, LANES))
    mix, s_new, c_new, n_new, m_new = _mixers(
        z.reshape(b, l, Z_COLS), gz.reshape(b, l, LANES), w["bias"], cosf, sinf, s0, c0, n0, m0b, w["g_ret"], w["g_ml"])
    h1 = _out_proj(mix.reshape(t, D_MODEL), w["w_out"], x2)
    idx, gate = _route(h1, w["g_ffn"], w["w_q"], w["k1"], w["k2"])
    h2 = _peer(idx.reshape(t * PEER_K), h1, w["g_ffn"], gate, w["uv"])
    y = _ple(h2, p.reshape(t, PLE_DIM), w["g_ple"], w["w_ple_gate"], w["w_ple_proj"], w["g_final"])
    return y.reshape(b, l, D_MODEL), s_new[None], c_new[None], n_new[None], m_new[None, :, :, 0]


def _prep_weights(g_mix, w_in, b_gates, g_ret, g_mlstm, w_out, g_ffn, w_peer_q, peer_keys1, peer_keys2, peer_u, peer_v,
                  g_ple, w_ple_gate, w_ple_proj, g_final):
    w_in0 = w_in[0]
    return {
        "g_mix": g_mix,
        "w_in": w_in0[:, :Z_COLS].astype(BF16),
        "w_gates": jnp.pad(w_in0[:, Z_COLS:], ((0, 0), (0, LANES - 2 * MLSTM_HEADS))).astype(BF16),
        "bias": jnp.pad(b_gates, ((0, 0), (0, LANES - 2 * MLSTM_HEADS))),
        "g_ret": g_ret,
        "g_ml": g_mlstm,
        "w_out": w_out[0].astype(BF16),
        "g_ffn": g_ffn,
        "w_q": w_peer_q[0].astype(BF16),
        "k1": peer_keys1[0].astype(BF16),
        "k2": peer_keys2[0].astype(BF16),
        "uv": jnp.concatenate([_pack_bf16_pairs(peer_u[0]), _pack_bf16_pairs(peer_v[0])], axis=1)[:, None, :],
        "g_ple": g_ple,
        "w_ple_gate": w_ple_gate[0].astype(BF16),
        "w_ple_proj": w_ple_proj[0].astype(BF16),
        "g_final": g_final[None, :],
    }


def kernel(x_prompt, x_sample, p_prompt, p_sample, state_ret, state_mlstm_C, state_mlstm_n, state_mlstm_m, g_mix, w_in, b_gates, g_ret, g_mlstm, w_out, g_ffn, w_peer_q, peer_keys1, peer_keys2, peer_u, peer_v, g_ple, w_ple_gate, w_ple_proj, g_final):
    w = _prep_weights(g_mix, w_in, b_gates, g_ret, g_mlstm, w_out, g_ffn, w_peer_q, peer_keys1, peer_keys2, peer_u, peer_v,
                      g_ple, w_ple_gate, w_ple_proj, g_final)
    bp, lp, _ = x_prompt.shape
    bs, ls, _ = x_sample.shape
    zeros = lambda *shape: jnp.zeros(shape, F32)
    past_len = 1024
    y_s, ret_s, c_s, n_s, m_s = _stream(
        x_sample, p_sample[0], past_len + jnp.arange(ls, dtype=F32),
        state_ret[0], state_mlstm_C[0], state_mlstm_n[0], state_mlstm_m[0], w)
    y_p, ret_p, c_p, n_p, m_p = _stream(
        x_prompt, p_prompt[0], jnp.arange(lp, dtype=F32),
        zeros(bp, RET_HEADS, RET_D, RET_D), zeros(bp, MLSTM_HEADS, MLSTM_DK, MLSTM_DV),
        zeros(bp, MLSTM_HEADS, MLSTM_DK), zeros(bp, MLSTM_HEADS), w)
    return (y_p, y_s, ret_p, c_p, n_p, m_p, ret_s, c_s, n_s, m_s)
```

```python
import functools
import math

import numpy as np
import jax
import jax.numpy as jnp
from jax import lax
from jax.experimental import pallas as pl
from jax.experimental.pallas import tpu as pltpu

F32 = jnp.float32
BF16 = jnp.bfloat16

D_MODEL = 2048
CHUNK = 64
RMS_EPS = 1e-6
ROPE_BASE = 10000.0
RET_HEADS = 8
RET_D = 128
MLSTM_HEADS = 4
MLSTM_DK = 128
MLSTM_DV = 256
Z_COLS = 7168
OFF_RQ, OFF_RK, OFF_RV, OFF_RG = 0, 1024, 2048, 3072
OFF_MQ, OFF_MK, OFF_MV, OFF_MO = 4096, 4608, 5120, 6144
PEER_HEADS = 8
N_KEYS = 128
PEER_TOPK = 16
PEER_K = PEER_HEADS * PEER_TOPK
PLE_DIM = 256
LANES = 128
SUBLANES = 8
CAND_ROWS = PEER_TOPK + (SUBLANES - 1) * SUBLANES + SUBLANES

LOG_GAMMA = [float(np.log(np.float32(1.0) - np.float32(2.0) ** np.float32(-5.0 - h))) for h in range(RET_HEADS)]

VMEM_LIMIT = 56 * 1024 * 1024


def _params(sem):
    return pltpu.CompilerParams(dimension_semantics=sem, vmem_limit_bytes=VMEM_LIMIT)


def _rms(x, g):
    return x * lax.rsqrt(jnp.mean(x * x, axis=-1, keepdims=True) + RMS_EPS) * g


def _mm(a, b):
    return jnp.dot(a.astype(BF16), b.astype(BF16), preferred_element_type=F32)


def _mm_nt(a, b):
    return lax.dot_general(a.astype(BF16), b.astype(BF16), (((1,), (1,)), ((), ())), preferred_element_type=F32)


def _sigmoid(x):
    return 1.0 / (1.0 + jnp.exp(-x))


def _in_proj_kernel(x_ref, g_ref, w_ref, wg_ref, z_ref, gz_ref, a_scr):
    @pl.when(pl.program_id(1) == 0)
    def _():
        a_scr[...] = _rms(x_ref[...], g_ref[...]).astype(BF16)
        gz_ref[...] = jnp.dot(a_scr[...], wg_ref[...], preferred_element_type=F32)

    z_ref[...] = jnp.dot(a_scr[...], w_ref[...], preferred_element_type=F32)


def _in_proj(x, g, w, wg, tm=1024, tn=1024):
    t = x.shape[0]
    tm = min(tm, t)
    return pl.pallas_call(
        _in_proj_kernel,
        grid=(t // tm, Z_COLS // tn),
        in_specs=[
            pl.BlockSpec((tm, D_MODEL), lambda i, j: (i, 0)),
            pl.BlockSpec((1, D_MODEL), lambda i, j: (0, 0)),
            pl.BlockSpec((D_MODEL, tn), lambda i, j: (0, j)),
            pl.BlockSpec((D_MODEL, LANES), lambda i, j: (0, 0)),
        ],
        out_specs=[
            pl.BlockSpec((tm, tn), lambda i, j: (i, j)),
            pl.BlockSpec((tm, LANES), lambda i, j: (i, 0)),
        ],
        out_shape=[jax.ShapeDtypeStruct((t, Z_COLS), F32), jax.ShapeDtypeStruct((t, LANES), F32)],
        scratch_shapes=[pltpu.VMEM((tm, D_MODEL), BF16)],
        compiler_params=_params(("arbitrary", "arbitrary")),
        name="in_proj",
    )(x, g, w, wg)


def _log_sigmoid(x):
    return -(jnp.maximum(-x, 0.0) + jnp.log1p(jnp.exp(-jnp.abs(x))))


def _mixers_kernel(z_ref, gz_ref, bias_ref, cos_ref, sin_ref, s0_ref, c0_ref, n0_ref, m0_ref, gret_ref, gml_ref,
                   mix_ref, s_ref, c_ref, n_ref, m_ref):
    @pl.when(pl.program_id(1) == 0)
    def _():
        s_ref[...] = s0_ref[...]
        c_ref[...] = c0_ref[...]
        n_ref[...] = n0_ref[...]
        m_ref[...] = m0_ref[...]

    cl = CHUNK
    row = lax.broadcasted_iota(jnp.int32, (cl, cl), 0)
    col = lax.broadcasted_iota(jnp.int32, (cl, cl), 1)
    causal = row >= col
    diff = jnp.where(causal, (row - col).astype(F32), 0.0)
    posc = lax.broadcasted_iota(jnp.int32, (cl, 1), 0).astype(F32)
    cosf = cos_ref[...]
    sinf = sin_ref[...]

    def rope(x):
        return x * cosf + pltpu.roll(x, RET_D // 2, axis=1) * sinf

    ret = []
    for h in range(RET_HEADS):
        lg = LOG_GAMMA[h]
        lo = h * RET_D
        q = rope(z_ref[0, :, OFF_RQ + lo:OFF_RQ + lo + RET_D])
        k = rope(z_ref[0, :, OFF_RK + lo:OFF_RK + lo + RET_D]) * (RET_D ** -0.5)
        q_dec = (q * jnp.exp((posc + 1.0) * lg)).astype(BF16)
        k_dec_t = (k * jnp.exp((cl - 1.0 - posc) * lg)).T.astype(BF16)
        ret.append((q.astype(BF16), k.astype(BF16), q_dec, k_dec_t))

    gates = gz_ref[0] + bias_ref[...]
    tri = causal.astype(F32)
    bcum = jnp.dot(tri, _log_sigmoid(gates), preferred_element_type=F32, precision=lax.Precision.HIGHEST)
    bcum_t = bcum.T
    gates_t = gates.T
    mls = []
    for h in range(MLSTM_HEADS):
        k = z_ref[0, :, OFF_MK + h * MLSTM_DK:OFF_MK + (h + 1) * MLSTM_DK] * (MLSTM_DK ** -0.5)
        f = MLSTM_HEADS + h
        b_col = bcum[:, f:f + 1]
        b_row = bcum_t[f:f + 1, :]
        ig_col = gates[:, h:h + 1]
        ig_row = gates_t[h:h + 1, :]
        m_prev = m_ref[0, h:h + 1, 0:1]
        dlog = jnp.where(causal, b_col - b_row + ig_row, -jnp.inf)
        inter_log = b_col + m_prev
        m_t = jnp.maximum(inter_log, jnp.max(dlog, axis=-1, keepdims=True))
        dw = jnp.exp(dlog - m_t)
        inter_w = jnp.exp(inter_log - m_t)
        m_new = m_t[cl - 1:cl, :]
        b_last = b_col[cl - 1:cl, :]
        ws = jnp.exp(b_last - b_col + ig_col - m_new)
        carry = jnp.exp(b_last + m_prev - m_new)
        kw = k * ws
        n_old = n_ref[0, h:h + 1, :]
        n_ref[0, h:h + 1, :] = carry * n_old + jnp.sum(kw, axis=0, keepdims=True)
        m_ref[0, h:h + 1, :] = jnp.broadcast_to(m_new, (1, LANES))
        mls.append((k.astype(BF16), kw.T.astype(BF16), dw, inter_w, m_t, carry, n_old))

    for h in range(RET_HEADS):
        lg = LOG_GAMMA[h]
        lo = h * RET_D
        q, k, q_dec, k_dec_t = ret[h]
        v = z_ref[0, :, OFF_RV + lo:OFF_RV + lo + RET_D].astype(BF16)
        rg = z_ref[0, :, OFF_RG + lo:OFF_RG + lo + RET_D]
        decay = jnp.where(causal, jnp.exp(diff * lg), 0.0)
        s_old = s_ref[0, h]
        scores = _mm_nt(q, k) * decay
        o = _mm(scores, v) + _mm(q_dec, s_old)
        s_ref[0, h] = math.exp(cl * lg) * s_old + _mm(k_dec_t, v)
        y = o * lax.rsqrt(jnp.mean(o * o, axis=-1, keepdims=True) + RMS_EPS) * gret_ref[:, lo:lo + RET_D]
        mix_ref[0, :, lo:lo + RET_D] = (y * (rg * _sigmoid(rg))).astype(BF16)

    for h in range(MLSTM_HEADS):
        k, kw_t, dw, inter_w, m_t, carry, n_old = mls[h]
        q = z_ref[0, :, OFF_MQ + h * MLSTM_DK:OFF_MQ + (h + 1) * MLSTM_DK]
        v = z_ref[0, :, OFF_MV + h * MLSTM_DV:OFF_MV + (h + 1) * MLSTM_DV].astype(BF16)
        mo = z_ref[0, :, OFF_MO + h * MLSTM_DV:OFF_MO + (h + 1) * MLSTM_DV]
        c_old = c_ref[0, h]
        sm = _mm_nt(q, k) * dw
        num = _mm(sm, v) + inter_w * _mm(q, c_old)
        den = jnp.sum(sm, axis=-1, keepdims=True) + inter_w * jnp.sum(q * n_old, axis=-1, keepdims=True)
        hh = num / jnp.maximum(jnp.abs(den), jnp.exp(-m_t))
        c_ref[0, h] = carry * c_old + _mm(kw_t, v)
        y = hh * lax.rsqrt(jnp.mean(hh * hh, axis=-1, keepdims=True) + RMS_EPS) * gml_ref[:, h * MLSTM_DV:(h + 1) * MLSTM_DV]
        lo = RET_HEADS * RET_D + h * MLSTM_DV
        mix_ref[0, :, lo:lo + MLSTM_DV] = (y * _sigmoid(mo)).astype(BF16)


def _mixers(z, gz, bias, cosf, sinf, s0, c0, n0, m0, g_ret, g_ml):
    b, l, _ = z.shape
    per_b = lambda *tail: (lambda i, c: (i,) + tail)
    return pl.pallas_call(
        _mixers_kernel,
        grid=(b, l // CHUNK),
        in_specs=[
            pl.BlockSpec((1, CHUNK, Z_COLS), lambda i, c: (i, c, 0)),
            pl.BlockSpec((1, CHUNK, LANES), lambda i, c: (i, c, 0)),
            pl.BlockSpec((1, LANES), lambda i, c: (0, 0)),
            pl.BlockSpec((CHUNK, RET_D), lambda i, c: (c, 0)),
            pl.BlockSpec((CHUNK, RET_D), lambda i, c: (c, 0)),
            pl.BlockSpec((1, RET_HEADS, RET_D, RET_D), per_b(0, 0, 0)),
            pl.BlockSpec((1, MLSTM_HEADS, MLSTM_DK, MLSTM_DV), per_b(0, 0, 0)),
            pl.BlockSpec((1, MLSTM_HEADS, MLSTM_DK), per_b(0, 0)),
            pl.BlockSpec((1, MLSTM_HEADS, LANES), per_b(0, 0)),
            pl.BlockSpec((1, RET_HEADS * RET_D), lambda i, c: (0, 0)),
            pl.BlockSpec((1, MLSTM_HEADS * MLSTM_DV), lambda i, c: (0, 0)),
        ],
        out_specs=[
            pl.BlockSpec((1, CHUNK, D_MODEL), lambda i, c: (i, c, 0)),
            pl.BlockSpec((1, RET_HEADS, RET_D, RET_D), per_b(0, 0, 0)),
            pl.BlockSpec((1, MLSTM_HEADS, MLSTM_DK, MLSTM_DV), per_b(0, 0, 0)),
            pl.BlockSpec((1, MLSTM_HEADS, MLSTM_DK), per_b(0, 0)),
            pl.BlockSpec((1, MLSTM_HEADS, LANES), per_b(0, 0)),
        ],
        out_shape=[
            jax.ShapeDtypeStruct((b, l, D_MODEL), BF16),
            jax.ShapeDtypeStruct((b, RET_HEADS, RET_D, RET_D), F32),
            jax.ShapeDtypeStruct((b, MLSTM_HEADS, MLSTM_DK, MLSTM_DV), F32),
            jax.ShapeDtypeStruct((b, MLSTM_HEADS, MLSTM_DK), F32),
            jax.ShapeDtypeStruct((b, MLSTM_HEADS, LANES), F32),
        ],
        compiler_params=_params(("arbitrary", "arbitrary")),
        name="mixers",
    )(z, gz, bias, cosf, sinf, s0, c0, n0, m0, g_ret, g_ml)


def _out_proj_kernel(a_ref, w_ref, r_ref, o_ref):
    o_ref[...] = r_ref[...] + jnp.dot(a_ref[...], w_ref[...], preferred_element_type=F32)


def _out_proj(a, w, r, tm=512):
    t = a.shape[0]
    return pl.pallas_call(
        _out_proj_kernel,
        grid=(t // tm,),
        in_specs=[
            pl.BlockSpec((tm, D_MODEL), lambda i: (i, 0)),
            pl.BlockSpec((D_MODEL, D_MODEL), lambda i: (0, 0)),
            pl.BlockSpec((tm, D_MODEL), lambda i: (i, 0)),
        ],
        out_specs=pl.BlockSpec((tm, D_MODEL), lambda i: (i, 0)),
        out_shape=jax.ShapeDtypeStruct((t, D_MODEL), F32),
        compiler_params=_params(("arbitrary",)),
        name="out_proj",
    )(a, w, r)


def _top16(x_ref, payload_ref, val_ref, idx_ref):
    x = x_ref[...]
    rows = x.shape[0]
    iota = lax.broadcasted_iota(jnp.int32, x.shape, 0)
    for r in range(PEER_TOPK):
        m = jnp.max(x, axis=0, keepdims=True)
        am = jnp.min(jnp.where(x == m, iota, rows), axis=0, keepdims=True)
        sel = iota == am
        val_ref[r:r + 1, :] = m
        if payload_ref is None:
            idx_ref[r:r + 1, :] = am
        else:
            idx_ref[r:r + 1, :] = jnp.max(jnp.where(sel, payload_ref[...], -1), axis=0, keepdims=True)
        x = jnp.where(sel, -jnp.inf, x)


ROUTE_SCRATCH = [
    pltpu.VMEM((N_KEYS, LANES), F32),
    pltpu.VMEM((PEER_TOPK, LANES), F32),
    pltpu.VMEM((PEER_TOPK, LANES), jnp.int32),
    pltpu.VMEM((PEER_TOPK, LANES), F32),
    pltpu.VMEM((PEER_TOPK, LANES), jnp.int32),
    pltpu.VMEM((CAND_ROWS, LANES), F32),
    pltpu.VMEM((CAND_ROWS, LANES), jnp.int32),
    pltpu.VMEM((PEER_TOPK, LANES), F32),
    pltpu.VMEM((PEER_TOPK, LANES), jnp.int32),
]


def _route_unit(q1, q2, k1_ref, k2_ref, work):
    s_scr, t1_scr, i1_scr, t2_scr, i2_scr, cand_scr, cidx_scr, sc_scr, e_scr = work
    s_scr[...] = _mm_nt(k1_ref[...], q1)
    _top16(s_scr, None, t1_scr, i1_scr)
    s_scr[...] = _mm_nt(k2_ref[...], q2)
    _top16(s_scr, None, t2_scr, i2_scr)
    cand_scr[0:PEER_TOPK, :] = t1_scr[0:1, :] + t2_scr[...]
    cidx_scr[0:PEER_TOPK, :] = i1_scr[0:1, :] * N_KEYS + i2_scr[...]
    t2 = t2_scr[0:SUBLANES, :]
    i2 = i2_scr[0:SUBLANES, :]
    sub_iota = lax.broadcasted_iota(jnp.int32, (SUBLANES, LANES), 0)
    for a in range(1, SUBLANES):
        lo = PEER_TOPK + (a - 1) * SUBLANES
        cand_scr[lo:lo + SUBLANES, :] = jnp.where(sub_iota < PEER_TOPK // (a + 1), t1_scr[a:a + 1, :] + t2, -jnp.inf)
        cidx_scr[lo:lo + SUBLANES, :] = i1_scr[a:a + 1, :] * N_KEYS + i2
    cand_scr[CAND_ROWS - SUBLANES:CAND_ROWS, :] = t1_scr[SUBLANES:PEER_TOPK, :] + t2_scr[0:1, :]
    cidx_scr[CAND_ROWS - SUBLANES:CAND_ROWS, :] = i1_scr[SUBLANES:PEER_TOPK, :] * N_KEYS + i2_scr[0:1, :]
    _top16(cand_scr, cidx_scr, sc_scr, e_scr)
    sc = sc_scr[...]
    p = jnp.exp(sc - sc[0:1, :])
    return e_scr[...], p / jnp.sum(p, axis=0, keepdims=True)


def _qproj_kernel(h_ref, g_ref, wq_ref, q_ref):
    c = _rms(h_ref[...], g_ref[...]).astype(BF16)
    q = jnp.dot(c, wq_ref[...], preferred_element_type=F32).astype(BF16)
    for sub in range(q_ref.shape[0]):
        for j in range(2 * PEER_HEADS):
            q_ref[sub, j] = q[sub * LANES:(sub + 1) * LANES, j * N_KEYS:(j + 1) * N_KEYS]


def _qproj(h, g, wq, tm=512):
    t = h.shape[0]
    subs = tm // LANES
    return pl.pallas_call(
        _qproj_kernel,
        grid=(t // tm,),
        in_specs=[
            pl.BlockSpec((tm, D_MODEL), lambda i: (i, 0)),
            pl.BlockSpec((1, D_MODEL), lambda i: (0, 0)),
            pl.BlockSpec((D_MODEL, D_MODEL), lambda i: (0, 0)),
        ],
        out_specs=pl.BlockSpec((subs, 2 * PEER_HEADS, LANES, N_KEYS), lambda i: (i, 0, 0, 0)),
        out_shape=jax.ShapeDtypeStruct((t // LANES, 2 * PEER_HEADS, LANES, N_KEYS), BF16),
        compiler_params=_params(("arbitrary",)),
        name="qproj",
    )(h, g, wq)


PEER_TB = 512
PEER_SLOTS = 16
PEER_LEAD = PEER_SLOTS - 1
PEER_SUBS = PEER_TB // LANES
ROUTE_UNITS = PEER_HEADS * PEER_SUBS
assert ROUTE_UNITS == PEER_TB // PEER_SLOTS
HALF_D = D_MODEL // 2
PACK_TILES = HALF_D // LANES
INV_SQRT2 = 0.7071067811865476


def _pack_bf16_pairs(a):
    bits = lax.bitcast_convert_type(a.astype(BF16), jnp.uint16).astype(jnp.uint32)
    return bits[:, :HALF_D] | (bits[:, HALF_D:] << 16)


def _peer_kernel(q_this, q_next, k1_ref, k2_ref, h_ref, g_ref, uv_hbm, o_ref,
                 idx_s, idx_v, gate_scr, et_scr, gt_scr, c_scr, *scratch):
    work = scratch[:len(ROUTE_SCRATCH)]
    rows = scratch[len(ROUTE_SCRATCH):len(ROUTE_SCRATCH) + PEER_SLOTS]
    idx_sem, row_sems = scratch[len(ROUTE_SCRATCH) + PEER_SLOTS:]
    tb = h_ref.shape[0]

    def route(q_ref, u):
        if isinstance(u, int):
            sub, hd = u % PEER_SUBS, u // PEER_SUBS
            lo = hd * PEER_TOPK
        else:
            sub = jnp.bitwise_and(u, PEER_SUBS - 1)
            hd = jnp.right_shift(u, PEER_SUBS.bit_length() - 1)
            lo = pl.multiple_of(hd * PEER_TOPK, PEER_TOPK)
        e, gate = _route_unit(q_ref[sub, 2 * hd], q_ref[sub, 2 * hd + 1], k1_ref, k2_ref, work)
        et_scr[sub, pl.ds(lo, PEER_TOPK), :] = e
        gt_scr[sub, pl.ds(lo, PEER_TOPK), :] = gate

    def publish():
        for sub in range(PEER_SUBS):
            idx_v[sub * LANES:(sub + 1) * LANES, :] = et_scr[sub].T
            gate_scr[sub * LANES:(sub + 1) * LANES, :] = gt_scr[sub].T
        copy = pltpu.make_async_copy(idx_v, idx_s, idx_sem)
        copy.start()
        copy.wait()

    @pl.when(pl.program_id(0) == 0)
    def _():
        def body(u, carry):
            route(q_this, u)
            return carry

        lax.fori_loop(0, ROUTE_UNITS, body, 0)
        publish()

    c_scr[...] = _rms(h_ref[...], g_ref[...])

    def issue(t, slot, k0, k1):
        for k in range(k0, k1):
            e = idx_s[t, k]
            pltpu.make_async_copy(uv_hbm.at[e], rows[slot].at[pl.ds(k, 1), :], row_sems.at[slot]).start(priority=k % 2)

    def wait(slot):
        pltpu.make_async_copy(uv_hbm.at[pl.ds(0, PEER_K), 0], rows[slot], row_sems.at[slot]).wait()

    def unpack(words):
        lo = lax.bitcast_convert_type(jnp.left_shift(words, jnp.uint32(16)), F32)
        hi = lax.bitcast_convert_type(jnp.bitwise_and(words, jnp.uint32(0xFFFF0000)), F32)
        return lo, hi

    eye = lax.broadcasted_iota(jnp.int32, (PEER_K, LANES), 0) == lax.broadcasted_iota(jnp.int32, (PEER_K, LANES), 1)
    per_piece = PEER_K // (2 * PACK_TILES)

    def token(t, slot, prefetch):
        wait(slot)
        ahead_slot = (slot + PEER_LEAD) % PEER_SLOTS
        x = c_scr[pl.ds(t, 1), :]
        acc = None
        for j in range(PACK_TILES):
            if prefetch:
                issue(t + PEER_LEAD, ahead_slot, j * per_piece, (j + 1) * per_piece)
            lo, hi = unpack(rows[slot][:, j * LANES:(j + 1) * LANES])
            p = lo * x[:, j * LANES:(j + 1) * LANES] + hi * x[:, HALF_D + j * LANES:HALF_D + (j + 1) * LANES]
            acc = p if acc is None else acc + p
        act = jnp.sum(acc, axis=1, keepdims=True)
        gate_col = jnp.sum(jnp.where(eye, gate_scr[pl.ds(t, 1), :], 0.0), axis=1, keepdims=True)
        coef = gate_col * (0.5 * act * (1.0 + lax.erf(act * INV_SQRT2)))
        mixed_lo, mixed_hi = [], []
        for j in range(PACK_TILES):
            if prefetch:
                issue(t + PEER_LEAD, ahead_slot, (PACK_TILES + j) * per_piece, (PACK_TILES + j + 1) * per_piece)
            lo, hi = unpack(rows[slot][:, HALF_D + j * LANES:HALF_D + (j + 1) * LANES])
            mixed_lo.append(jnp.sum(lo * coef, axis=0, keepdims=True))
            mixed_hi.append(jnp.sum(hi * coef, axis=0, keepdims=True))
        o_ref[pl.ds(t, 1), :] = h_ref[pl.ds(t, 1), :] + jnp.concatenate(mixed_lo + mixed_hi, axis=1)

    for s in range(PEER_LEAD):
        issue(s, s, 0, PEER_K)

    def group(i, carry):
        for s in range(PEER_SLOTS):
            token(i * PEER_SLOTS + s, s, True)
        route(q_next, i)
        return carry

    lax.fori_loop(0, tb // PEER_SLOTS - 1, group, 0)
    last = tb - PEER_SLOTS
    token(last, 0, True)
    for s in range(1, PEER_SLOTS):
        token(last + s, s, False)
    route(q_next, ROUTE_UNITS - 1)
    publish()


def _peer(q, k1, k2, h, g, uv):
    t = h.shape[0]
    tb = PEER_TB
    steps = t // tb
    q_block = (PEER_SUBS, 2 * PEER_HEADS, LANES, N_KEYS)
    return pl.pallas_call(
        _peer_kernel,
        grid=(steps,),
        in_specs=[
            pl.BlockSpec(q_block, lambda i: (i, 0, 0, 0)),
            pl.BlockSpec(q_block, lambda i: (jnp.minimum(i + 1, steps - 1), 0, 0, 0)),
            pl.BlockSpec((N_KEYS, N_KEYS), lambda i: (0, 0)),
            pl.BlockSpec((N_KEYS, N_KEYS), lambda i: (0, 0)),
            pl.BlockSpec((tb, D_MODEL), lambda i: (i, 0)),
            pl.BlockSpec((1, D_MODEL), lambda i: (0, 0)),
            pl.BlockSpec(memory_space=pl.ANY),
        ],
        out_specs=pl.BlockSpec((tb, D_MODEL), lambda i: (i, 0)),
        out_shape=jax.ShapeDtypeStruct((t, D_MODEL), F32),
        scratch_shapes=[
            pltpu.SMEM((tb, PEER_K), jnp.int32),
            pltpu.VMEM((tb, PEER_K), jnp.int32),
            pltpu.VMEM((tb, PEER_K), F32),
            pltpu.VMEM((PEER_SUBS, PEER_K, LANES), jnp.int32),
            pltpu.VMEM((PEER_SUBS, PEER_K, LANES), F32),
            pltpu.VMEM((tb, D_MODEL), F32),
        ] + ROUTE_SCRATCH + [pltpu.VMEM((PEER_K, D_MODEL), jnp.uint32)] * PEER_SLOTS + [
            pltpu.SemaphoreType.DMA(()),
            pltpu.SemaphoreType.DMA((PEER_SLOTS,)),
        ],
        compiler_params=_params(("arbitrary",)),
        name="peer",
    )(q, q, k1, k2, h, g, uv)


def _ple_kernel(h_ref, p_ref, gple_ref, wg_ref, wp_ref, gfin_ref, y_ref):
    h = h_ref[...]
    e = _rms(h, gple_ref[...]).astype(BF16)
    gate = _sigmoid(jnp.dot(e, wg_ref[...], preferred_element_type=F32))
    proj = jnp.dot(p_ref[...].astype(BF16), wp_ref[...], preferred_element_type=F32)
    y_ref[...] = _rms(h + gate * proj, gfin_ref[...])


def _ple(h, p, g_ple, wg, wp, g_fin, tm=256):
    t = h.shape[0]
    return pl.pallas_call(
        _ple_kernel,
        grid=(t // tm,),
        in_specs=[
            pl.BlockSpec((tm, D_MODEL), lambda i: (i, 0)),
            pl.BlockSpec((tm, PLE_DIM), lambda i: (i, 0)),
            pl.BlockSpec((1, D_MODEL), lambda i: (0, 0)),
            pl.BlockSpec((D_MODEL, D_MODEL), lambda i: (0, 0)),
            pl.BlockSpec((PLE_DIM, D_MODEL), lambda i: (0, 0)),
            pl.BlockSpec((1, D_MODEL), lambda i: (0, 0)),
        ],
        out_specs=pl.BlockSpec((tm, D_MODEL), lambda i: (i, 0)),
        out_shape=jax.ShapeDtypeStruct((t, D_MODEL), F32),
        compiler_params=_params(("arbitrary",)),
        name="ple",
    )(h, p, g_ple, wg, wp, g_fin)


def _rope_tables(pos):
    half = RET_D // 2
    inv = ROPE_BASE ** (-jnp.arange(half, dtype=F32) / half)
    ang = pos[:, None] * inv[None, :]
    cos, sin = jnp.cos(ang), jnp.sin(ang)
    return jnp.concatenate([cos, cos], axis=-1), jnp.concatenate([-sin, sin], axis=-1)


def _stream(x, p, pos, s0, c0, n0, m0, w):
    b, l, _ = x.shape
    t = b * l
    x2 = x.reshape(t, D_MODEL)
    z, gz = _in_proj(x2, w["g_mix"], w["w_in"], w["w_gates"])
    cosf, sinf = _rope_tables(pos)
    m0b = jnp.broadcast_to(m0[:, :, None], (b, MLSTM_HEADS, LANES))
    mix, s_new, c_new, n_new, m_new = _mixers(
        z.reshape(b, l, Z_COLS), gz.reshape(b, l, LANES), w["bias"], cosf, sinf, s0, c0, n0, m0b, w["g_ret"], w["g_ml"])
    h1 = _out_proj(mix.reshape(t, D_MODEL), w["w_out"], x2)
    q = _qproj(h1, w["g_ffn"], w["w_q"])
    h2 = _peer(q, w["k1"], w["k2"], h1, w["g_ffn"], w["uv"])
    y = _ple(h2, p.reshape(t, PLE_DIM), w["g_ple"], w["w_ple_gate"], w["w_ple_proj"], w["g_final"])
    return y.reshape(b, l, D_MODEL), s_new[None], c_new[None], n_new[None], m_new[None, :, :, 0]


def _prep_weights(g_mix, w_in, b_gates, g_ret, g_mlstm, w_out, g_ffn, w_peer_q, peer_keys1, peer_keys2, peer_u, peer_v,
                  g_ple, w_ple_gate, w_ple_proj, g_final):
    w_in0 = w_in[0]
    return {
        "g_mix": g_mix,
        "w_in": w_in0[:, :Z_COLS].astype(BF16),
        "w_gates": jnp.pad(w_in0[:, Z_COLS:], ((0, 0), (0, LANES - 2 * MLSTM_HEADS))).astype(BF16),
        "bias": jnp.pad(b_gates, ((0, 0), (0, LANES - 2 * MLSTM_HEADS))),
        "g_ret": g_ret,
        "g_ml": g_mlstm,
        "w_out": w_out[0].astype(BF16),
        "g_ffn": g_ffn,
        "w_q": w_peer_q[0].astype(BF16),
        "k1": peer_keys1[0].astype(BF16),
        "k2": peer_keys2[0].astype(BF16),
        "uv": jnp.concatenate([_pack_bf16_pairs(peer_u[0]), _pack_bf16_pairs(peer_v[0])], axis=1)[:, None, :],
        "g_ple": g_ple,
        "w_ple_gate": w_ple_gate[0].astype(BF16),
        "w_ple_proj": w_ple_proj[0].astype(BF16),
        "g_final": g_final[None, :],
    }


def kernel(x_prompt, x_sample, p_prompt, p_sample, state_ret, state_mlstm_C, state_mlstm_n, state_mlstm_m, g_mix, w_in, b_gates, g_ret, g_mlstm, w_out, g_ffn, w_peer_q, peer_keys1, peer_keys2, peer_u, peer_v, g_ple, w_ple_gate, w_ple_proj, g_final):
    w = _prep_weights(g_mix, w_in, b_gates, g_ret, g_mlstm, w_out, g_ffn, w_peer_q, peer_keys1, peer_keys2, peer_u, peer_v,
                      g_ple, w_ple_gate, w_ple_proj, g_final)
    bp, lp, _ = x_prompt.shape
    bs, ls, _ = x_sample.shape
    zeros = lambda *shape: jnp.zeros(shape, F32)
    past_len = 1024
    y_s, ret_s, c_s, n_s, m_s = _stream(
        x_sample, p_sample[0], past_len + jnp.arange(ls, dtype=F32),
        state_ret[0], state_mlstm_C[0], state_mlstm_n[0], state_mlstm_m[0], w)
    y_p, ret_p, c_p, n_p, m_p = _stream(
        x_prompt, p_prompt[0], jnp.arange(lp, dtype=F32),
        zeros(bp, RET_HEADS, RET_D, RET_D), zeros(bp, MLSTM_HEADS, MLSTM_DK, MLSTM_DV),
        zeros(bp, MLSTM_HEADS, MLSTM_DK), zeros(bp, MLSTM_HEADS), w)
    return (y_p, y_s, ret_p, c_p, n_p, m_p, ret_s, c_s, n_s, m_s)
```

```python
import functools
import math

import numpy as np
import jax
import jax.numpy as jnp
from jax import lax
from jax.experimental import pallas as pl
from jax.experimental.pallas import tpu as pltpu

F32 = jnp.float32
BF16 = jnp.bfloat16

D_MODEL = 2048
CHUNK = 64
RMS_EPS = 1e-6
ROPE_BASE = 10000.0
RET_HEADS = 8
RET_D = 128
MLSTM_HEADS = 4
MLSTM_DK = 128
MLSTM_DV = 256
Z_COLS = 7168
OFF_RQ, OFF_RK, OFF_RV, OFF_RG = 0, 1024, 2048, 3072
OFF_MQ, OFF_MK, OFF_MV, OFF_MO = 4096, 4608, 5120, 6144
PEER_HEADS = 8
N_KEYS = 128
PEER_TOPK = 16
PEER_K = PEER_HEADS * PEER_TOPK
PLE_DIM = 256
LANES = 128
SUBLANES = 8
CAND_ROWS = PEER_TOPK + (SUBLANES - 1) * SUBLANES + SUBLANES

LOG_GAMMA = [float(np.log(np.float32(1.0) - np.float32(2.0) ** np.float32(-5.0 - h))) for h in range(RET_HEADS)]

VMEM_LIMIT = 56 * 1024 * 1024


def _params(sem):
    return pltpu.CompilerParams(dimension_semantics=sem, vmem_limit_bytes=VMEM_LIMIT)


def _rms(x, g):
    return x * lax.rsqrt(jnp.mean(x * x, axis=-1, keepdims=True) + RMS_EPS) * g


def _mm(a, b):
    return jnp.dot(a.astype(BF16), b.astype(BF16), preferred_element_type=F32)


def _mm_nt(a, b):
    return lax.dot_general(a.astype(BF16), b.astype(BF16), (((1,), (1,)), ((), ())), preferred_element_type=F32)


def _sigmoid(x):
    return 1.0 / (1.0 + jnp.exp(-x))


def _in_proj_kernel(x_ref, g_ref, w_ref, wg_ref, z_ref, gz_ref, a_scr):
    @pl.when(pl.program_id(1) == 0)
    def _():
        a_scr[...] = _rms(x_ref[...], g_ref[...]).astype(BF16)
        gz_ref[...] = jnp.dot(a_scr[...], wg_ref[...], preferred_element_type=F32)

    z_ref[...] = jnp.dot(a_scr[...], w_ref[...], preferred_element_type=F32)


def _in_proj(x, g, w, wg, tm=1024, tn=1024):
    t = x.shape[0]
    tm = min(tm, t)
    return pl.pallas_call(
        _in_proj_kernel,
        grid=(t // tm, Z_COLS // tn),
        in_specs=[
            pl.BlockSpec((tm, D_MODEL), lambda i, j: (i, 0)),
            pl.BlockSpec((1, D_MODEL), lambda i, j: (0, 0)),
            pl.BlockSpec((D_MODEL, tn), lambda i, j: (0, j)),
            pl.BlockSpec((D_MODEL, LANES), lambda i, j: (0, 0)),
        ],
        out_specs=[
            pl.BlockSpec((tm, tn), lambda i, j: (i, j)),
            pl.BlockSpec((tm, LANES), lambda i, j: (i, 0)),
        ],
        out_shape=[jax.ShapeDtypeStruct((t, Z_COLS), F32), jax.ShapeDtypeStruct((t, LANES), F32)],
        scratch_shapes=[pltpu.VMEM((tm, D_MODEL), BF16)],
        compiler_params=_params(("arbitrary", "arbitrary")),
        name="in_proj",
    )(x, g, w, wg)


def _log_sigmoid(x):
    return -(jnp.maximum(-x, 0.0) + jnp.log1p(jnp.exp(-jnp.abs(x))))


def _mixers_kernel(z_ref, gz_ref, bias_ref, cos_ref, sin_ref, s0_ref, c0_ref, n0_ref, m0_ref, gret_ref, gml_ref,
                   mix_ref, s_ref, c_ref, n_ref, m_ref):
    @pl.when(pl.program_id(1) == 0)
    def _():
        s_ref[...] = s0_ref[...]
        c_ref[...] = c0_ref[...]
        n_ref[...] = n0_ref[...]
        m_ref[...] = m0_ref[...]

    cl = CHUNK
    row = lax.broadcasted_iota(jnp.int32, (cl, cl), 0)
    col = lax.broadcasted_iota(jnp.int32, (cl, cl), 1)
    causal = row >= col
    diff = jnp.where(causal, (row - col).astype(F32), 0.0)
    posc = lax.broadcasted_iota(jnp.int32, (cl, 1), 0).astype(F32)
    cosf = cos_ref[...]
    sinf = sin_ref[...]

    def rope(x):
        return x * cosf + pltpu.roll(x, RET_D // 2, axis=1) * sinf

    ret = []
    for h in range(RET_HEADS):
        lg = LOG_GAMMA[h]
        lo = h * RET_D
        q = rope(z_ref[0, :, OFF_RQ + lo:OFF_RQ + lo + RET_D])
        k = rope(z_ref[0, :, OFF_RK + lo:OFF_RK + lo + RET_D]) * (RET_D ** -0.5)
        q_dec = (q * jnp.exp((posc + 1.0) * lg)).astype(BF16)
        k_dec_t = (k * jnp.exp((cl - 1.0 - posc) * lg)).T.astype(BF16)
        ret.append((q.astype(BF16), k.astype(BF16), q_dec, k_dec_t))

    gates = gz_ref[0] + bias_ref[...]
    tri = causal.astype(F32)
    bcum = jnp.dot(tri, _log_sigmoid(gates), preferred_element_type=F32, precision=lax.Precision.HIGHEST)
    bcum_t = bcum.T
    gates_t = gates.T
    mls = []
    for h in range(MLSTM_HEADS):
        k = z_ref[0, :, OFF_MK + h * MLSTM_DK:OFF_MK + (h + 1) * MLSTM_DK] * (MLSTM_DK ** -0.5)
        f = MLSTM_HEADS + h
        b_col = bcum[:, f:f + 1]
        b_row = bcum_t[f:f + 1, :]
        ig_col = gates[:, h:h + 1]
        ig_row = gates_t[h:h + 1, :]
        m_prev = m_ref[0, h:h + 1, 0:1]
        dlog = jnp.where(causal, b_col - b_row + ig_row, -jnp.inf)
        inter_log = b_col + m_prev
        m_t = jnp.maximum(inter_log, jnp.max(dlog, axis=-1, keepdims=True))
        dw = jnp.exp(dlog - m_t)
        inter_w = jnp.exp(inter_log - m_t)
        m_new = m_t[cl - 1:cl, :]
        b_last = b_col[cl - 1:cl, :]
        ws = jnp.exp(b_last - b_col + ig_col - m_new)
        carry = jnp.exp(b_last + m_prev - m_new)
        kw = k * ws
        n_old = n_ref[0, h:h + 1, :]
        n_ref[0, h:h + 1, :] = carry * n_old + jnp.sum(kw, axis=0, keepdims=True)
        m_ref[0, h:h + 1, :] = jnp.broadcast_to(m_new, (1, LANES))
        mls.append((k.astype(BF16), kw.T.astype(BF16), dw, inter_w, m_t, carry, n_old))

    for h in range(RET_HEADS):
        lg = LOG_GAMMA[h]
        lo = h * RET_D
        q, k, q_dec, k_dec_t = ret[h]
        v = z_ref[0, :, OFF_RV + lo:OFF_RV + lo + RET_D].astype(BF16)
        rg = z_ref[0, :, OFF_RG + lo:OFF_RG + lo + RET_D]
        decay = jnp.where(causal, jnp.exp(diff * lg), 0.0)
        s_old = s_ref[0, h]
        scores = _mm_nt(q, k) * decay
        o = _mm(scores, v) + _mm(q_dec, s_old)
        s_ref[0, h] = math.exp(cl * lg) * s_old + _mm(k_dec_t, v)
        y = o * lax.rsqrt(jnp.mean(o * o, axis=-1, keepdims=True) + RMS_EPS) * gret_ref[:, lo:lo + RET_D]
        mix_ref[0, :, lo:lo + RET_D] = (y * (rg * _sigmoid(rg))).astype(BF16)

    for h in range(MLSTM_HEADS):
        k, kw_t, dw, inter_w, m_t, carry, n_old = mls[h]
        q = z_ref[0, :, OFF_MQ + h * MLSTM_DK:OFF_MQ + (h + 1) * MLSTM_DK]
        v = z_ref[0, :, OFF_MV + h * MLSTM_DV:OFF_MV + (h + 1) * MLSTM_DV].astype(BF16)
        mo = z_ref[0, :, OFF_MO + h * MLSTM_DV:OFF_MO + (h + 1) * MLSTM_DV]
        c_old = c_ref[0, h]
        sm = _mm_nt(q, k) * dw
        num = _mm(sm, v) + inter_w * _mm(q, c_old)
        den = jnp.sum(sm, axis=-1, keepdims=True) + inter_w * jnp.sum(q * n_old, axis=-1, keepdims=True)
        hh = num / jnp.maximum(jnp.abs(den), jnp.exp(-m_t))
        c_ref[0, h] = carry * c_old + _mm(kw_t, v)
        y = hh * lax.rsqrt(jnp.mean(hh * hh, axis=-1, keepdims=True) + RMS_EPS) * gml_ref[:, h * MLSTM_DV:(h + 1) * MLSTM_DV]
        lo = RET_HEADS * RET_D + h * MLSTM_DV
        mix_ref[0, :, lo:lo + MLSTM_DV] = (y * _sigmoid(mo)).astype(BF16)


def _mixers(z, gz, bias, cosf, sinf, s0, c0, n0, m0, g_ret, g_ml):
    b, l, _ = z.shape
    per_b = lambda *tail: (lambda i, c: (i,) + tail)
    return pl.pallas_call(
        _mixers_kernel,
        grid=(b, l // CHUNK),
        in_specs=[
            pl.BlockSpec((1, CHUNK, Z_COLS), lambda i, c: (i, c, 0)),
            pl.BlockSpec((1, CHUNK, LANES), lambda i, c: (i, c, 0)),
            pl.BlockSpec((1, LANES), lambda i, c: (0, 0)),
            pl.BlockSpec((CHUNK, RET_D), lambda i, c: (c, 0)),
            pl.BlockSpec((CHUNK, RET_D), lambda i, c: (c, 0)),
            pl.BlockSpec((1, RET_HEADS, RET_D, RET_D), per_b(0, 0, 0)),
            pl.BlockSpec((1, MLSTM_HEADS, MLSTM_DK, MLSTM_DV), per_b(0, 0, 0)),
            pl.BlockSpec((1, MLSTM_HEADS, MLSTM_DK), per_b(0, 0)),
            pl.BlockSpec((1, MLSTM_HEADS, LANES), per_b(0, 0)),
            pl.BlockSpec((1, RET_HEADS * RET_D), lambda i, c: (0, 0)),
            pl.BlockSpec((1, MLSTM_HEADS * MLSTM_DV), lambda i, c: (0, 0)),
        ],
        out_specs=[
            pl.BlockSpec((1, CHUNK, D_MODEL), lambda i, c: (i, c, 0)),
            pl.BlockSpec((1, RET_HEADS, RET_D, RET_D), per_b(0, 0, 0)),
            pl.BlockSpec((1, MLSTM_HEADS, MLSTM_DK, MLSTM_DV), per_b(0, 0, 0)),
            pl.BlockSpec((1, MLSTM_HEADS, MLSTM_DK), per_b(0, 0)),
            pl.BlockSpec((1, MLSTM_HEADS, LANES), per_b(0, 0)),
        ],
        out_shape=[
            jax.ShapeDtypeStruct((b, l, D_MODEL), BF16),
            jax.ShapeDtypeStruct((b, RET_HEADS, RET_D, RET_D), F32),
            jax.ShapeDtypeStruct((b, MLSTM_HEADS, MLSTM_DK, MLSTM_DV), F32),
            jax.ShapeDtypeStruct((b, MLSTM_HEADS, MLSTM_DK), F32),
            jax.ShapeDtypeStruct((b, MLSTM_HEADS, LANES), F32),
        ],
        compiler_params=_params(("arbitrary", "arbitrary")),
        name="mixers",
    )(z, gz, bias, cosf, sinf, s0, c0, n0, m0, g_ret, g_ml)


def _out_proj_kernel(a_ref, w_ref, r_ref, o_ref):
    o_ref[...] = r_ref[...] + jnp.dot(a_ref[...], w_ref[...], preferred_element_type=F32)


def _out_proj(a, w, r, tm=512):
    t = a.shape[0]
    return pl.pallas_call(
        _out_proj_kernel,
        grid=(t // tm,),
        in_specs=[
            pl.BlockSpec((tm, D_MODEL), lambda i: (i, 0)),
            pl.BlockSpec((D_MODEL, D_MODEL), lambda i: (0, 0)),
            pl.BlockSpec((tm, D_MODEL), lambda i: (i, 0)),
        ],
        out_specs=pl.BlockSpec((tm, D_MODEL), lambda i: (i, 0)),
        out_shape=jax.ShapeDtypeStruct((t, D_MODEL), F32),
        compiler_params=_params(("arbitrary",)),
        name="out_proj",
    )(a, w, r)


def _top16_steps(x_ref, payload_ref, val_ref, idx_ref):
    rows = x_ref.shape[0]
    for r in range(PEER_TOPK):
        x = x_ref[...]
        iota = lax.broadcasted_iota(jnp.int32, x.shape, 0)
        m = jnp.max(x, axis=0, keepdims=True)
        am = jnp.min(jnp.where(x == m, iota, rows), axis=0, keepdims=True)
        sel = iota == am
        val_ref[r:r + 1, :] = m
        if payload_ref is None:
            idx_ref[r:r + 1, :] = am
        else:
            idx_ref[r:r + 1, :] = jnp.max(jnp.where(sel, payload_ref[...], -1), axis=0, keepdims=True)
        x_ref[...] = jnp.where(sel, -jnp.inf, x)
        yield


ROUTE_SCRATCH = [
    pltpu.VMEM((N_KEYS, LANES), F32),
    pltpu.VMEM((N_KEYS, LANES), F32),
    pltpu.VMEM((PEER_TOPK, LANES), F32),
    pltpu.VMEM((PEER_TOPK, LANES), jnp.int32),
    pltpu.VMEM((PEER_TOPK, LANES), F32),
    pltpu.VMEM((PEER_TOPK, LANES), jnp.int32),
    pltpu.VMEM((CAND_ROWS, LANES), F32),
    pltpu.VMEM((CAND_ROWS, LANES), jnp.int32),
    pltpu.VMEM((PEER_TOPK, LANES), F32),
    pltpu.VMEM((PEER_TOPK, LANES), jnp.int32),
]


def _route_unit_steps(q1, q2, k1_ref, k2_ref, work, emit):
    s1_scr, s2_scr, t1_scr, i1_scr, t2_scr, i2_scr, cand_scr, cidx_scr, sc_scr, e_scr = work
    s1_scr[...] = _mm_nt(k1_ref[...], q1)
    s2_scr[...] = _mm_nt(k2_ref[...], q2)
    yield
    for _ in zip(_top16_steps(s1_scr, None, t1_scr, i1_scr), _top16_steps(s2_scr, None, t2_scr, i2_scr)):
        yield
    cand_scr[0:PEER_TOPK, :] = t1_scr[0:1, :] + t2_scr[...]
    cidx_scr[0:PEER_TOPK, :] = i1_scr[0:1, :] * N_KEYS + i2_scr[...]
    t2 = t2_scr[0:SUBLANES, :]
    i2 = i2_scr[0:SUBLANES, :]
    sub_iota = lax.broadcasted_iota(jnp.int32, (SUBLANES, LANES), 0)
    for a in range(1, SUBLANES):
        lo = PEER_TOPK + (a - 1) * SUBLANES
        cand_scr[lo:lo + SUBLANES, :] = jnp.where(sub_iota < PEER_TOPK // (a + 1), t1_scr[a:a + 1, :] + t2, -jnp.inf)
        cidx_scr[lo:lo + SUBLANES, :] = i1_scr[a:a + 1, :] * N_KEYS + i2
    cand_scr[CAND_ROWS - SUBLANES:CAND_ROWS, :] = t1_scr[SUBLANES:PEER_TOPK, :] + t2_scr[0:1, :]
    cidx_scr[CAND_ROWS - SUBLANES:CAND_ROWS, :] = i1_scr[SUBLANES:PEER_TOPK, :] * N_KEYS + i2_scr[0:1, :]
    yield
    yield from _top16_steps(cand_scr, cidx_scr, sc_scr, e_scr)
    sc = sc_scr[...]
    p = jnp.exp(sc - sc[0:1, :])
    emit(e_scr[...], p / jnp.sum(p, axis=0, keepdims=True))


def _qproj_kernel(h_ref, g_ref, wq_ref, q_ref):
    c = _rms(h_ref[...], g_ref[...]).astype(BF16)
    q = jnp.dot(c, wq_ref[...], preferred_element_type=F32).astype(BF16)
    for sub in range(q_ref.shape[0]):
        for j in range(2 * PEER_HEADS):
            q_ref[sub, j] = q[sub * LANES:(sub + 1) * LANES, j * N_KEYS:(j + 1) * N_KEYS]


def _qproj(h, g, wq, tm=512):
    t = h.shape[0]
    subs = tm // LANES
    return pl.pallas_call(
        _qproj_kernel,
        grid=(t // tm,),
        in_specs=[
            pl.BlockSpec((tm, D_MODEL), lambda i: (i, 0)),
            pl.BlockSpec((1, D_MODEL), lambda i: (0, 0)),
            pl.BlockSpec((D_MODEL, D_MODEL), lambda i: (0, 0)),
        ],
        out_specs=pl.BlockSpec((subs, 2 * PEER_HEADS, LANES, N_KEYS), lambda i: (i, 0, 0, 0)),
        out_shape=jax.ShapeDtypeStruct((t // LANES, 2 * PEER_HEADS, LANES, N_KEYS), BF16),
        compiler_params=_params(("arbitrary",)),
        name="qproj",
    )(h, g, wq)


PEER_TB = 512
PEER_SLOTS = 16
PEER_LEAD = PEER_SLOTS - 1
PEER_SUBS = PEER_TB // LANES
ROUTE_UNITS = PEER_HEADS * PEER_SUBS
assert ROUTE_UNITS == PEER_TB // PEER_SLOTS
SIDE_EVERY = 8
HALF_D = D_MODEL // 2
PACK_TILES = HALF_D // LANES
INV_SQRT2 = 0.7071067811865476


def _pack_bf16_pairs(a):
    bits = lax.bitcast_convert_type(a.astype(BF16), jnp.uint16).astype(jnp.uint32)
    return bits[:, :HALF_D] | (bits[:, HALF_D:] << 16)


def _peer_kernel(q_this, q_next, k1_ref, k2_ref, h_ref, g_ref, uv_hbm, o_ref,
                 idx_s, idx_v, gate_scr, et_scr, gt_scr, c_scr, *scratch):
    work = scratch[:len(ROUTE_SCRATCH)]
    rows = scratch[len(ROUTE_SCRATCH):len(ROUTE_SCRATCH) + PEER_SLOTS]
    idx_sem, row_sems = scratch[len(ROUTE_SCRATCH) + PEER_SLOTS:]
    tb = h_ref.shape[0]

    def route(q_ref, u):
        if isinstance(u, int):
            sub, hd = u % PEER_SUBS, u // PEER_SUBS
            lo = hd * PEER_TOPK
        else:
            sub = jnp.bitwise_and(u, PEER_SUBS - 1)
            hd = jnp.right_shift(u, PEER_SUBS.bit_length() - 1)
            lo = pl.multiple_of(hd * PEER_TOPK, PEER_TOPK)

        def emit(e, gate):
            et_scr[sub, pl.ds(lo, PEER_TOPK), :] = e
            gt_scr[sub, pl.ds(lo, PEER_TOPK), :] = gate

        return _route_unit_steps(q_ref[sub, 2 * hd], q_ref[sub, 2 * hd + 1], k1_ref, k2_ref, work, emit)

    def finish(steps):
        for _ in steps:
            pass

    def publish():
        for sub in range(PEER_SUBS):
            idx_v[sub * LANES:(sub + 1) * LANES, :] = et_scr[sub].T
            gate_scr[sub * LANES:(sub + 1) * LANES, :] = gt_scr[sub].T
        copy = pltpu.make_async_copy(idx_v, idx_s, idx_sem)
        copy.start()
        copy.wait()

    @pl.when(pl.program_id(0) == 0)
    def _():
        def body(u, carry):
            finish(route(q_this, u))
            return carry

        lax.fori_loop(0, ROUTE_UNITS, body, 0)
        publish()

    c_scr[...] = _rms(h_ref[...], g_ref[...])

    def issue(t, slot, k0, k1):
        for k in range(k0, k1):
            e = idx_s[t, k]
            pltpu.make_async_copy(uv_hbm.at[e], rows[slot].at[pl.ds(k, 1), :], row_sems.at[slot]).start(priority=k % 2)

    def wait(slot):
        pltpu.make_async_copy(uv_hbm.at[pl.ds(0, PEER_K), 0], rows[slot], row_sems.at[slot]).wait()

    def unpack(words):
        lo = lax.bitcast_convert_type(jnp.left_shift(words, jnp.uint32(16)), F32)
        hi = lax.bitcast_convert_type(jnp.bitwise_and(words, jnp.uint32(0xFFFF0000)), F32)
        return lo, hi

    eye = lax.broadcasted_iota(jnp.int32, (PEER_K, LANES), 0) == lax.broadcasted_iota(jnp.int32, (PEER_K, LANES), 1)
    per_piece = PEER_K // (2 * PACK_TILES)

    def token(t, slot, prefetch, side_steps):
        wait(slot)
        ahead_slot = (slot + PEER_LEAD) % PEER_SLOTS
        x = c_scr[pl.ds(t, 1), :]
        acc = None
        for j in range(PACK_TILES):
            if prefetch:
                issue(t + PEER_LEAD, ahead_slot, j * per_piece, (j + 1) * per_piece)
            lo, hi = unpack(rows[slot][:, j * LANES:(j + 1) * LANES])
            p = lo * x[:, j * LANES:(j + 1) * LANES] + hi * x[:, HALF_D + j * LANES:HALF_D + (j + 1) * LANES]
            acc = p if acc is None else acc + p
            if j % SIDE_EVERY == SIDE_EVERY - 1:
                next(side_steps, None)
        act = jnp.sum(acc, axis=1, keepdims=True)
        gate_col = jnp.sum(jnp.where(eye, gate_scr[pl.ds(t, 1), :], 0.0), axis=1, keepdims=True)
        coef = gate_col * (0.5 * act * (1.0 + lax.erf(act * INV_SQRT2)))
        mixed_lo, mixed_hi = [], []
        for j in range(PACK_TILES):
            if prefetch:
                issue(t + PEER_LEAD, ahead_slot, (PACK_TILES + j) * per_piece, (PACK_TILES + j + 1) * per_piece)
            lo, hi = unpack(rows[slot][:, HALF_D + j * LANES:HALF_D + (j + 1) * LANES])
            mixed_lo.append(jnp.sum(lo * coef, axis=0, keepdims=True))
            mixed_hi.append(jnp.sum(hi * coef, axis=0, keepdims=True))
            if j % SIDE_EVERY == SIDE_EVERY - 1:
                next(side_steps, None)
        o_ref[pl.ds(t, 1), :] = h_ref[pl.ds(t, 1), :] + jnp.concatenate(mixed_lo + mixed_hi, axis=1)

    for s in range(PEER_LEAD):
        issue(s, s, 0, PEER_K)

    def group(i, carry):
        steps = route(q_next, i)
        for s in range(PEER_SLOTS):
            token(i * PEER_SLOTS + s, s, True, steps)
        finish(steps)
        return carry

    lax.fori_loop(0, tb // PEER_SLOTS - 1, group, 0)
    last = tb - PEER_SLOTS
    steps = route(q_next, ROUTE_UNITS - 1)
    token(last, 0, True, steps)
    for s in range(1, PEER_SLOTS):
        token(last + s, s, False, steps)
    finish(steps)
    publish()


def _peer(q, k1, k2, h, g, uv):
    t = h.shape[0]
    tb = PEER_TB
    steps = t // tb
    q_block = (PEER_SUBS, 2 * PEER_HEADS, LANES, N_KEYS)
    return pl.pallas_call(
        _peer_kernel,
        grid=(steps,),
        in_specs=[
            pl.BlockSpec(q_block, lambda i: (i, 0, 0, 0)),
            pl.BlockSpec(q_block, lambda i: (jnp.minimum(i + 1, steps - 1), 0, 0, 0)),
            pl.BlockSpec((N_KEYS, N_KEYS), lambda i: (0, 0)),
            pl.BlockSpec((N_KEYS, N_KEYS), lambda i: (0, 0)),
            pl.BlockSpec((tb, D_MODEL), lambda i: (i, 0)),
            pl.BlockSpec((1, D_MODEL), lambda i: (0, 0)),
            pl.BlockSpec(memory_space=pl.ANY),
        ],
        out_specs=pl.BlockSpec((tb, D_MODEL), lambda i: (i, 0)),
        out_shape=jax.ShapeDtypeStruct((t, D_MODEL), F32),
        scratch_shapes=[
            pltpu.SMEM((tb, PEER_K), jnp.int32),
            pltpu.VMEM((tb, PEER_K), jnp.int32),
            pltpu.VMEM((tb, PEER_K), F32),
            pltpu.VMEM((PEER_SUBS, PEER_K, LANES), jnp.int32),
            pltpu.VMEM((PEER_SUBS, PEER_K, LANES), F32),
            pltpu.VMEM((tb, D_MODEL), F32),
        ] + ROUTE_SCRATCH + [pltpu.VMEM((PEER_K, D_MODEL), jnp.uint32)] * PEER_SLOTS + [
            pltpu.SemaphoreType.DMA(()),
            pltpu.SemaphoreType.DMA((PEER_SLOTS,)),
        ],
        compiler_params=_params(("arbitrary",)),
        name="peer",
    )(q, q, k1, k2, h, g, uv)


def _ple_kernel(h_ref, p_ref, gple_ref, wg_ref, wp_ref, gfin_ref, y_ref):
    h = h_ref[...]
    e = _rms(h, gple_ref[...]).astype(BF16)
    gate = _sigmoid(jnp.dot(e, wg_ref[...], preferred_element_type=F32))
    proj = jnp.dot(p_ref[...].astype(BF16), wp_ref[...], preferred_element_type=F32)
    y_ref[...] = _rms(h + gate * proj, gfin_ref[...])


def _ple(h, p, g_ple, wg, wp, g_fin, tm=256):
    t = h.shape[0]
    return pl.pallas_call(
        _ple_kernel,
        grid=(t // tm,),
        in_specs=[
            pl.BlockSpec((tm, D_MODEL), lambda i: (i, 0)),
            pl.BlockSpec((tm, PLE_DIM), lambda i: (i, 0)),
            pl.BlockSpec((1, D_MODEL), lambda i: (0, 0)),
            pl.BlockSpec((D_MODEL, D_MODEL), lambda i: (0, 0)),
            pl.BlockSpec((PLE_DIM, D_MODEL), lambda i: (0, 0)),
            pl.BlockSpec((1, D_MODEL), lambda i: (0, 0)),
        ],
        out_specs=pl.BlockSpec((tm, D_MODEL), lambda i: (i, 0)),
        out_shape=jax.ShapeDtypeStruct((t, D_MODEL), F32),
        compiler_params=_params(("arbitrary",)),
        name="ple",
    )(h, p, g_ple, wg, wp, g_fin)


def _rope_tables(pos):
    half = RET_D // 2
    inv = ROPE_BASE ** (-jnp.arange(half, dtype=F32) / half)
    ang = pos[:, None] * inv[None, :]
    cos, sin = jnp.cos(ang), jnp.sin(ang)
    return jnp.concatenate([cos, cos], axis=-1), jnp.concatenate([-sin, sin], axis=-1)


def _stream(x, p, pos, s0, c0, n0, m0, w):
    b, l, _ = x.shape
    t = b * l
    x2 = x.reshape(t, D_MODEL)
    z, gz = _in_proj(x2, w["g_mix"], w["w_in"], w["w_gates"])
    cosf, sinf = _rope_tables(pos)
    m0b = jnp.broadcast_to(m0[:, :, None], (b, MLSTM_HEADS, LANES))
    mix, s_new, c_new, n_new, m_new = _mixers(
        z.reshape(b, l, Z_COLS), gz.reshape(b, l, LANES), w["bias"], cosf, sinf, s0, c0, n0, m0b, w["g_ret"], w["g_ml"])
    h1 = _out_proj(mix.reshape(t, D_MODEL), w["w_out"], x2)
    q = _qproj(h1, w["g_ffn"], w["w_q"])
    h2 = _peer(q, w["k1"], w["k2"], h1, w["g_ffn"], w["uv"])
    y = _ple(h2, p.reshape(t, PLE_DIM), w["g_ple"], w["w_ple_gate"], w["w_ple_proj"], w["g_final"])
    return y.reshape(b, l, D_MODEL), s_new[None], c_new[None], n_new[None], m_new[None, :, :, 0]


def _prep_weights(g_mix, w_in, b_gates, g_ret, g_mlstm, w_out, g_ffn, w_peer_q, peer_keys1, peer_keys2, peer_u, peer_v,
                  g_ple, w_ple_gate, w_ple_proj, g_final):
    w_in0 = w_in[0]
    return {
        "g_mix": g_mix,
        "w_in": w_in0[:, :Z_COLS].astype(BF16),
        "w_gates": jnp.pad(w_in0[:, Z_COLS:], ((0, 0), (0, LANES - 2 * MLSTM_HEADS))).astype(BF16),
        "bias": jnp.pad(b_gates, ((0, 0), (0, LANES - 2 * MLSTM_HEADS))),
        "g_ret": g_ret,
        "g_ml": g_mlstm,
        "w_out": w_out[0].astype(BF16),
        "g_ffn": g_ffn,
        "w_q": w_peer_q[0].astype(BF16),
        "k1": peer_keys1[0].astype(BF16),
        "k2": peer_keys2[0].astype(BF16),
        "uv": jnp.concatenate([_pack_bf16_pairs(peer_u[0]), _pack_bf16_pairs(peer_v[0])], axis=1)[:, None, :],
        "g_ple": g_ple,
        "w_ple_gate": w_ple_gate[0].astype(BF16),
        "w_ple_proj": w_ple_proj[0].astype(BF16),
        "g_final": g_final[None, :],
    }


def kernel(x_prompt, x_sample, p_prompt, p_sample, state_ret, state_mlstm_C, state_mlstm_n, state_mlstm_m, g_mix, w_in, b_gates, g_ret, g_mlstm, w_out, g_ffn, w_peer_q, peer_keys1, peer_keys2, peer_u, peer_v, g_ple, w_ple_gate, w_ple_proj, g_final):
    w = _prep_weights(g_mix, w_in, b_gates, g_ret, g_mlstm, w_out, g_ffn, w_peer_q, peer_keys1, peer_keys2, peer_u, peer_v,
                      g_ple, w_ple_gate, w_ple_proj, g_final)
    bp, lp, _ = x_prompt.shape
    bs, ls, _ = x_sample.shape
    zeros = lambda *shape: jnp.zeros(shape, F32)
    past_len = 1024
    y_s, ret_s, c_s, n_s, m_s = _stream(
        x_sample, p_sample[0], past_len + jnp.arange(ls, dtype=F32),
        state_ret[0], state_mlstm_C[0], state_mlstm_n[0], state_mlstm_m[0], w)
    y_p, ret_p, c_p, n_p, m_p = _stream(
        x_prompt, p_prompt[0], jnp.arange(lp, dtype=F32),
        zeros(bp, RET_HEADS, RET_D, RET_D), zeros(bp, MLSTM_HEADS, MLSTM_DK, MLSTM_DV),
        zeros(bp, MLSTM_HEADS, MLSTM_DK), zeros(bp, MLSTM_HEADS), w)
    return (y_p, y_s, ret_p, c_p, n_p, m_p, ret_s, c_s, n_s, m_s)
```

```python
import functools
import math

import numpy as np
import jax
import jax.numpy as jnp
from jax import lax
from jax.experimental import pallas as pl
from jax.experimental.pallas import tpu as pltpu

F32 = jnp.float32
BF16 = jnp.bfloat16

D_MODEL = 2048
CHUNK = 64
RMS_EPS = 1e-6
ROPE_BASE = 10000.0
RET_HEADS = 8
RET_D = 128
MLSTM_HEADS = 4
MLSTM_DK = 128
MLSTM_DV = 256
Z_COLS = 7168
OFF_RQ, OFF_RK, OFF_RV, OFF_RG = 0, 1024, 2048, 3072
OFF_MQ, OFF_MK, OFF_MV, OFF_MO = 4096, 4608, 5120, 6144
PEER_HEADS = 8
N_KEYS = 128
PEER_TOPK = 16
PEER_K = PEER_HEADS * PEER_TOPK
PLE_DIM = 256
LANES = 128
SUBLANES = 8
CAND_ROWS = PEER_TOPK + (SUBLANES - 1) * SUBLANES + SUBLANES

LOG_GAMMA = [float(np.log(np.float32(1.0) - np.float32(2.0) ** np.float32(-5.0 - h))) for h in range(RET_HEADS)]

VMEM_LIMIT = 56 * 1024 * 1024


def _params(sem):
    return pltpu.CompilerParams(dimension_semantics=sem, vmem_limit_bytes=VMEM_LIMIT)


def _rms(x, g):
    return x * lax.rsqrt(jnp.mean(x * x, axis=-1, keepdims=True) + RMS_EPS) * g


def _mm(a, b):
    return jnp.dot(a.astype(BF16), b.astype(BF16), preferred_element_type=F32)


def _mm_nt(a, b):
    return lax.dot_general(a.astype(BF16), b.astype(BF16), (((1,), (1,)), ((), ())), preferred_element_type=F32)


def _sigmoid(x):
    return 1.0 / (1.0 + jnp.exp(-x))


def _in_proj_kernel(x_ref, g_ref, w_ref, wg_ref, z_ref, gz_ref, a_scr):
    @pl.when(pl.program_id(1) == 0)
    def _():
        a_scr[...] = _rms(x_ref[...], g_ref[...]).astype(BF16)
        gz_ref[...] = jnp.dot(a_scr[...], wg_ref[...], preferred_element_type=F32)

    z_ref[...] = jnp.dot(a_scr[...], w_ref[...], preferred_element_type=F32)


def _in_proj(x, g, w, wg, tm=1024, tn=1024):
    t = x.shape[0]
    tm = min(tm, t)
    return pl.pallas_call(
        _in_proj_kernel,
        grid=(t // tm, Z_COLS // tn),
        in_specs=[
            pl.BlockSpec((tm, D_MODEL), lambda i, j: (i, 0)),
            pl.BlockSpec((1, D_MODEL), lambda i, j: (0, 0)),
            pl.BlockSpec((D_MODEL, tn), lambda i, j: (0, j)),
            pl.BlockSpec((D_MODEL, LANES), lambda i, j: (0, 0)),
        ],
        out_specs=[
            pl.BlockSpec((tm, tn), lambda i, j: (i, j)),
            pl.BlockSpec((tm, LANES), lambda i, j: (i, 0)),
        ],
        out_shape=[jax.ShapeDtypeStruct((t, Z_COLS), F32), jax.ShapeDtypeStruct((t, LANES), F32)],
        scratch_shapes=[pltpu.VMEM((tm, D_MODEL), BF16)],
        compiler_params=_params(("arbitrary", "arbitrary")),
        name="in_proj",
    )(x, g, w, wg)


def _log_sigmoid(x):
    return -(jnp.maximum(-x, 0.0) + jnp.log1p(jnp.exp(-jnp.abs(x))))


def _mixers_kernel(z_ref, gz_ref, bias_ref, cos_ref, sin_ref, s0_ref, c0_ref, n0_ref, m0_ref, gret_ref, gml_ref,
                   mix_ref, s_ref, c_ref, n_ref, m_ref):
    @pl.when(pl.program_id(1) == 0)
    def _():
        s_ref[...] = s0_ref[...]
        c_ref[...] = c0_ref[...]
        n_ref[...] = n0_ref[...]
        m_ref[...] = m0_ref[...]

    cl = CHUNK
    row = lax.broadcasted_iota(jnp.int32, (cl, cl), 0)
    col = lax.broadcasted_iota(jnp.int32, (cl, cl), 1)
    causal = row >= col
    diff = jnp.where(causal, (row - col).astype(F32), 0.0)
    posc = lax.broadcasted_iota(jnp.int32, (cl, 1), 0).astype(F32)
    cosf = cos_ref[...]
    sinf = sin_ref[...]

    def rope(x):
        return x * cosf + pltpu.roll(x, RET_D // 2, axis=1) * sinf

    ret = []
    for h in range(RET_HEADS):
        lg = LOG_GAMMA[h]
        lo = h * RET_D
        q = rope(z_ref[0, :, OFF_RQ + lo:OFF_RQ + lo + RET_D])
        k = rope(z_ref[0, :, OFF_RK + lo:OFF_RK + lo + RET_D]) * (RET_D ** -0.5)
        q_dec = (q * jnp.exp((posc + 1.0) * lg)).astype(BF16)
        k_dec_t = (k * jnp.exp((cl - 1.0 - posc) * lg)).T.astype(BF16)
        ret.append((q.astype(BF16), k.astype(BF16), q_dec, k_dec_t))

    gates = gz_ref[0] + bias_ref[...]
    tri = causal.astype(F32)
    bcum = jnp.dot(tri, _log_sigmoid(gates), preferred_element_type=F32, precision=lax.Precision.HIGHEST)
    bcum_t = bcum.T
    gates_t = gates.T
    mls = []
    for h in range(MLSTM_HEADS):
        k = z_ref[0, :, OFF_MK + h * MLSTM_DK:OFF_MK + (h + 1) * MLSTM_DK] * (MLSTM_DK ** -0.5)
        f = MLSTM_HEADS + h
        b_col = bcum[:, f:f + 1]
        b_row = bcum_t[f:f + 1, :]
        ig_col = gates[:, h:h + 1]
        ig_row = gates_t[h:h + 1, :]
        m_prev = m_ref[0, h:h + 1, 0:1]
        dlog = jnp.where(causal, b_col - b_row + ig_row, -jnp.inf)
        inter_log = b_col + m_prev
        m_t = jnp.maximum(inter_log, jnp.max(dlog, axis=-1, keepdims=True))
        dw = jnp.exp(dlog - m_t)
        inter_w = jnp.exp(inter_log - m_t)
        m_new = m_t[cl - 1:cl, :]
        b_last = b_col[cl - 1:cl, :]
        ws = jnp.exp(b_last - b_col + ig_col - m_new)
        carry = jnp.exp(b_last + m_prev - m_new)
        kw = k * ws
        n_old = n_ref[0, h:h + 1, :]
        n_ref[0, h:h + 1, :] = carry * n_old + jnp.sum(kw, axis=0, keepdims=True)
        m_ref[0, h:h + 1, :] = jnp.broadcast_to(m_new, (1, LANES))
        mls.append((k.astype(BF16), kw.T.astype(BF16), dw, inter_w, m_t, carry, n_old))

    for h in range(RET_HEADS):
        lg = LOG_GAMMA[h]
        lo = h * RET_D
        q, k, q_dec, k_dec_t = ret[h]
        v = z_ref[0, :, OFF_RV + lo:OFF_RV + lo + RET_D].astype(BF16)
        rg = z_ref[0, :, OFF_RG + lo:OFF_RG + lo + RET_D]
        decay = jnp.where(causal, jnp.exp(diff * lg), 0.0)
        s_old = s_ref[0, h]
        scores = _mm_nt(q, k) * decay
        o = _mm(scores, v) + _mm(q_dec, s_old)
        s_ref[0, h] = math.exp(cl * lg) * s_old + _mm(k_dec_t, v)
        y = o * lax.rsqrt(jnp.mean(o * o, axis=-1, keepdims=True) + RMS_EPS) * gret_ref[:, lo:lo + RET_D]
        mix_ref[0, :, lo:lo + RET_D] = (y * (rg * _sigmoid(rg))).astype(BF16)

    for h in range(MLSTM_HEADS):
        k, kw_t, dw, inter_w, m_t, carry, n_old = mls[h]
        q = z_ref[0, :, OFF_MQ + h * MLSTM_DK:OFF_MQ + (h + 1) * MLSTM_DK]
        v = z_ref[0, :, OFF_MV + h * MLSTM_DV:OFF_MV + (h + 1) * MLSTM_DV].astype(BF16)
        mo = z_ref[0, :, OFF_MO + h * MLSTM_DV:OFF_MO + (h + 1) * MLSTM_DV]
        c_old = c_ref[0, h]
        sm = _mm_nt(q, k) * dw
        num = _mm(sm, v) + inter_w * _mm(q, c_old)
        den = jnp.sum(sm, axis=-1, keepdims=True) + inter_w * jnp.sum(q * n_old, axis=-1, keepdims=True)
        hh = num / jnp.maximum(jnp.abs(den), jnp.exp(-m_t))
        c_ref[0, h] = carry * c_old + _mm(kw_t, v)
        y = hh * lax.rsqrt(jnp.mean(hh * hh, axis=-1, keepdims=True) + RMS_EPS) * gml_ref[:, h * MLSTM_DV:(h + 1) * MLSTM_DV]
        lo = RET_HEADS * RET_D + h * MLSTM_DV
        mix_ref[0, :, lo:lo + MLSTM_DV] = (y * _sigmoid(mo)).astype(BF16)


def _mixers(z, gz, bias, cosf, sinf, s0, c0, n0, m0, g_ret, g_ml):
    b, l, _ = z.shape
    per_b = lambda *tail: (lambda i, c: (i,) + tail)
    return pl.pallas_call(
        _mixers_kernel,
        grid=(b, l // CHUNK),
        in_specs=[
            pl.BlockSpec((1, CHUNK, Z_COLS), lambda i, c: (i, c, 0)),
            pl.BlockSpec((1, CHUNK, LANES), lambda i, c: (i, c, 0)),
            pl.BlockSpec((1, LANES), lambda i, c: (0, 0)),
            pl.BlockSpec((CHUNK, RET_D), lambda i, c: (c, 0)),
            pl.BlockSpec((CHUNK, RET_D), lambda i, c: (c, 0)),
            pl.BlockSpec((1, RET_HEADS, RET_D, RET_D), per_b(0, 0, 0)),
            pl.BlockSpec((1, MLSTM_HEADS, MLSTM_DK, MLSTM_DV), per_b(0, 0, 0)),
            pl.BlockSpec((1, MLSTM_HEADS, MLSTM_DK), per_b(0, 0)),
            pl.BlockSpec((1, MLSTM_HEADS, LANES), per_b(0, 0)),
            pl.BlockSpec((1, RET_HEADS * RET_D), lambda i, c: (0, 0)),
            pl.BlockSpec((1, MLSTM_HEADS * MLSTM_DV), lambda i, c: (0, 0)),
        ],
        out_specs=[
            pl.BlockSpec((1, CHUNK, D_MODEL), lambda i, c: (i, c, 0)),
            pl.BlockSpec((1, RET_HEADS, RET_D, RET_D), per_b(0, 0, 0)),
            pl.BlockSpec((1, MLSTM_HEADS, MLSTM_DK, MLSTM_DV), per_b(0, 0, 0)),
            pl.BlockSpec((1, MLSTM_HEADS, MLSTM_DK), per_b(0, 0)),
            pl.BlockSpec((1, MLSTM_HEADS, LANES), per_b(0, 0)),
        ],
        out_shape=[
            jax.ShapeDtypeStruct((b, l, D_MODEL), BF16),
            jax.ShapeDtypeStruct((b, RET_HEADS, RET_D, RET_D), F32),
            jax.ShapeDtypeStruct((b, MLSTM_HEADS, MLSTM_DK, MLSTM_DV), F32),
            jax.ShapeDtypeStruct((b, MLSTM_HEADS, MLSTM_DK), F32),
            jax.ShapeDtypeStruct((b, MLSTM_HEADS, LANES), F32),
        ],
        compiler_params=_params(("arbitrary", "arbitrary")),
        name="mixers",
    )(z, gz, bias, cosf, sinf, s0, c0, n0, m0, g_ret, g_ml)


def _out_proj_kernel(a_ref, w_ref, r_ref, o_ref):
    o_ref[...] = r_ref[...] + jnp.dot(a_ref[...], w_ref[...], preferred_element_type=F32)


def _out_proj(a, w, r, tm=512):
    t = a.shape[0]
    return pl.pallas_call(
        _out_proj_kernel,
        grid=(t // tm,),
        in_specs=[
            pl.BlockSpec((tm, D_MODEL), lambda i: (i, 0)),
            pl.BlockSpec((D_MODEL, D_MODEL), lambda i: (0, 0)),
            pl.BlockSpec((tm, D_MODEL), lambda i: (i, 0)),
        ],
        out_specs=pl.BlockSpec((tm, D_MODEL), lambda i: (i, 0)),
        out_shape=jax.ShapeDtypeStruct((t, D_MODEL), F32),
        compiler_params=_params(("arbitrary",)),
        name="out_proj",
    )(a, w, r)


def _top16_steps(x_ref, payload_ref, val_ref, idx_ref):
    rows = x_ref.shape[0]
    for r in range(PEER_TOPK):
        x = x_ref[...]
        iota = lax.broadcasted_iota(jnp.int32, x.shape, 0)
        m = jnp.max(x, axis=0, keepdims=True)
        am = jnp.min(jnp.where(x == m, iota, rows), axis=0, keepdims=True)
        sel = iota == am
        val_ref[r:r + 1, :] = m
        if payload_ref is None:
            idx_ref[r:r + 1, :] = am
        else:
            idx_ref[r:r + 1, :] = jnp.max(jnp.where(sel, payload_ref[...], -1), axis=0, keepdims=True)
        x_ref[...] = jnp.where(sel, -jnp.inf, x)
        yield


ROUTE_SCRATCH = [
    pltpu.VMEM((N_KEYS, LANES), F32),
    pltpu.VMEM((N_KEYS, LANES), F32),
    pltpu.VMEM((PEER_TOPK, LANES), F32),
    pltpu.VMEM((PEER_TOPK, LANES), jnp.int32),
    pltpu.VMEM((PEER_TOPK, LANES), F32),
    pltpu.VMEM((PEER_TOPK, LANES), jnp.int32),
    pltpu.VMEM((CAND_ROWS, LANES), F32),
    pltpu.VMEM((CAND_ROWS, LANES), jnp.int32),
    pltpu.VMEM((PEER_TOPK, LANES), F32),
    pltpu.VMEM((PEER_TOPK, LANES), jnp.int32),
]


def _route_unit_steps(q1, q2, k1_ref, k2_ref, work, emit):
    s1_scr, s2_scr, t1_scr, i1_scr, t2_scr, i2_scr, cand_scr, cidx_scr, sc_scr, e_scr = work
    s1_scr[...] = _mm_nt(k1_ref[...], q1)
    s2_scr[...] = _mm_nt(k2_ref[...], q2)
    yield
    for _ in zip(_top16_steps(s1_scr, None, t1_scr, i1_scr), _top16_steps(s2_scr, None, t2_scr, i2_scr)):
        yield
    cand_scr[0:PEER_TOPK, :] = t1_scr[0:1, :] + t2_scr[...]
    cidx_scr[0:PEER_TOPK, :] = i1_scr[0:1, :] * N_KEYS + i2_scr[...]
    t2 = t2_scr[0:SUBLANES, :]
    i2 = i2_scr[0:SUBLANES, :]
    sub_iota = lax.broadcasted_iota(jnp.int32, (SUBLANES, LANES), 0)
    for a in range(1, SUBLANES):
        lo = PEER_TOPK + (a - 1) * SUBLANES
        cand_scr[lo:lo + SUBLANES, :] = jnp.where(sub_iota < PEER_TOPK // (a + 1), t1_scr[a:a + 1, :] + t2, -jnp.inf)
        cidx_scr[lo:lo + SUBLANES, :] = i1_scr[a:a + 1, :] * N_KEYS + i2
    cand_scr[CAND_ROWS - SUBLANES:CAND_ROWS, :] = t1_scr[SUBLANES:PEER_TOPK, :] + t2_scr[0:1, :]
    cidx_scr[CAND_ROWS - SUBLANES:CAND_ROWS, :] = i1_scr[SUBLANES:PEER_TOPK, :] * N_KEYS + i2_scr[0:1, :]
    yield
    yield from _top16_steps(cand_scr, cidx_scr, sc_scr, e_scr)
    sc = sc_scr[...]
    p = jnp.exp(sc - sc[0:1, :])
    emit(e_scr[...], p / jnp.sum(p, axis=0, keepdims=True))


def _qproj_kernel(h_ref, g_ref, wq_ref, q_ref):
    c = _rms(h_ref[...], g_ref[...]).astype(BF16)
    q = jnp.dot(c, wq_ref[...], preferred_element_type=F32).astype(BF16)
    for sub in range(q_ref.shape[0]):
        for j in range(2 * PEER_HEADS):
            q_ref[sub, j] = q[sub * LANES:(sub + 1) * LANES, j * N_KEYS:(j + 1) * N_KEYS]


def _qproj(h, g, wq, tm=512):
    t = h.shape[0]
    subs = tm // LANES
    return pl.pallas_call(
        _qproj_kernel,
        grid=(t // tm,),
        in_specs=[
            pl.BlockSpec((tm, D_MODEL), lambda i: (i, 0)),
            pl.BlockSpec((1, D_MODEL), lambda i: (0, 0)),
            pl.BlockSpec((D_MODEL, D_MODEL), lambda i: (0, 0)),
        ],
        out_specs=pl.BlockSpec((subs, 2 * PEER_HEADS, LANES, N_KEYS), lambda i: (i, 0, 0, 0)),
        out_shape=jax.ShapeDtypeStruct((t // LANES, 2 * PEER_HEADS, LANES, N_KEYS), BF16),
        compiler_params=_params(("arbitrary",)),
        name="qproj",
    )(h, g, wq)


PEER_TB = 512
PEER_SLOTS = 8
PEER_LEAD = PEER_SLOTS - 1
PEER_SUBS = PEER_TB // LANES
ROUTE_UNITS = PEER_HEADS * PEER_SUBS
PEER_GROUP = PEER_TB // ROUTE_UNITS
assert PEER_GROUP % PEER_SLOTS == 0
SIDE_EVERY = 8
HALF_D = D_MODEL // 2
PACK_TILES = HALF_D // LANES
INV_SQRT2 = 0.7071067811865476


def _pack_bf16_pairs(a):
    bits = lax.bitcast_convert_type(a.astype(BF16), jnp.uint16).astype(jnp.uint32)
    return bits[:, :HALF_D] | (bits[:, HALF_D:] << 16)


def _peer_kernel(q_this, q_next, k1_ref, k2_ref, h_ref, g_ref, uv_hbm, o_ref,
                 idx_s, idx_v, gate_scr, et_scr, gt_scr, c_scr, *scratch):
    work = scratch[:len(ROUTE_SCRATCH)]
    rows = scratch[len(ROUTE_SCRATCH):len(ROUTE_SCRATCH) + PEER_SLOTS]
    idx_sem, row_sems = scratch[len(ROUTE_SCRATCH) + PEER_SLOTS:]
    tb = h_ref.shape[0]

    def route(q_ref, u):
        if isinstance(u, int):
            sub, hd = u % PEER_SUBS, u // PEER_SUBS
            lo = hd * PEER_TOPK
        else:
            sub = jnp.bitwise_and(u, PEER_SUBS - 1)
            hd = jnp.right_shift(u, PEER_SUBS.bit_length() - 1)
            lo = pl.multiple_of(hd * PEER_TOPK, PEER_TOPK)

        def emit(e, gate):
            et_scr[sub, pl.ds(lo, PEER_TOPK), :] = e
            gt_scr[sub, pl.ds(lo, PEER_TOPK), :] = gate

        return _route_unit_steps(q_ref[sub, 2 * hd], q_ref[sub, 2 * hd + 1], k1_ref, k2_ref, work, emit)

    def finish(steps):
        for _ in steps:
            pass

    def publish():
        for sub in range(PEER_SUBS):
            idx_v[sub * LANES:(sub + 1) * LANES, :] = et_scr[sub].T
            gate_scr[sub * LANES:(sub + 1) * LANES, :] = gt_scr[sub].T
        copy = pltpu.make_async_copy(idx_v, idx_s, idx_sem)
        copy.start()
        copy.wait()

    @pl.when(pl.program_id(0) == 0)
    def _():
        def body(u, carry):
            finish(route(q_this, u))
            return carry

        lax.fori_loop(0, ROUTE_UNITS, body, 0)
        publish()

    c_scr[...] = _rms(h_ref[...], g_ref[...])

    def issue(t, slot, k0, k1):
        for k in range(k0, k1):
            e = idx_s[t, k]
            pltpu.make_async_copy(uv_hbm.at[e], rows[slot].at[pl.ds(k, 1), :], row_sems.at[slot]).start(priority=k % 2)

    def wait(slot):
        pltpu.make_async_copy(uv_hbm.at[pl.ds(0, PEER_K), 0], rows[slot], row_sems.at[slot]).wait()

    def unpack(words):
        lo = lax.bitcast_convert_type(jnp.left_shift(words, jnp.uint32(16)), F32)
        hi = lax.bitcast_convert_type(jnp.bitwise_and(words, jnp.uint32(0xFFFF0000)), F32)
        return lo, hi

    eye = lax.broadcasted_iota(jnp.int32, (PEER_K, LANES), 0) == lax.broadcasted_iota(jnp.int32, (PEER_K, LANES), 1)
    per_piece = PEER_K // (2 * PACK_TILES)

    def token(t, slot, prefetch, side_steps):
        wait(slot)
        ahead_slot = (slot + PEER_LEAD) % PEER_SLOTS
        x = c_scr[pl.ds(t, 1), :]
        acc = None
        for j in range(PACK_TILES):
            if prefetch:
                issue(t + PEER_LEAD, ahead_slot, j * per_piece, (j + 1) * per_piece)
            lo, hi = unpack(rows[slot][:, j * LANES:(j + 1) * LANES])
            p = lo * x[:, j * LANES:(j + 1) * LANES] + hi * x[:, HALF_D + j * LANES:HALF_D + (j + 1) * LANES]
            acc = p if acc is None else acc + p
            if j % SIDE_EVERY == SIDE_EVERY - 1:
                next(side_steps, None)
        act = jnp.sum(acc, axis=1, keepdims=True)
        gate_col = jnp.sum(jnp.where(eye, gate_scr[pl.ds(t, 1), :], 0.0), axis=1, keepdims=True)
        coef = gate_col * (0.5 * act * (1.0 + lax.erf(act * INV_SQRT2)))
        mixed_lo, mixed_hi = [], []
        for j in range(PACK_TILES):
            if prefetch:
                issue(t + PEER_LEAD, ahead_slot, (PACK_TILES + j) * per_piece, (PACK_TILES + j + 1) * per_piece)
            lo, hi = unpack(rows[slot][:, HALF_D + j * LANES:HALF_D + (j + 1) * LANES])
            mixed_lo.append(jnp.sum(lo * coef, axis=0, keepdims=True))
            mixed_hi.append(jnp.sum(hi * coef, axis=0, keepdims=True))
            if j % SIDE_EVERY == SIDE_EVERY - 1:
                next(side_steps, None)
        o_ref[pl.ds(t, 1), :] = h_ref[pl.ds(t, 1), :] + jnp.concatenate(mixed_lo + mixed_hi, axis=1)

    for s in range(PEER_LEAD):
        issue(s, s, 0, PEER_K)

    def group(i, carry):
        steps = route(q_next, i)
        for s in range(PEER_GROUP):
            token(i * PEER_GROUP + s, s % PEER_SLOTS, True, steps)
        finish(steps)
        return carry

    lax.fori_loop(0, ROUTE_UNITS - 1, group, 0)
    last = tb - PEER_GROUP
    steps = route(q_next, ROUTE_UNITS - 1)
    for s in range(PEER_GROUP):
        token(last + s, s % PEER_SLOTS, s + PEER_LEAD < PEER_GROUP, steps)
    finish(steps)
    publish()


def _peer(q, k1, k2, h, g, uv):
    t = h.shape[0]
    tb = PEER_TB
    steps = t // tb
    q_block = (PEER_SUBS, 2 * PEER_HEADS, LANES, N_KEYS)
    return pl.pallas_call(
        _peer_kernel,
        grid=(steps,),
        in_specs=[
            pl.BlockSpec(q_block, lambda i: (i, 0, 0, 0)),
            pl.BlockSpec(q_block, lambda i: (jnp.minimum(i + 1, steps - 1), 0, 0, 0)),
            pl.BlockSpec((N_KEYS, N_KEYS), lambda i: (0, 0)),
            pl.BlockSpec((N_KEYS, N_KEYS), lambda i: (0, 0)),
            pl.BlockSpec((tb, D_MODEL), lambda i: (i, 0)),
            pl.BlockSpec((1, D_MODEL), lambda i: (0, 0)),
            pl.BlockSpec(memory_space=pl.ANY),
        ],
        out_specs=pl.BlockSpec((tb, D_MODEL), lambda i: (i, 0)),
        out_shape=jax.ShapeDtypeStruct((t, D_MODEL), F32),
        scratch_shapes=[
            pltpu.SMEM((tb, PEER_K), jnp.int32),
            pltpu.VMEM((tb, PEER_K), jnp.int32),
            pltpu.VMEM((tb, PEER_K), F32),
            pltpu.VMEM((PEER_SUBS, PEER_K, LANES), jnp.int32),
            pltpu.VMEM((PEER_SUBS, PEER_K, LANES), F32),
            pltpu.VMEM((tb, D_MODEL), F32),
        ] + ROUTE_SCRATCH + [pltpu.VMEM((PEER_K, D_MODEL), jnp.uint32)] * PEER_SLOTS + [
            pltpu.SemaphoreType.DMA(()),
            pltpu.SemaphoreType.DMA((PEER_SLOTS,)),
        ],
        compiler_params=_params(("arbitrary",)),
        name="peer",
    )(q, q, k1, k2, h, g, uv)


def _ple_kernel(h_ref, p_ref, gple_ref, wg_ref, wp_ref, gfin_ref, y_ref):
    h = h_ref[...]
    e = _rms(h, gple_ref[...]).astype(BF16)
    gate = _sigmoid(jnp.dot(e, wg_ref[...], preferred_element_type=F32))
    proj = jnp.dot(p_ref[...].astype(BF16), wp_ref[...], preferred_element_type=F32)
    y_ref[...] = _rms(h + gate * proj, gfin_ref[...])


def _ple(h, p, g_ple, wg, wp, g_fin, tm=256):
    t = h.shape[0]
    return pl.pallas_call(
        _ple_kernel,
        grid=(t // tm,),
        in_specs=[
            pl.BlockSpec((tm, D_MODEL), lambda i: (i, 0)),
            pl.BlockSpec((tm, PLE_DIM), lambda i: (i, 0)),
            pl.BlockSpec((1, D_MODEL), lambda i: (0, 0)),
            pl.BlockSpec((D_MODEL, D_MODEL), lambda i: (0, 0)),
            pl.BlockSpec((PLE_DIM, D_MODEL), lambda i: (0, 0)),
            pl.BlockSpec((1, D_MODEL), lambda i: (0, 0)),
        ],
        out_specs=pl.BlockSpec((tm, D_MODEL), lambda i: (i, 0)),
        out_shape=jax.ShapeDtypeStruct((t, D_MODEL), F32),
        compiler_params=_params(("arbitrary",)),
        name="ple",
    )(h, p, g_ple, wg, wp, g_fin)


def _rope_tables(pos):
    half = RET_D // 2
    inv = ROPE_BASE ** (-jnp.arange(half, dtype=F32) / half)
    ang = pos[:, None] * inv[None, :]
    cos, sin = jnp.cos(ang), jnp.sin(ang)
    return jnp.concatenate([cos, cos], axis=-1), jnp.concatenate([-sin, sin], axis=-1)


def _stream(x, p, pos, s0, c0, n0, m0, w):
    b, l, _ = x.shape
    t = b * l
    x2 = x.reshape(t, D_MODEL)
    z, gz = _in_proj(x2, w["g_mix"], w["w_in"], w["w_gates"])
    cosf, sinf = _rope_tables(pos)
    m0b = jnp.broadcast_to(m0[:, :, None], (b, MLSTM_HEADS, LANES))
    mix, s_new, c_new, n_new, m_new = _mixers(
        z.reshape(b, l, Z_COLS), gz.reshape(b, l, LANES), w["bias"], cosf, sinf, s0, c0, n0, m0b, w["g_ret"], w["g_ml"])
    h1 = _out_proj(mix.reshape(t, D_MODEL), w["w_out"], x2)
    q = _qproj(h1, w["g_ffn"], w["w_q"])
    h2 = _peer(q, w["k1"], w["k2"], h1, w["g_ffn"], w["uv"])
    y = _ple(h2, p.reshape(t, PLE_DIM), w["g_ple"], w["w_ple_gate"], w["w_ple_proj"], w["g_final"])
    return y.reshape(b, l, D_MODEL), s_new[None], c_new[None], n_new[None], m_new[None, :, :, 0]


def _prep_weights(g_mix, w_in, b_gates, g_ret, g_mlstm, w_out, g_ffn, w_peer_q, peer_keys1, peer_keys2, peer_u, peer_v,
                  g_ple, w_ple_gate, w_ple_proj, g_final):
    w_in0 = w_in[0]
    return {
        "g_mix": g_mix,
        "w_in": w_in0[:, :Z_COLS].astype(BF16),
        "w_gates": jnp.pad(w_in0[:, Z_COLS:], ((0, 0), (0, LANES - 2 * MLSTM_HEADS))).astype(BF16),
        "bias": jnp.pad(b_gates, ((0, 0), (0, LANES - 2 * MLSTM_HEADS))),
        "g_ret": g_ret,
        "g_ml": g_mlstm,
        "w_out": w_out[0].astype(BF16),
        "g_ffn": g_ffn,
        "w_q": w_peer_q[0].astype(BF16),
        "k1": peer_keys1[0].astype(BF16),
        "k2": peer_keys2[0].astype(BF16),
        "uv": jnp.concatenate([_pack_bf16_pairs(peer_u[0]), _pack_bf16_pairs(peer_v[0])], axis=1)[:, None, :],
        "g_ple": g_ple,
        "w_ple_gate": w_ple_gate[0].astype(BF16),
        "w_ple_proj": w_ple_proj[0].astype(BF16),
        "g_final": g_final[None, :],
    }


def kernel(x_prompt, x_sample, p_prompt, p_sample, state_ret, state_mlstm_C, state_mlstm_n, state_mlstm_m, g_mix, w_in, b_gates, g_ret, g_mlstm, w_out, g_ffn, w_peer_q, peer_keys1, peer_keys2, peer_u, peer_v, g_ple, w_ple_gate, w_ple_proj, g_final):
    w = _prep_weights(g_mix, w_in, b_gates, g_ret, g_mlstm, w_out, g_ffn, w_peer_q, peer_keys1, peer_keys2, peer_u, peer_v,
                      g_ple, w_ple_gate, w_ple_proj, g_final)
    bp, lp, _ = x_prompt.shape
    bs, ls, _ = x_sample.shape
    zeros = lambda *shape: jnp.zeros(shape, F32)
    past_len = 1024
    y_s, ret_s, c_s, n_s, m_s = _stream(
        x_sample, p_sample[0], past_len + jnp.arange(ls, dtype=F32),
        state_ret[0], state_mlstm_C[0], state_mlstm_n[0], state_mlstm_m[0], w)
    y_p, ret_p, c_p, n_p, m_p = _stream(
        x_prompt, p_prompt[0], jnp.arange(lp, dtype=F32),
        zeros(bp, RET_HEADS, RET_D, RET_D), zeros(bp, MLSTM_HEADS, MLSTM_DK, MLSTM_DV),
        zeros(bp, MLSTM_HEADS, MLSTM_DK), zeros(bp, MLSTM_HEADS), w)
    return (y_p, y_s, ret_p, c_p, n_p, m_p, ret_s, c_s, n_s, m_s)
```

```python
import functools
import math

import numpy as np
import jax
import jax.numpy as jnp
from jax import lax
from jax.experimental import pallas as pl
from jax.experimental.pallas import tpu as pltpu

F32 = jnp.float32
BF16 = jnp.bfloat16

D_MODEL = 2048
CHUNK = 64
RMS_EPS = 1e-6
ROPE_BASE = 10000.0
RET_HEADS = 8
RET_D = 128
MLSTM_HEADS = 4
MLSTM_DK = 128
MLSTM_DV = 256
Z_COLS = 7168
OFF_RQ, OFF_RK, OFF_RV, OFF_RG = 0, 1024, 2048, 3072
OFF_MQ, OFF_MK, OFF_MV, OFF_MO = 4096, 4608, 5120, 6144
PEER_HEADS = 8
N_KEYS = 128
PEER_TOPK = 16
PEER_K = PEER_HEADS * PEER_TOPK
PLE_DIM = 256
LANES = 128
SUBLANES = 8
CAND_ROWS = PEER_TOPK + (SUBLANES - 1) * SUBLANES + SUBLANES

LOG_GAMMA = [float(np.log(np.float32(1.0) - np.float32(2.0) ** np.float32(-5.0 - h))) for h in range(RET_HEADS)]

VMEM_LIMIT = 56 * 1024 * 1024


def _params(sem):
    return pltpu.CompilerParams(dimension_semantics=sem, vmem_limit_bytes=VMEM_LIMIT)


def _rms(x, g):
    return x * lax.rsqrt(jnp.mean(x * x, axis=-1, keepdims=True) + RMS_EPS) * g


def _mm(a, b):
    return jnp.dot(a.astype(BF16), b.astype(BF16), preferred_element_type=F32)


def _mm_nt(a, b):
    return lax.dot_general(a.astype(BF16), b.astype(BF16), (((1,), (1,)), ((), ())), preferred_element_type=F32)


def _sigmoid(x):
    return 1.0 / (1.0 + jnp.exp(-x))


def _in_proj_kernel(x_ref, g_ref, w_ref, wg_ref, z_ref, gz_ref, a_scr):
    @pl.when(pl.program_id(1) == 0)
    def _():
        a_scr[...] = _rms(x_ref[...], g_ref[...]).astype(BF16)
        gz_ref[...] = jnp.dot(a_scr[...], wg_ref[...], preferred_element_type=F32)

    z_ref[...] = jnp.dot(a_scr[...], w_ref[...], preferred_element_type=F32)


def _in_proj(x, g, w, wg, tm=1024, tn=1024):
    t = x.shape[0]
    tm = min(tm, t)
    return pl.pallas_call(
        _in_proj_kernel,
        grid=(t // tm, Z_COLS // tn),
        in_specs=[
            pl.BlockSpec((tm, D_MODEL), lambda i, j: (i, 0)),
            pl.BlockSpec((1, D_MODEL), lambda i, j: (0, 0)),
            pl.BlockSpec((D_MODEL, tn), lambda i, j: (0, j)),
            pl.BlockSpec((D_MODEL, LANES), lambda i, j: (0, 0)),
        ],
        out_specs=[
            pl.BlockSpec((tm, tn), lambda i, j: (i, j)),
            pl.BlockSpec((tm, LANES), lambda i, j: (i, 0)),
        ],
        out_shape=[jax.ShapeDtypeStruct((t, Z_COLS), F32), jax.ShapeDtypeStruct((t, LANES), F32)],
        scratch_shapes=[pltpu.VMEM((tm, D_MODEL), BF16)],
        compiler_params=_params(("arbitrary", "arbitrary")),
        name="in_proj",
    )(x, g, w, wg)


def _log_sigmoid(x):
    return -(jnp.maximum(-x, 0.0) + jnp.log1p(jnp.exp(-jnp.abs(x))))


def _mixers_kernel(z_ref, gz_ref, bias_ref, cos_ref, sin_ref, s0_ref, c0_ref, n0_ref, m0_ref, gret_ref, gml_ref,
                   mix_ref, s_ref, c_ref, n_ref, m_ref):
    @pl.when(pl.program_id(1) == 0)
    def _():
        s_ref[...] = s0_ref[...]
        c_ref[...] = c0_ref[...]
        n_ref[...] = n0_ref[...]
        m_ref[...] = m0_ref[...]

    cl = CHUNK
    row = lax.broadcasted_iota(jnp.int32, (cl, cl), 0)
    col = lax.broadcasted_iota(jnp.int32, (cl, cl), 1)
    causal = row >= col
    diff = jnp.where(causal, (row - col).astype(F32), 0.0)
    posc = lax.broadcasted_iota(jnp.int32, (cl, 1), 0).astype(F32)
    cosf = cos_ref[...]
    sinf = sin_ref[...]

    def rope(x):
        return x * cosf + pltpu.roll(x, RET_D // 2, axis=1) * sinf

    ret = []
    for h in range(RET_HEADS):
        lg = LOG_GAMMA[h]
        lo = h * RET_D
        q = rope(z_ref[0, :, OFF_RQ + lo:OFF_RQ + lo + RET_D])
        k = rope(z_ref[0, :, OFF_RK + lo:OFF_RK + lo + RET_D]) * (RET_D ** -0.5)
        q_dec = (q * jnp.exp((posc + 1.0) * lg)).astype(BF16)
        k_dec_t = (k * jnp.exp((cl - 1.0 - posc) * lg)).T.astype(BF16)
        ret.append((q.astype(BF16), k.astype(BF16), q_dec, k_dec_t))

    gates = gz_ref[0] + bias_ref[...]
    tri = causal.astype(F32)
    bcum = jnp.dot(tri, _log_sigmoid(gates), preferred_element_type=F32, precision=lax.Precision.HIGHEST)
    bcum_t = bcum.T
    gates_t = gates.T
    mls = []
    for h in range(MLSTM_HEADS):
        k = z_ref[0, :, OFF_MK + h * MLSTM_DK:OFF_MK + (h + 1) * MLSTM_DK] * (MLSTM_DK ** -0.5)
        f = MLSTM_HEADS + h
        b_col = bcum[:, f:f + 1]
        b_row = bcum_t[f:f + 1, :]
        ig_col = gates[:, h:h + 1]
        ig_row = gates_t[h:h + 1, :]
        m_prev = m_ref[0, h:h + 1, 0:1]
        dlog = jnp.where(causal, b_col - b_row + ig_row, -jnp.inf)
        inter_log = b_col + m_prev
        m_t = jnp.maximum(inter_log, jnp.max(dlog, axis=-1, keepdims=True))
        dw = jnp.exp(dlog - m_t)
        inter_w = jnp.exp(inter_log - m_t)
        m_new = m_t[cl - 1:cl, :]
        b_last = b_col[cl - 1:cl, :]
        ws = jnp.exp(b_last - b_col + ig_col - m_new)
        carry = jnp.exp(b_last + m_prev - m_new)
        kw = k * ws
        n_old = n_ref[0, h:h + 1, :]
        n_ref[0, h:h + 1, :] = carry * n_old + jnp.sum(kw, axis=0, keepdims=True)
        m_ref[0, h:h + 1, :] = jnp.broadcast_to(m_new, (1, LANES))
        mls.append((k.astype(BF16), kw.T.astype(BF16), dw, inter_w, m_t, carry, n_old))

    for h in range(RET_HEADS):
        lg = LOG_GAMMA[h]
        lo = h * RET_D
        q, k, q_dec, k_dec_t = ret[h]
        v = z_ref[0, :, OFF_RV + lo:OFF_RV + lo + RET_D].astype(BF16)
        rg = z_ref[0, :, OFF_RG + lo:OFF_RG + lo + RET_D]
        decay = jnp.where(causal, jnp.exp(diff * lg), 0.0)
        s_old = s_ref[0, h]
        scores = _mm_nt(q, k) * decay
        o = _mm(scores, v) + _mm(q_dec, s_old)
        s_ref[0, h] = math.exp(cl * lg) * s_old + _mm(k_dec_t, v)
        y = o * lax.rsqrt(jnp.mean(o * o, axis=-1, keepdims=True) + RMS_EPS) * gret_ref[:, lo:lo + RET_D]
        mix_ref[0, :, lo:lo + RET_D] = (y * (rg * _sigmoid(rg))).astype(BF16)

    for h in range(MLSTM_HEADS):
        k, kw_t, dw, inter_w, m_t, carry, n_old = mls[h]
        q = z_ref[0, :, OFF_MQ + h * MLSTM_DK:OFF_MQ + (h + 1) * MLSTM_DK]
        v = z_ref[0, :, OFF_MV + h * MLSTM_DV:OFF_MV + (h + 1) * MLSTM_DV].astype(BF16)
        mo = z_ref[0, :, OFF_MO + h * MLSTM_DV:OFF_MO + (h + 1) * MLSTM_DV]
        c_old = c_ref[0, h]
        sm = _mm_nt(q, k) * dw
        num = _mm(sm, v) + inter_w * _mm(q, c_old)
        den = jnp.sum(sm, axis=-1, keepdims=True) + inter_w * jnp.sum(q * n_old, axis=-1, keepdims=True)
        hh = num / jnp.maximum(jnp.abs(den), jnp.exp(-m_t))
        c_ref[0, h] = carry * c_old + _mm(kw_t, v)
        y = hh * lax.rsqrt(jnp.mean(hh * hh, axis=-1, keepdims=True) + RMS_EPS) * gml_ref[:, h * MLSTM_DV:(h + 1) * MLSTM_DV]
        lo = RET_HEADS * RET_D + h * MLSTM_DV
        mix_ref[0, :, lo:lo + MLSTM_DV] = (y * _sigmoid(mo)).astype(BF16)


def _mixers(z, gz, bias, cosf, sinf, s0, c0, n0, m0, g_ret, g_ml):
    b, l, _ = z.shape
    per_b = lambda *tail: (lambda i, c: (i,) + tail)
    return pl.pallas_call(
        _mixers_kernel,
        grid=(b, l // CHUNK),
        in_specs=[
            pl.BlockSpec((1, CHUNK, Z_COLS), lambda i, c: (i, c, 0)),
            pl.BlockSpec((1, CHUNK, LANES), lambda i, c: (i, c, 0)),
            pl.BlockSpec((1, LANES), lambda i, c: (0, 0)),
            pl.BlockSpec((CHUNK, RET_D), lambda i, c: (c, 0)),
            pl.BlockSpec((CHUNK, RET_D), lambda i, c: (c, 0)),
            pl.BlockSpec((1, RET_HEADS, RET_D, RET_D), per_b(0, 0, 0)),
            pl.BlockSpec((1, MLSTM_HEADS, MLSTM_DK, MLSTM_DV), per_b(0, 0, 0)),
            pl.BlockSpec((1, MLSTM_HEADS, MLSTM_DK), per_b(0, 0)),
            pl.BlockSpec((1, MLSTM_HEADS, LANES), per_b(0, 0)),
            pl.BlockSpec((1, RET_HEADS * RET_D), lambda i, c: (0, 0)),
            pl.BlockSpec((1, MLSTM_HEADS * MLSTM_DV), lambda i, c: (0, 0)),
        ],
        out_specs=[
            pl.BlockSpec((1, CHUNK, D_MODEL), lambda i, c: (i, c, 0)),
            pl.BlockSpec((1, RET_HEADS, RET_D, RET_D), per_b(0, 0, 0)),
            pl.BlockSpec((1, MLSTM_HEADS, MLSTM_DK, MLSTM_DV), per_b(0, 0, 0)),
            pl.BlockSpec((1, MLSTM_HEADS, MLSTM_DK), per_b(0, 0)),
            pl.BlockSpec((1, MLSTM_HEADS, LANES), per_b(0, 0)),
        ],
        out_shape=[
            jax.ShapeDtypeStruct((b, l, D_MODEL), BF16),
            jax.ShapeDtypeStruct((b, RET_HEADS, RET_D, RET_D), F32),
            jax.ShapeDtypeStruct((b, MLSTM_HEADS, MLSTM_DK, MLSTM_DV), F32),
            jax.ShapeDtypeStruct((b, MLSTM_HEADS, MLSTM_DK), F32),
            jax.ShapeDtypeStruct((b, MLSTM_HEADS, LANES), F32),
        ],
        compiler_params=_params(("arbitrary", "arbitrary")),
        name="mixers",
    )(z, gz, bias, cosf, sinf, s0, c0, n0, m0, g_ret, g_ml)


def _out_proj_kernel(a_ref, w_ref, r_ref, o_ref):
    o_ref[...] = r_ref[...] + jnp.dot(a_ref[...], w_ref[...], preferred_element_type=F32)


def _out_proj(a, w, r, tm=512):
    t = a.shape[0]
    return pl.pallas_call(
        _out_proj_kernel,
        grid=(t // tm,),
        in_specs=[
            pl.BlockSpec((tm, D_MODEL), lambda i: (i, 0)),
            pl.BlockSpec((D_MODEL, D_MODEL), lambda i: (0, 0)),
            pl.BlockSpec((tm, D_MODEL), lambda i: (i, 0)),
        ],
        out_specs=pl.BlockSpec((tm, D_MODEL), lambda i: (i, 0)),
        out_shape=jax.ShapeDtypeStruct((t, D_MODEL), F32),
        compiler_params=_params(("arbitrary",)),
        name="out_proj",
    )(a, w, r)


def _top16_steps(x_ref, payload_ref, val_ref, idx_ref):
    rows = x_ref.shape[0]
    for r in range(PEER_TOPK):
        x = x_ref[...]
        iota = lax.broadcasted_iota(jnp.int32, x.shape, 0)
        m = jnp.max(x, axis=0, keepdims=True)
        am = jnp.min(jnp.where(x == m, iota, rows), axis=0, keepdims=True)
        sel = iota == am
        val_ref[r:r + 1, :] = m
        if payload_ref is None:
            idx_ref[r:r + 1, :] = am
        else:
            idx_ref[r:r + 1, :] = jnp.max(jnp.where(sel, payload_ref[...], -1), axis=0, keepdims=True)
        x_ref[...] = jnp.where(sel, -jnp.inf, x)
        yield


ROUTE_SCRATCH = [
    pltpu.VMEM((N_KEYS, LANES), F32),
    pltpu.VMEM((N_KEYS, LANES), F32),
    pltpu.VMEM((PEER_TOPK, LANES), F32),
    pltpu.VMEM((PEER_TOPK, LANES), jnp.int32),
    pltpu.VMEM((PEER_TOPK, LANES), F32),
    pltpu.VMEM((PEER_TOPK, LANES), jnp.int32),
    pltpu.VMEM((CAND_ROWS, LANES), F32),
    pltpu.VMEM((CAND_ROWS, LANES), jnp.int32),
    pltpu.VMEM((PEER_TOPK, LANES), F32),
    pltpu.VMEM((PEER_TOPK, LANES), jnp.int32),
]


def _route_unit_steps(q1, q2, k1_ref, k2_ref, work, emit):
    s1_scr, s2_scr, t1_scr, i1_scr, t2_scr, i2_scr, cand_scr, cidx_scr, sc_scr, e_scr = work
    s1_scr[...] = _mm_nt(k1_ref[...], q1)
    s2_scr[...] = _mm_nt(k2_ref[...], q2)
    yield
    for _ in zip(_top16_steps(s1_scr, None, t1_scr, i1_scr), _top16_steps(s2_scr, None, t2_scr, i2_scr)):
        yield
    cand_scr[0:PEER_TOPK, :] = t1_scr[0:1, :] + t2_scr[...]
    cidx_scr[0:PEER_TOPK, :] = i1_scr[0:1, :] * N_KEYS + i2_scr[...]
    t2 = t2_scr[0:SUBLANES, :]
    i2 = i2_scr[0:SUBLANES, :]
    sub_iota = lax.broadcasted_iota(jnp.int32, (SUBLANES, LANES), 0)
    for a in range(1, SUBLANES):
        lo = PEER_TOPK + (a - 1) * SUBLANES
        cand_scr[lo:lo + SUBLANES, :] = jnp.where(sub_iota < PEER_TOPK // (a + 1), t1_scr[a:a + 1, :] + t2, -jnp.inf)
        cidx_scr[lo:lo + SUBLANES, :] = i1_scr[a:a + 1, :] * N_KEYS + i2
    cand_scr[CAND_ROWS - SUBLANES:CAND_ROWS, :] = t1_scr[SUBLANES:PEER_TOPK, :] + t2_scr[0:1, :]
    cidx_scr[CAND_ROWS - SUBLANES:CAND_ROWS, :] = i1_scr[SUBLANES:PEER_TOPK, :] * N_KEYS + i2_scr[0:1, :]
    yield
    yield from _top16_steps(cand_scr, cidx_scr, sc_scr, e_scr)
    sc = sc_scr[...]
    p = jnp.exp(sc - sc[0:1, :])
    emit(e_scr[...], p / jnp.sum(p, axis=0, keepdims=True))


def _qproj_kernel(h_ref, g_ref, wq_ref, q_ref):
    c = _rms(h_ref[...], g_ref[...]).astype(BF16)
    q = jnp.dot(c, wq_ref[...], preferred_element_type=F32).astype(BF16)
    for sub in range(q_ref.shape[0]):
        for j in range(2 * PEER_HEADS):
            q_ref[sub, j] = q[sub * LANES:(sub + 1) * LANES, j * N_KEYS:(j + 1) * N_KEYS]


def _qproj(h, g, wq, tm=512):
    t = h.shape[0]
    subs = tm // LANES
    return pl.pallas_call(
        _qproj_kernel,
        grid=(t // tm,),
        in_specs=[
            pl.BlockSpec((tm, D_MODEL), lambda i: (i, 0)),
            pl.BlockSpec((1, D_MODEL), lambda i: (0, 0)),
            pl.BlockSpec((D_MODEL, D_MODEL), lambda i: (0, 0)),
        ],
        out_specs=pl.BlockSpec((subs, 2 * PEER_HEADS, LANES, N_KEYS), lambda i: (i, 0, 0, 0)),
        out_shape=jax.ShapeDtypeStruct((t // LANES, 2 * PEER_HEADS, LANES, N_KEYS), BF16),
        compiler_params=_params(("arbitrary",)),
        name="qproj",
    )(h, g, wq)


PEER_TB = 512
PEER_SLOTS = 8
PEER_LEAD = PEER_SLOTS - 1
PEER_SUBS = PEER_TB // LANES
ROUTE_UNITS = PEER_HEADS * PEER_SUBS
PEER_GROUP = PEER_TB // ROUTE_UNITS
assert PEER_GROUP % PEER_SLOTS == 0
SIDE_EVERY = 8
HALF_D = D_MODEL // 2
PACK_TILES = HALF_D // LANES
INV_SQRT2 = 0.7071067811865476


def _pack_tables(u, v):
    def bits(a):
        return lax.bitcast_convert_type(a.astype(BF16), jnp.uint16).astype(jnp.uint32)

    lo = jnp.concatenate([u[:, None, :HALF_D], v[:, None, :HALF_D]], axis=2)
    hi = jnp.concatenate([u[:, None, HALF_D:], v[:, None, HALF_D:]], axis=2)
    return bits(lo) | (bits(hi) << 16)


def _peer_kernel(q_this, q_next, k1_ref, k2_ref, h_ref, g_ref, uv_hbm, o_ref,
                 idx_s, idx_v, gate_scr, et_scr, gt_scr, c_scr, *scratch):
    work = scratch[:len(ROUTE_SCRATCH)]
    rows = scratch[len(ROUTE_SCRATCH):len(ROUTE_SCRATCH) + PEER_SLOTS]
    idx_sem, row_sems = scratch[len(ROUTE_SCRATCH) + PEER_SLOTS:]
    tb = h_ref.shape[0]

    def route(q_ref, u):
        if isinstance(u, int):
            sub, hd = u % PEER_SUBS, u // PEER_SUBS
            lo = hd * PEER_TOPK
        else:
            sub = jnp.bitwise_and(u, PEER_SUBS - 1)
            hd = jnp.right_shift(u, PEER_SUBS.bit_length() - 1)
            lo = pl.multiple_of(hd * PEER_TOPK, PEER_TOPK)

        def emit(e, gate):
            et_scr[sub, pl.ds(lo, PEER_TOPK), :] = e
            gt_scr[sub, pl.ds(lo, PEER_TOPK), :] = gate

        return _route_unit_steps(q_ref[sub, 2 * hd], q_ref[sub, 2 * hd + 1], k1_ref, k2_ref, work, emit)

    def finish(steps):
        for _ in steps:
            pass

    def publish():
        for sub in range(PEER_SUBS):
            idx_v[sub * LANES:(sub + 1) * LANES, :] = et_scr[sub].T
            gate_scr[sub * LANES:(sub + 1) * LANES, :] = gt_scr[sub].T
        copy = pltpu.make_async_copy(idx_v, idx_s, idx_sem)
        copy.start()
        copy.wait()

    @pl.when(pl.program_id(0) == 0)
    def _():
        def body(u, carry):
            finish(route(q_this, u))
            return carry

        lax.fori_loop(0, ROUTE_UNITS, body, 0)
        publish()

    c_scr[...] = _rms(h_ref[...], g_ref[...])

    def issue(t, slot, k0, k1):
        for k in range(k0, k1):
            e = idx_s[t, k]
            pltpu.make_async_copy(uv_hbm.at[e], rows[slot].at[pl.ds(k, 1), :], row_sems.at[slot]).start(priority=k % 2)

    def wait(slot):
        pltpu.make_async_copy(uv_hbm.at[pl.ds(0, PEER_K), 0], rows[slot], row_sems.at[slot]).wait()

    def unpack(words):
        lo = lax.bitcast_convert_type(jnp.left_shift(words, jnp.uint32(16)), F32)
        hi = lax.bitcast_convert_type(jnp.bitwise_and(words, jnp.uint32(0xFFFF0000)), F32)
        return lo, hi

    eye = lax.broadcasted_iota(jnp.int32, (PEER_K, LANES), 0) == lax.broadcasted_iota(jnp.int32, (PEER_K, LANES), 1)
    per_piece = PEER_K // (2 * PACK_TILES)

    def token(t, slot, prefetch, side_steps):
        wait(slot)
        ahead_slot = (slot + PEER_LEAD) % PEER_SLOTS
        x = c_scr[pl.ds(t, 1), :]
        acc = None
        for j in range(PACK_TILES):
            if prefetch:
                issue(t + PEER_LEAD, ahead_slot, j * per_piece, (j + 1) * per_piece)
            lo, hi = unpack(rows[slot][:, j * LANES:(j + 1) * LANES])
            p = lo * x[:, j * LANES:(j + 1) * LANES] + hi * x[:, HALF_D + j * LANES:HALF_D + (j + 1) * LANES]
            acc = p if acc is None else acc + p
            if j % SIDE_EVERY == SIDE_EVERY - 1:
                next(side_steps, None)
        act = jnp.sum(acc, axis=1, keepdims=True)
        gate_col = jnp.sum(jnp.where(eye, gate_scr[pl.ds(t, 1), :], 0.0), axis=1, keepdims=True)
        coef = gate_col * (0.5 * act * (1.0 + lax.erf(act * INV_SQRT2)))
        mixed_lo, mixed_hi = [], []
        for j in range(PACK_TILES):
            if prefetch:
                issue(t + PEER_LEAD, ahead_slot, (PACK_TILES + j) * per_piece, (PACK_TILES + j + 1) * per_piece)
            lo, hi = unpack(rows[slot][:, HALF_D + j * LANES:HALF_D + (j + 1) * LANES])
            mixed_lo.append(jnp.sum(lo * coef, axis=0, keepdims=True))
            mixed_hi.append(jnp.sum(hi * coef, axis=0, keepdims=True))
            if j % SIDE_EVERY == SIDE_EVERY - 1:
                next(side_steps, None)
        o_ref[pl.ds(t, 1), :] = h_ref[pl.ds(t, 1), :] + jnp.concatenate(mixed_lo + mixed_hi, axis=1)

    for s in range(PEER_LEAD):
        issue(s, s, 0, PEER_K)

    def group(i, carry):
        steps = route(q_next, i)
        for s in range(PEER_GROUP):
            token(i * PEER_GROUP + s, s % PEER_SLOTS, True, steps)
        finish(steps)
        return carry

    lax.fori_loop(0, ROUTE_UNITS - 1, group, 0)
    last = tb - PEER_GROUP
    steps = route(q_next, ROUTE_UNITS - 1)
    for s in range(PEER_GROUP):
        token(last + s, s % PEER_SLOTS, s + PEER_LEAD < PEER_GROUP, steps)
    finish(steps)
    publish()


def _peer(q, k1, k2, h, g, uv):
    t = h.shape[0]
    tb = PEER_TB
    steps = t // tb
    q_block = (PEER_SUBS, 2 * PEER_HEADS, LANES, N_KEYS)
    return pl.pallas_call(
        _peer_kernel,
        grid=(steps,),
        in_specs=[
            pl.BlockSpec(q_block, lambda i: (i, 0, 0, 0)),
            pl.BlockSpec(q_block, lambda i: (jnp.minimum(i + 1, steps - 1), 0, 0, 0)),
            pl.BlockSpec((N_KEYS, N_KEYS), lambda i: (0, 0)),
            pl.BlockSpec((N_KEYS, N_KEYS), lambda i: (0, 0)),
            pl.BlockSpec((tb, D_MODEL), lambda i: (i, 0)),
            pl.BlockSpec((1, D_MODEL), lambda i: (0, 0)),
            pl.BlockSpec(memory_space=pl.ANY),
        ],
        out_specs=pl.BlockSpec((tb, D_MODEL), lambda i: (i, 0)),
        out_shape=jax.ShapeDtypeStruct((t, D_MODEL), F32),
        scratch_shapes=[
            pltpu.SMEM((tb, PEER_K), jnp.int32),
            pltpu.VMEM((tb, PEER_K), jnp.int32),
            pltpu.VMEM((tb, PEER_K), F32),
            pltpu.VMEM((PEER_SUBS, PEER_K, LANES), jnp.int32),
            pltpu.VMEM((PEER_SUBS, PEER_K, LANES), F32),
            pltpu.VMEM((tb, D_MODEL), F32),
        ] + ROUTE_SCRATCH + [pltpu.VMEM((PEER_K, D_MODEL), jnp.uint32)] * PEER_SLOTS + [
            pltpu.SemaphoreType.DMA(()),
            pltpu.SemaphoreType.DMA((PEER_SLOTS,)),
        ],
        compiler_params=_params(("arbitrary",)),
        name="peer",
    )(q, q, k1, k2, h, g, uv)


def _ple_kernel(h_ref, p_ref, gple_ref, wg_ref, wp_ref, gfin_ref, y_ref):
    h = h_ref[...]
    e = _rms(h, gple_ref[...]).astype(BF16)
    gate = _sigmoid(jnp.dot(e, wg_ref[...], preferred_element_type=F32))
    proj = jnp.dot(p_ref[...].astype(BF16), wp_ref[...], preferred_element_type=F32)
    y_ref[...] = _rms(h + gate * proj, gfin_ref[...])


def _ple(h, p, g_ple, wg, wp, g_fin, tm=512):
    t = h.shape[0]
    return pl.pallas_call(
        _ple_kernel,
        grid=(t // tm,),
        in_specs=[
            pl.BlockSpec((tm, D_MODEL), lambda i: (i, 0)),
            pl.BlockSpec((tm, PLE_DIM), lambda i: (i, 0)),
            pl.BlockSpec((1, D_MODEL), lambda i: (0, 0)),
            pl.BlockSpec((D_MODEL, D_MODEL), lambda i: (0, 0)),
            pl.BlockSpec((PLE_DIM, D_MODEL), lambda i: (0, 0)),
            pl.BlockSpec((1, D_MODEL), lambda i: (0, 0)),
        ],
        out_specs=pl.BlockSpec((tm, D_MODEL), lambda i: (i, 0)),
        out_shape=jax.ShapeDtypeStruct((t, D_MODEL), F32),
        compiler_params=_params(("arbitrary",)),
        name="ple",
    )(h, p, g_ple, wg, wp, g_fin)


def _rope_tables(pos):
    half = RET_D // 2
    inv = ROPE_BASE ** (-jnp.arange(half, dtype=F32) / half)
    ang = pos[:, None] * inv[None, :]
    cos, sin = jnp.cos(ang), jnp.sin(ang)
    return jnp.concatenate([cos, cos], axis=-1), jnp.concatenate([-sin, sin], axis=-1)


def _stream(x, p, pos, s0, c0, n0, m0, w):
    b, l, _ = x.shape
    t = b * l
    x2 = x.reshape(t, D_MODEL)
    z, gz = _in_proj(x2, w["g_mix"], w["w_in"], w["w_gates"])
    cosf, sinf = _rope_tables(pos)
    m0b = jnp.broadcast_to(m0[:, :, None], (b, MLSTM_HEADS, LANES))
    mix, s_new, c_new, n_new, m_new = _mixers(
        z.reshape(b, l, Z_COLS), gz.reshape(b, l, LANES), w["bias"], cosf, sinf, s0, c0, n0, m0b, w["g_ret"], w["g_ml"])
    h1 = _out_proj(mix.reshape(t, D_MODEL), w["w_out"], x2)
    q = _qproj(h1, w["g_ffn"], w["w_q"])
    h2 = _peer(q, w["k1"], w["k2"], h1, w["g_ffn"], w["uv"])
    y = _ple(h2, p.reshape(t, PLE_DIM), w["g_ple"], w["w_ple_gate"], w["w_ple_proj"], w["g_final"])
    return y.reshape(b, l, D_MODEL), s_new[None], c_new[None], n_new[None], m_new[None, :, :, 0]


def _prep_weights(g_mix, w_in, b_gates, g_ret, g_mlstm, w_out, g_ffn, w_peer_q, peer_keys1, peer_keys2, peer_u, peer_v,
                  g_ple, w_ple_gate, w_ple_proj, g_final):
    w_in0 = w_in[0]
    return {
        "g_mix": g_mix,
        "w_in": w_in0[:, :Z_COLS].astype(BF16),
        "w_gates": jnp.pad(w_in0[:, Z_COLS:], ((0, 0), (0, LANES - 2 * MLSTM_HEADS))).astype(BF16),
        "bias": jnp.pad(b_gates, ((0, 0), (0, LANES - 2 * MLSTM_HEADS))),
        "g_ret": g_ret,
        "g_ml": g_mlstm,
        "w_out": w_out[0].astype(BF16),
        "g_ffn": g_ffn,
        "w_q": w_peer_q[0].astype(BF16),
        "k1": peer_keys1[0].astype(BF16),
        "k2": peer_keys2[0].astype(BF16),
        "uv": _pack_tables(peer_u[0], peer_v[0]),
        "g_ple": g_ple,
        "w_ple_gate": w_ple_gate[0].astype(BF16),
        "w_ple_proj": w_ple_proj[0].astype(BF16),
        "g_final": g_final[None, :],
    }


def kernel(x_prompt, x_sample, p_prompt, p_sample, state_ret, state_mlstm_C, state_mlstm_n, state_mlstm_m, g_mix, w_in, b_gates, g_ret, g_mlstm, w_out, g_ffn, w_peer_q, peer_keys1, peer_keys2, peer_u, peer_v, g_ple, w_ple_gate, w_ple_proj, g_final):
    w = _prep_weights(g_mix, w_in, b_gates, g_ret, g_mlstm, w_out, g_ffn, w_peer_q, peer_keys1, peer_keys2, peer_u, peer_v,
                      g_ple, w_ple_gate, w_ple_proj, g_final)
    bp, lp, _ = x_prompt.shape
    bs, ls, _ = x_sample.shape
    zeros = lambda *shape: jnp.zeros(shape, F32)
    past_len = 1024
    y_s, ret_s, c_s, n_s, m_s = _stream(
        x_sample, p_sample[0], past_len + jnp.arange(ls, dtype=F32),
        state_ret[0], state_mlstm_C[0], state_mlstm_n[0], state_mlstm_m[0], w)
    y_p, ret_p, c_p, n_p, m_p = _stream(
        x_prompt, p_prompt[0], jnp.arange(lp, dtype=F32),
        zeros(bp, RET_HEADS, RET_D, RET_D), zeros(bp, MLSTM_HEADS, MLSTM_DK, MLSTM_DV),
        zeros(bp, MLSTM_HEADS, MLSTM_DK), zeros(bp, MLSTM_HEADS), w)
    return (y_p, y_s, ret_p, c_p, n_p, m_p, ret_s, c_s, n_s, m_s)
```

```python
import math

import numpy as np
import jax
import jax.numpy as jnp
from jax import lax
from jax.experimental import pallas as pl
from jax.experimental.pallas import tpu as pltpu

F32 = jnp.float32
BF16 = jnp.bfloat16

D_MODEL = 2048
CHUNK = 64
PAST_LEN = 1024
RMS_EPS = 1e-6
ROPE_BASE = 10000.0
RET_HEADS = 8
RET_D = 128
MLSTM_HEADS = 4
MLSTM_DK = 128
MLSTM_DV = 256
Z_COLS = 7168
OFF_RQ, OFF_RK, OFF_RV, OFF_RG = 0, 1024, 2048, 3072
OFF_MQ, OFF_MK, OFF_MV, OFF_MO = 4096, 4608, 5120, 6144
PEER_HEADS = 8
N_KEYS = 128
PEER_TOPK = 16
PEER_K = PEER_HEADS * PEER_TOPK
PLE_DIM = 256
LANES = 128
SUBLANES = 8
CAND_ROWS = PEER_TOPK + (SUBLANES - 1) * SUBLANES + SUBLANES

LOG_GAMMA = [float(np.log(np.float32(1.0) - np.float32(2.0) ** np.float32(-5.0 - h))) for h in range(RET_HEADS)]

VMEM_LIMIT = 56 * 1024 * 1024


def _params(sem):
    return pltpu.CompilerParams(dimension_semantics=sem, vmem_limit_bytes=VMEM_LIMIT)


def _rms(x, g):
    return x * lax.rsqrt(jnp.mean(x * x, axis=-1, keepdims=True) + RMS_EPS) * g


def _mm(a, b):
    return jnp.dot(a.astype(BF16), b.astype(BF16), preferred_element_type=F32)


def _mm_nt(a, b):
    return lax.dot_general(a.astype(BF16), b.astype(BF16), (((1,), (1,)), ((), ())), preferred_element_type=F32)


def _sigmoid(x):
    return 1.0 / (1.0 + jnp.exp(-x))


def _in_proj_kernel(x_ref, g_ref, w_ref, wg_ref, z_ref, gz_ref, a_scr):
    @pl.when(pl.program_id(1) == 0)
    def _():
        a_scr[...] = _rms(x_ref[...], g_ref[...]).astype(BF16)
        gz_ref[...] = jnp.dot(a_scr[...], wg_ref[...], preferred_element_type=F32)

    z_ref[...] = jnp.dot(a_scr[...], w_ref[...], preferred_element_type=F32)


def _in_proj(x, g, w, wg, tm=1024, tn=1024):
    t = x.shape[0]
    tm = min(tm, t)
    return pl.pallas_call(
        _in_proj_kernel,
        grid=(t // tm, Z_COLS // tn),
        in_specs=[
            pl.BlockSpec((tm, D_MODEL), lambda i, j: (i, 0)),
            pl.BlockSpec((1, D_MODEL), lambda i, j: (0, 0)),
            pl.BlockSpec((D_MODEL, tn), lambda i, j: (0, j)),
            pl.BlockSpec((D_MODEL, LANES), lambda i, j: (0, 0)),
        ],
        out_specs=[
            pl.BlockSpec((tm, tn), lambda i, j: (i, j)),
            pl.BlockSpec((tm, LANES), lambda i, j: (i, 0)),
        ],
        out_shape=[jax.ShapeDtypeStruct((t, Z_COLS), F32), jax.ShapeDtypeStruct((t, LANES), F32)],
        scratch_shapes=[pltpu.VMEM((tm, D_MODEL), BF16)],
        compiler_params=_params(("arbitrary", "arbitrary")),
        name="in_proj",
    )(x, g, w, wg)


def _log_sigmoid(x):
    return -(jnp.maximum(-x, 0.0) + jnp.log1p(jnp.exp(-jnp.abs(x))))


def _mixers_kernel(z_ref, gz_ref, bias_ref, cos_ref, sin_ref, s0_ref, c0_ref, n0_ref, m0_ref, gret_ref, gml_ref,
                   mix_ref, s_ref, c_ref, n_ref, m_ref):
    @pl.when(pl.program_id(1) == 0)
    def _():
        s_ref[...] = s0_ref[...]
        c_ref[...] = c0_ref[...]
        n_ref[...] = n0_ref[...]
        m_ref[...] = m0_ref[...]

    cl = CHUNK
    row = lax.broadcasted_iota(jnp.int32, (cl, cl), 0)
    col = lax.broadcasted_iota(jnp.int32, (cl, cl), 1)
    causal = row >= col
    diff = jnp.where(causal, (row - col).astype(F32), 0.0)
    posc = lax.broadcasted_iota(jnp.int32, (cl, 1), 0).astype(F32)
    cosf = cos_ref[...]
    sinf = sin_ref[...]

    def rope(x):
        return x * cosf + pltpu.roll(x, RET_D // 2, axis=1) * sinf

    ret = []
    for h in range(RET_HEADS):
        lg = LOG_GAMMA[h]
        lo = h * RET_D
        q = rope(z_ref[0, :, OFF_RQ + lo:OFF_RQ + lo + RET_D])
        k = rope(z_ref[0, :, OFF_RK + lo:OFF_RK + lo + RET_D]) * (RET_D ** -0.5)
        q_dec = (q * jnp.exp((posc + 1.0) * lg)).astype(BF16)
        k_dec_t = (k * jnp.exp((cl - 1.0 - posc) * lg)).T.astype(BF16)
        ret.append((q.astype(BF16), k.astype(BF16), q_dec, k_dec_t))

    gates = gz_ref[0] + bias_ref[...]
    tri = causal.astype(F32)
    bcum = jnp.dot(tri, _log_sigmoid(gates), preferred_element_type=F32, precision=lax.Precision.HIGHEST)
    bcum_t = bcum.T
    gates_t = gates.T
    mls = []
    for h in range(MLSTM_HEADS):
        k = z_ref[0, :, OFF_MK + h * MLSTM_DK:OFF_MK + (h + 1) * MLSTM_DK] * (MLSTM_DK ** -0.5)
        f = MLSTM_HEADS + h
        b_col = bcum[:, f:f + 1]
        b_row = bcum_t[f:f + 1, :]
        ig_col = gates[:, h:h + 1]
        ig_row = gates_t[h:h + 1, :]
        m_prev = m_ref[0, h:h + 1, 0:1]
        dlog = jnp.where(causal, b_col - b_row + ig_row, -jnp.inf)
        inter_log = b_col + m_prev
        m_t = jnp.maximum(inter_log, jnp.max(dlog, axis=-1, keepdims=True))
        dw = jnp.exp(dlog - m_t)
        inter_w = jnp.exp(inter_log - m_t)
        m_new = m_t[cl - 1:cl, :]
        b_last = b_col[cl - 1:cl, :]
        ws = jnp.exp(b_last - b_col + ig_col - m_new)
        carry = jnp.exp(b_last + m_prev - m_new)
        kw = k * ws
        n_old = n_ref[0, h:h + 1, :]
        n_ref[0, h:h + 1, :] = carry * n_old + jnp.sum(kw, axis=0, keepdims=True)
        m_ref[0, h:h + 1, :] = jnp.broadcast_to(m_new, (1, LANES))
        mls.append((k.astype(BF16), kw.T.astype(BF16), dw, inter_w, m_t, carry, n_old))

    for h in range(RET_HEADS):
        lg = LOG_GAMMA[h]
        lo = h * RET_D
        q, k, q_dec, k_dec_t = ret[h]
        v = z_ref[0, :, OFF_RV + lo:OFF_RV + lo + RET_D].astype(BF16)
        rg = z_ref[0, :, OFF_RG + lo:OFF_RG + lo + RET_D]
        decay = jnp.where(causal, jnp.exp(diff * lg), 0.0)
        s_old = s_ref[0, h]
        scores = _mm_nt(q, k) * decay
        o = _mm(scores, v) + _mm(q_dec, s_old)
        s_ref[0, h] = math.exp(cl * lg) * s_old + _mm(k_dec_t, v)
        y = o * lax.rsqrt(jnp.mean(o * o, axis=-1, keepdims=True) + RMS_EPS) * gret_ref[:, lo:lo + RET_D]
        mix_ref[0, :, lo:lo + RET_D] = (y * (rg * _sigmoid(rg))).astype(BF16)

    for h in range(MLSTM_HEADS):
        k, kw_t, dw, inter_w, m_t, carry, n_old = mls[h]
        q = z_ref[0, :, OFF_MQ + h * MLSTM_DK:OFF_MQ + (h + 1) * MLSTM_DK]
        v = z_ref[0, :, OFF_MV + h * MLSTM_DV:OFF_MV + (h + 1) * MLSTM_DV].astype(BF16)
        mo = z_ref[0, :, OFF_MO + h * MLSTM_DV:OFF_MO + (h + 1) * MLSTM_DV]
        c_old = c_ref[0, h]
        sm = _mm_nt(q, k) * dw
        num = _mm(sm, v) + inter_w * _mm(q, c_old)
        den = jnp.sum(sm, axis=-1, keepdims=True) + inter_w * jnp.sum(q * n_old, axis=-1, keepdims=True)
        hh = num / jnp.maximum(jnp.abs(den), jnp.exp(-m_t))
        c_ref[0, h] = carry * c_old + _mm(kw_t, v)
        y = hh * lax.rsqrt(jnp.mean(hh * hh, axis=-1, keepdims=True) + RMS_EPS) * gml_ref[:, h * MLSTM_DV:(h + 1) * MLSTM_DV]
        lo = RET_HEADS * RET_D + h * MLSTM_DV
        mix_ref[0, :, lo:lo + MLSTM_DV] = (y * _sigmoid(mo)).astype(BF16)


def _mixers(z, gz, bias, cosf, sinf, s0, c0, n0, m0, g_ret, g_ml):
    b, l, _ = z.shape
    per_b = lambda *tail: (lambda i, c: (i,) + tail)
    return pl.pallas_call(
        _mixers_kernel,
        grid=(b, l // CHUNK),
        in_specs=[
            pl.BlockSpec((1, CHUNK, Z_COLS), lambda i, c: (i, c, 0)),
            pl.BlockSpec((1, CHUNK, LANES), lambda i, c: (i, c, 0)),
            pl.BlockSpec((1, LANES), lambda i, c: (0, 0)),
            pl.BlockSpec((CHUNK, RET_D), lambda i, c: (c, 0)),
            pl.BlockSpec((CHUNK, RET_D), lambda i, c: (c, 0)),
            pl.BlockSpec((1, RET_HEADS, RET_D, RET_D), per_b(0, 0, 0)),
            pl.BlockSpec((1, MLSTM_HEADS, MLSTM_DK, MLSTM_DV), per_b(0, 0, 0)),
            pl.BlockSpec((1, MLSTM_HEADS, MLSTM_DK), per_b(0, 0)),
            pl.BlockSpec((1, MLSTM_HEADS, LANES), per_b(0, 0)),
            pl.BlockSpec((1, RET_HEADS * RET_D), lambda i, c: (0, 0)),
            pl.BlockSpec((1, MLSTM_HEADS * MLSTM_DV), lambda i, c: (0, 0)),
        ],
        out_specs=[
            pl.BlockSpec((1, CHUNK, D_MODEL), lambda i, c: (i, c, 0)),
            pl.BlockSpec((1, RET_HEADS, RET_D, RET_D), per_b(0, 0, 0)),
            pl.BlockSpec((1, MLSTM_HEADS, MLSTM_DK, MLSTM_DV), per_b(0, 0, 0)),
            pl.BlockSpec((1, MLSTM_HEADS, MLSTM_DK), per_b(0, 0)),
            pl.BlockSpec((1, MLSTM_HEADS, LANES), per_b(0, 0)),
        ],
        out_shape=[
            jax.ShapeDtypeStruct((b, l, D_MODEL), BF16),
            jax.ShapeDtypeStruct((b, RET_HEADS, RET_D, RET_D), F32),
            jax.ShapeDtypeStruct((b, MLSTM_HEADS, MLSTM_DK, MLSTM_DV), F32),
            jax.ShapeDtypeStruct((b, MLSTM_HEADS, MLSTM_DK), F32),
            jax.ShapeDtypeStruct((b, MLSTM_HEADS, LANES), F32),
        ],
        compiler_params=_params(("arbitrary", "arbitrary")),
        name="mixers",
    )(z, gz, bias, cosf, sinf, s0, c0, n0, m0, g_ret, g_ml)


def _out_proj_kernel(a_ref, w_ref, r_ref, o_ref):
    o_ref[...] = r_ref[...] + jnp.dot(a_ref[...], w_ref[...], preferred_element_type=F32)


def _out_proj(a, w, r, tm=512):
    t = a.shape[0]
    return pl.pallas_call(
        _out_proj_kernel,
        grid=(t // tm,),
        in_specs=[
            pl.BlockSpec((tm, D_MODEL), lambda i: (i, 0)),
            pl.BlockSpec((D_MODEL, D_MODEL), lambda i: (0, 0)),
            pl.BlockSpec((tm, D_MODEL), lambda i: (i, 0)),
        ],
        out_specs=pl.BlockSpec((tm, D_MODEL), lambda i: (i, 0)),
        out_shape=jax.ShapeDtypeStruct((t, D_MODEL), F32),
        compiler_params=_params(("arbitrary",)),
        name="out_proj",
    )(a, w, r)


def _top16_steps(x_ref, payload_ref, val_ref, idx_ref):
    rows = x_ref.shape[0]
    for r in range(PEER_TOPK):
        x = x_ref[...]
        iota = lax.broadcasted_iota(jnp.int32, x.shape, 0)
        m = jnp.max(x, axis=0, keepdims=True)
        am = jnp.min(jnp.where(x == m, iota, rows), axis=0, keepdims=True)
        sel = iota == am
        val_ref[r:r + 1, :] = m
        if payload_ref is None:
            idx_ref[r:r + 1, :] = am
        else:
            idx_ref[r:r + 1, :] = jnp.max(jnp.where(sel, payload_ref[...], -1), axis=0, keepdims=True)
        x_ref[...] = jnp.where(sel, -jnp.inf, x)
        yield


ROUTE_SCRATCH = [
    pltpu.VMEM((N_KEYS, LANES), F32),
    pltpu.VMEM((N_KEYS, LANES), F32),
    pltpu.VMEM((PEER_TOPK, LANES), F32),
    pltpu.VMEM((PEER_TOPK, LANES), jnp.int32),
    pltpu.VMEM((PEER_TOPK, LANES), F32),
    pltpu.VMEM((PEER_TOPK, LANES), jnp.int32),
    pltpu.VMEM((CAND_ROWS, LANES), F32),
    pltpu.VMEM((CAND_ROWS, LANES), jnp.int32),
    pltpu.VMEM((PEER_TOPK, LANES), F32),
    pltpu.VMEM((PEER_TOPK, LANES), jnp.int32),
]


def _route_unit_steps(q1, q2, k1_ref, k2_ref, work, emit):
    s1_scr, s2_scr, t1_scr, i1_scr, t2_scr, i2_scr, cand_scr, cidx_scr, sc_scr, e_scr = work
    s1_scr[...] = _mm_nt(k1_ref[...], q1)
    s2_scr[...] = _mm_nt(k2_ref[...], q2)
    yield
    for _ in zip(_top16_steps(s1_scr, None, t1_scr, i1_scr), _top16_steps(s2_scr, None, t2_scr, i2_scr)):
        yield
    cand_scr[0:PEER_TOPK, :] = t1_scr[0:1, :] + t2_scr[...]
    cidx_scr[0:PEER_TOPK, :] = i1_scr[0:1, :] * N_KEYS + i2_scr[...]
    t2 = t2_scr[0:SUBLANES, :]
    i2 = i2_scr[0:SUBLANES, :]
    sub_iota = lax.broadcasted_iota(jnp.int32, (SUBLANES, LANES), 0)
    for a in range(1, SUBLANES):
        lo = PEER_TOPK + (a - 1) * SUBLANES
        cand_scr[lo:lo + SUBLANES, :] = jnp.where(sub_iota < PEER_TOPK // (a + 1), t1_scr[a:a + 1, :] + t2, -jnp.inf)
        cidx_scr[lo:lo + SUBLANES, :] = i1_scr[a:a + 1, :] * N_KEYS + i2
    cand_scr[CAND_ROWS - SUBLANES:CAND_ROWS, :] = t1_scr[SUBLANES:PEER_TOPK, :] + t2_scr[0:1, :]
    cidx_scr[CAND_ROWS - SUBLANES:CAND_ROWS, :] = i1_scr[SUBLANES:PEER_TOPK, :] * N_KEYS + i2_scr[0:1, :]
    yield
    yield from _top16_steps(cand_scr, cidx_scr, sc_scr, e_scr)
    sc = sc_scr[...]
    p = jnp.exp(sc - sc[0:1, :])
    emit(e_scr[...], p / jnp.sum(p, axis=0, keepdims=True))


def _qproj_kernel(h_ref, g_ref, wq_ref, q_ref):
    c = _rms(h_ref[...], g_ref[...]).astype(BF16)
    q = jnp.dot(c, wq_ref[...], preferred_element_type=F32).astype(BF16)
    for sub in range(q_ref.shape[0]):
        for j in range(2 * PEER_HEADS):
            q_ref[sub, j] = q[sub * LANES:(sub + 1) * LANES, j * N_KEYS:(j + 1) * N_KEYS]


def _qproj(h, g, wq, tm=512):
    t = h.shape[0]
    subs = tm // LANES
    return pl.pallas_call(
        _qproj_kernel,
        grid=(t // tm,),
        in_specs=[
            pl.BlockSpec((tm, D_MODEL), lambda i: (i, 0)),
            pl.BlockSpec((1, D_MODEL), lambda i: (0, 0)),
            pl.BlockSpec((D_MODEL, D_MODEL), lambda i: (0, 0)),
        ],
        out_specs=pl.BlockSpec((subs, 2 * PEER_HEADS, LANES, N_KEYS), lambda i: (i, 0, 0, 0)),
        out_shape=jax.ShapeDtypeStruct((t // LANES, 2 * PEER_HEADS, LANES, N_KEYS), BF16),
        compiler_params=_params(("arbitrary",)),
        name="qproj",
    )(h, g, wq)


PEER_TB = 512
PEER_SLOTS = 8
PEER_LEAD = PEER_SLOTS - 1
PEER_SUBS = PEER_TB // LANES
ROUTE_UNITS = PEER_HEADS * PEER_SUBS
PEER_GROUP = PEER_TB // ROUTE_UNITS
assert PEER_GROUP % PEER_SLOTS == 0
ROUTE_STEPS = 2 * PEER_TOPK + 3
SIDE_AFTER = (4, 10, 15)
assert len(SIDE_AFTER) * PEER_GROUP >= ROUTE_STEPS
HALF_D = D_MODEL // 2
PACK_TILES = HALF_D // LANES
INV_SQRT2 = 0.7071067811865476


def _pack_bf16_pairs(a):
    bits = lax.bitcast_convert_type(a.astype(BF16), jnp.uint16).astype(jnp.uint32)
    return bits[:, :HALF_D] | (bits[:, HALF_D:] << 16)


def _peer_kernel(q_this, q_next, k1_ref, k2_ref, h_ref, g_ref, uv_hbm, o_ref,
                 idx_s, idx_v, gate_scr, et_scr, gt_scr, c_scr, *scratch):
    work = scratch[:len(ROUTE_SCRATCH)]
    rows = scratch[len(ROUTE_SCRATCH):len(ROUTE_SCRATCH) + PEER_SLOTS]
    idx_sem, row_sems = scratch[len(ROUTE_SCRATCH) + PEER_SLOTS:]
    tb = h_ref.shape[0]

    def route(q_ref, u):
        if isinstance(u, int):
            sub, hd = u % PEER_SUBS, u // PEER_SUBS
            lo = hd * PEER_TOPK
        else:
            sub = jnp.bitwise_and(u, PEER_SUBS - 1)
            hd = jnp.right_shift(u, PEER_SUBS.bit_length() - 1)
            lo = pl.multiple_of(hd * PEER_TOPK, PEER_TOPK)

        def emit(e, gate):
            et_scr[sub, pl.ds(lo, PEER_TOPK), :] = e
            gt_scr[sub, pl.ds(lo, PEER_TOPK), :] = gate

        return _route_unit_steps(q_ref[sub, 2 * hd], q_ref[sub, 2 * hd + 1], k1_ref, k2_ref, work, emit)

    def finish(steps):
        for _ in steps:
            pass

    def publish():
        for sub in range(PEER_SUBS):
            idx_v[sub * LANES:(sub + 1) * LANES, :] = et_scr[sub].T
            gate_scr[sub * LANES:(sub + 1) * LANES, :] = gt_scr[sub].T
        copy = pltpu.make_async_copy(idx_v, idx_s, idx_sem)
        copy.start()
        copy.wait()

    @pl.when(pl.program_id(0) == 0)
    def _():
        def body(u, carry):
            finish(route(q_this, u))
            return carry

        lax.fori_loop(0, ROUTE_UNITS, body, 0)
        publish()

    c_scr[...] = _rms(h_ref[...], g_ref[...])

    def issue(t, slot, k0, k1):
        for k in range(k0, k1):
            e = idx_s[t, k]
            pltpu.make_async_copy(uv_hbm.at[e], rows[slot].at[pl.ds(k, 1), :], row_sems.at[slot]).start(priority=k % 2)

    def wait(slot):
        pltpu.make_async_copy(uv_hbm.at[pl.ds(0, PEER_K), 0], rows[slot], row_sems.at[slot]).wait()

    def unpack(words):
        lo = lax.bitcast_convert_type(jnp.left_shift(words, jnp.uint32(16)), F32)
        hi = lax.bitcast_convert_type(jnp.bitwise_and(words, jnp.uint32(0xFFFF0000)), F32)
        return lo, hi

    eye = lax.broadcasted_iota(jnp.int32, (PEER_K, LANES), 0) == lax.broadcasted_iota(jnp.int32, (PEER_K, LANES), 1)
    per_piece = PEER_K // (2 * PACK_TILES)

    def token(t, slot, prefetch, side_steps):
        wait(slot)
        ahead_slot = (slot + PEER_LEAD) % PEER_SLOTS
        x = c_scr[pl.ds(t, 1), :]
        acc = None
        for j in range(PACK_TILES):
            if prefetch:
                issue(t + PEER_LEAD, ahead_slot, j * per_piece, (j + 1) * per_piece)
            lo, hi = unpack(rows[slot][:, j * LANES:(j + 1) * LANES])
            p = lo * x[:, j * LANES:(j + 1) * LANES] + hi * x[:, HALF_D + j * LANES:HALF_D + (j + 1) * LANES]
            acc = p if acc is None else acc + p
            if j in SIDE_AFTER:
                next(side_steps, None)
        act = jnp.sum(acc, axis=1, keepdims=True)
        gate_col = jnp.sum(jnp.where(eye, gate_scr[pl.ds(t, 1), :], 0.0), axis=1, keepdims=True)
        coef = gate_col * (0.5 * act * (1.0 + lax.erf(act * INV_SQRT2)))
        mixed_lo, mixed_hi = [], []
        for j in range(PACK_TILES):
            if prefetch:
                issue(t + PEER_LEAD, ahead_slot, (PACK_TILES + j) * per_piece, (PACK_TILES + j + 1) * per_piece)
            lo, hi = unpack(rows[slot][:, HALF_D + j * LANES:HALF_D + (j + 1) * LANES])
            mixed_lo.append(jnp.sum(lo * coef, axis=0, keepdims=True))
            mixed_hi.append(jnp.sum(hi * coef, axis=0, keepdims=True))
            if PACK_TILES + j in SIDE_AFTER:
                next(side_steps, None)
        o_ref[pl.ds(t, 1), :] = h_ref[pl.ds(t, 1), :] + jnp.concatenate(mixed_lo + mixed_hi, axis=1)

    for s in range(PEER_LEAD):
        issue(s, s, 0, PEER_K)

    def group(i, carry):
        steps = route(q_next, i)
        for s in range(PEER_GROUP):
            token(i * PEER_GROUP + s, s % PEER_SLOTS, True, steps)
        finish(steps)
        return carry

    lax.fori_loop(0, ROUTE_UNITS - 1, group, 0)
    last = tb - PEER_GROUP
    steps = route(q_next, ROUTE_UNITS - 1)
    for s in range(PEER_GROUP):
        token(last + s, s % PEER_SLOTS, s + PEER_LEAD < PEER_GROUP, steps)
    finish(steps)
    publish()


def _peer(q, k1, k2, h, g, uv):
    t = h.shape[0]
    tb = PEER_TB
    steps = t // tb
    q_block = (PEER_SUBS, 2 * PEER_HEADS, LANES, N_KEYS)
    return pl.pallas_call(
        _peer_kernel,
        grid=(steps,),
        in_specs=[
            pl.BlockSpec(q_block, lambda i: (i, 0, 0, 0)),
            pl.BlockSpec(q_block, lambda i: (jnp.minimum(i + 1, steps - 1), 0, 0, 0)),
            pl.BlockSpec((N_KEYS, N_KEYS), lambda i: (0, 0)),
            pl.BlockSpec((N_KEYS, N_KEYS), lambda i: (0, 0)),
            pl.BlockSpec((tb, D_MODEL), lambda i: (i, 0)),
            pl.BlockSpec((1, D_MODEL), lambda i: (0, 0)),
            pl.BlockSpec(memory_space=pl.ANY),
        ],
        out_specs=pl.BlockSpec((tb, D_MODEL), lambda i: (i, 0)),
        out_shape=jax.ShapeDtypeStruct((t, D_MODEL), F32),
        scratch_shapes=[
            pltpu.SMEM((tb, PEER_K), jnp.int32),
            pltpu.VMEM((tb, PEER_K), jnp.int32),
            pltpu.VMEM((tb, PEER_K), F32),
            pltpu.VMEM((PEER_SUBS, PEER_K, LANES), jnp.int32),
            pltpu.VMEM((PEER_SUBS, PEER_K, LANES), F32),
            pltpu.VMEM((tb, D_MODEL), F32),
        ] + ROUTE_SCRATCH + [pltpu.VMEM((PEER_K, D_MODEL), jnp.uint32)] * PEER_SLOTS + [
            pltpu.SemaphoreType.DMA(()),
            pltpu.SemaphoreType.DMA((PEER_SLOTS,)),
        ],
        compiler_params=_params(("arbitrary",)),
        name="peer",
    )(q, q, k1, k2, h, g, uv)


def _ple_kernel(h_ref, p_ref, gple_ref, wg_ref, wp_ref, gfin_ref, y_ref):
    h = h_ref[...]
    e = _rms(h, gple_ref[...]).astype(BF16)
    gate = _sigmoid(jnp.dot(e, wg_ref[...], preferred_element_type=F32))
    proj = jnp.dot(p_ref[...].astype(BF16), wp_ref[...], preferred_element_type=F32)
    y_ref[...] = _rms(h + gate * proj, gfin_ref[...])


def _ple(h, p, g_ple, wg, wp, g_fin, tm=512):
    t = h.shape[0]
    return pl.pallas_call(
        _ple_kernel,
        grid=(t // tm,),
        in_specs=[
            pl.BlockSpec((tm, D_MODEL), lambda i: (i, 0)),
            pl.BlockSpec((tm, PLE_DIM), lambda i: (i, 0)),
            pl.BlockSpec((1, D_MODEL), lambda i: (0, 0)),
            pl.BlockSpec((D_MODEL, D_MODEL), lambda i: (0, 0)),
            pl.BlockSpec((PLE_DIM, D_MODEL), lambda i: (0, 0)),
            pl.BlockSpec((1, D_MODEL), lambda i: (0, 0)),
        ],
        out_specs=pl.BlockSpec((tm, D_MODEL), lambda i: (i, 0)),
        out_shape=jax.ShapeDtypeStruct((t, D_MODEL), F32),
        compiler_params=_params(("arbitrary",)),
        name="ple",
    )(h, p, g_ple, wg, wp, g_fin)


def _rope_tables(pos):
    half = RET_D // 2
    inv = ROPE_BASE ** (-jnp.arange(half, dtype=F32) / half)
    ang = pos[:, None] * inv[None, :]
    cos, sin = jnp.cos(ang), jnp.sin(ang)
    return jnp.concatenate([cos, cos], axis=-1), jnp.concatenate([-sin, sin], axis=-1)


def _stream(x, p, pos, s0, c0, n0, m0, w):
    b, l, _ = x.shape
    t = b * l
    x2 = x.reshape(t, D_MODEL)
    z, gz = _in_proj(x2, w["g_mix"], w["w_in"], w["w_gates"])
    cosf, sinf = _rope_tables(pos)
    m0b = jnp.broadcast_to(m0[:, :, None], (b, MLSTM_HEADS, LANES))
    mix, s_new, c_new, n_new, m_new = _mixers(
        z.reshape(b, l, Z_COLS), gz.reshape(b, l, LANES), w["bias"], cosf, sinf, s0, c0, n0, m0b, w["g_ret"], w["g_ml"])
    h1 = _out_proj(mix.reshape(t, D_MODEL), w["w_out"], x2)
    q = _qproj(h1, w["g_ffn"], w["w_q"])
    h2 = _peer(q, w["k1"], w["k2"], h1, w["g_ffn"], w["uv"])
    y = _ple(h2, p.reshape(t, PLE_DIM), w["g_ple"], w["w_ple_gate"], w["w_ple_proj"], w["g_final"])
    return y.reshape(b, l, D_MODEL), s_new[None], c_new[None], n_new[None], m_new[None, :, :, 0]


def _prep_weights(g_mix, w_in, b_gates, g_ret, g_mlstm, w_out, g_ffn, w_peer_q, peer_keys1, peer_keys2, peer_u, peer_v,
                  g_ple, w_ple_gate, w_ple_proj, g_final):
    w_in0 = w_in[0]
    return {
        "g_mix": g_mix,
        "w_in": w_in0[:, :Z_COLS].astype(BF16),
        "w_gates": jnp.pad(w_in0[:, Z_COLS:], ((0, 0), (0, LANES - 2 * MLSTM_HEADS))).astype(BF16),
        "bias": jnp.pad(b_gates, ((0, 0), (0, LANES - 2 * MLSTM_HEADS))),
        "g_ret": g_ret,
        "g_ml": g_mlstm,
        "w_out": w_out[0].astype(BF16),
        "g_ffn": g_ffn,
        "w_q": w_peer_q[0].astype(BF16),
        "k1": peer_keys1[0].astype(BF16),
        "k2": peer_keys2[0].astype(BF16),
        "uv": jnp.concatenate([_pack_bf16_pairs(peer_u[0]), _pack_bf16_pairs(peer_v[0])], axis=1)[:, None, :],
        "g_ple": g_ple,
        "w_ple_gate": w_ple_gate[0].astype(BF16),
        "w_ple_proj": w_ple_proj[0].astype(BF16),
        "g_final": g_final[None, :],
    }


def kernel(x_prompt, x_sample, p_prompt, p_sample, state_ret, state_mlstm_C, state_mlstm_n, state_mlstm_m, g_mix, w_in, b_gates, g_ret, g_mlstm, w_out, g_ffn, w_peer_q, peer_keys1, peer_keys2, peer_u, peer_v, g_ple, w_ple_gate, w_ple_proj, g_final):
    w = _prep_weights(g_mix, w_in, b_gates, g_ret, g_mlstm, w_out, g_ffn, w_peer_q, peer_keys1, peer_keys2, peer_u, peer_v,
                      g_ple, w_ple_gate, w_ple_proj, g_final)
    bp, lp, _ = x_prompt.shape
    bs, ls, _ = x_sample.shape
    zeros = lambda *shape: jnp.zeros(shape, F32)
    y_s, ret_s, c_s, n_s, m_s = _stream(
        x_sample, p_sample[0], PAST_LEN + jnp.arange(ls, dtype=F32),
        state_ret[0], state_mlstm_C[0], state_mlstm_n[0], state_mlstm_m[0], w)
    y_p, ret_p, c_p, n_p, m_p = _stream(
        x_prompt, p_prompt[0], jnp.arange(lp, dtype=F32),
        zeros(bp, RET_HEADS, RET_D, RET_D), zeros(bp, MLSTM_HEADS, MLSTM_DK, MLSTM_DV),
        zeros(bp, MLSTM_HEADS, MLSTM_DK), zeros(bp, MLSTM_HEADS), w)
    return (y_p, y_s, ret_p, c_p, n_p, m_p, ret_s, c_s, n_s, m_s)
```

```python
import math

import numpy as np
import jax
import jax.numpy as jnp
from jax import lax
from jax.experimental import pallas as pl
from jax.experimental.pallas import tpu as pltpu

F32 = jnp.float32
BF16 = jnp.bfloat16

D_MODEL = 2048
CHUNK = 64
PAST_LEN = 1024
RMS_EPS = 1e-6
ROPE_BASE = 10000.0
RET_HEADS = 8
RET_D = 128
MLSTM_HEADS = 4
MLSTM_DK = 128
MLSTM_DV = 256
Z_COLS = 7168
OFF_RQ, OFF_RK, OFF_RV, OFF_RG = 0, 1024, 2048, 3072
OFF_MQ, OFF_MK, OFF_MV, OFF_MO = 4096, 4608, 5120, 6144
PEER_HEADS = 8
N_KEYS = 128
PEER_TOPK = 16
PEER_K = PEER_HEADS * PEER_TOPK
PLE_DIM = 256
LANES = 128
SUBLANES = 8
CAND_ROWS = PEER_TOPK + (SUBLANES - 1) * SUBLANES + SUBLANES

LOG_GAMMA = [float(np.log(np.float32(1.0) - np.float32(2.0) ** np.float32(-5.0 - h))) for h in range(RET_HEADS)]

VMEM_LIMIT = 56 * 1024 * 1024


def _params(sem):
    return pltpu.CompilerParams(dimension_semantics=sem, vmem_limit_bytes=VMEM_LIMIT)


def _rms(x, g):
    return x * lax.rsqrt(jnp.mean(x * x, axis=-1, keepdims=True) + RMS_EPS) * g


def _mm(a, b):
    return jnp.dot(a.astype(BF16), b.astype(BF16), preferred_element_type=F32)


def _mm_nt(a, b):
    return lax.dot_general(a.astype(BF16), b.astype(BF16), (((1,), (1,)), ((), ())), preferred_element_type=F32)


def _sigmoid(x):
    return 1.0 / (1.0 + jnp.exp(-x))


def _in_proj_kernel(x_ref, g_ref, w_ref, wg_ref, z_ref, gz_ref, a_scr):
    @pl.when(pl.program_id(1) == 0)
    def _():
        a_scr[...] = _rms(x_ref[...], g_ref[...]).astype(BF16)
        gz_ref[...] = jnp.dot(a_scr[...], wg_ref[...], preferred_element_type=F32)

    z_ref[...] = jnp.dot(a_scr[...], w_ref[...], preferred_element_type=F32)


def _in_proj(x, g, w, wg, tm=1024, tn=1024):
    t = x.shape[0]
    tm = min(tm, t)
    return pl.pallas_call(
        _in_proj_kernel,
        grid=(t // tm, Z_COLS // tn),
        in_specs=[
            pl.BlockSpec((tm, D_MODEL), lambda i, j: (i, 0)),
            pl.BlockSpec((1, D_MODEL), lambda i, j: (0, 0)),
            pl.BlockSpec((D_MODEL, tn), lambda i, j: (0, j)),
            pl.BlockSpec((D_MODEL, LANES), lambda i, j: (0, 0)),
        ],
        out_specs=[
            pl.BlockSpec((tm, tn), lambda i, j: (i, j)),
            pl.BlockSpec((tm, LANES), lambda i, j: (i, 0)),
        ],
        out_shape=[jax.ShapeDtypeStruct((t, Z_COLS), F32), jax.ShapeDtypeStruct((t, LANES), F32)],
        scratch_shapes=[pltpu.VMEM((tm, D_MODEL), BF16)],
        compiler_params=_params(("arbitrary", "arbitrary")),
        name="in_proj",
    )(x, g, w, wg)


def _log_sigmoid(x):
    return -(jnp.maximum(-x, 0.0) + jnp.log1p(jnp.exp(-jnp.abs(x))))


def _mixers_kernel(z_ref, gz_ref, bias_ref, cos_ref, sin_ref, s0_ref, c0_ref, n0_ref, m0_ref, gret_ref, gml_ref,
                   mix_ref, s_ref, c_ref, n_ref, m_ref):
    @pl.when(pl.program_id(1) == 0)
    def _():
        s_ref[...] = s0_ref[...]
        c_ref[...] = c0_ref[...]
        n_ref[...] = n0_ref[...]
        m_ref[...] = m0_ref[...]

    cl = CHUNK
    row = lax.broadcasted_iota(jnp.int32, (cl, cl), 0)
    col = lax.broadcasted_iota(jnp.int32, (cl, cl), 1)
    causal = row >= col
    diff = jnp.where(causal, (row - col).astype(F32), 0.0)
    posc = lax.broadcasted_iota(jnp.int32, (cl, 1), 0).astype(F32)
    cosf = cos_ref[...]
    sinf = sin_ref[...]

    def rope(x):
        return x * cosf + pltpu.roll(x, RET_D // 2, axis=1) * sinf

    ret = []
    for h in range(RET_HEADS):
        lg = LOG_GAMMA[h]
        lo = h * RET_D
        q = rope(z_ref[0, :, OFF_RQ + lo:OFF_RQ + lo + RET_D])
        k = rope(z_ref[0, :, OFF_RK + lo:OFF_RK + lo + RET_D]) * (RET_D ** -0.5)
        q_dec = (q * jnp.exp((posc + 1.0) * lg)).astype(BF16)
        k_dec_t = (k * jnp.exp((cl - 1.0 - posc) * lg)).T.astype(BF16)
        ret.append((q.astype(BF16), k.astype(BF16), q_dec, k_dec_t))

    gates = gz_ref[0] + bias_ref[...]
    tri = causal.astype(F32)
    bcum = jnp.dot(tri, _log_sigmoid(gates), preferred_element_type=F32, precision=lax.Precision.HIGHEST)
    bcum_t = bcum.T
    gates_t = gates.T
    mls = []
    for h in range(MLSTM_HEADS):
        k = z_ref[0, :, OFF_MK + h * MLSTM_DK:OFF_MK + (h + 1) * MLSTM_DK] * (MLSTM_DK ** -0.5)
        f = MLSTM_HEADS + h
        b_col = bcum[:, f:f + 1]
        b_row = bcum_t[f:f + 1, :]
        ig_col = gates[:, h:h + 1]
        ig_row = gates_t[h:h + 1, :]
        m_prev = m_ref[0, h:h + 1, 0:1]
        dlog = jnp.where(causal, b_col - b_row + ig_row, -jnp.inf)
        inter_log = b_col + m_prev
        m_t = jnp.maximum(inter_log, jnp.max(dlog, axis=-1, keepdims=True))
        dw = jnp.exp(dlog - m_t)
        inter_w = jnp.exp(inter_log - m_t)
        m_new = m_t[cl - 1:cl, :]
        b_last = b_col[cl - 1:cl, :]
        ws = jnp.exp(b_last - b_col + ig_col - m_new)
        carry = jnp.exp(b_last + m_prev - m_new)
        kw = k * ws
        n_old = n_ref[0, h:h + 1, :]
        n_ref[0, h:h + 1, :] = carry * n_old + jnp.sum(kw, axis=0, keepdims=True)
        m_ref[0, h:h + 1, :] = jnp.broadcast_to(m_new, (1, LANES))
        mls.append((k.astype(BF16), kw.T.astype(BF16), dw, inter_w, m_t, carry, n_old))

    for h in range(RET_HEADS):
        lg = LOG_GAMMA[h]
        lo = h * RET_D
        q, k, q_dec, k_dec_t = ret[h]
        v = z_ref[0, :, OFF_RV + lo:OFF_RV + lo + RET_D].astype(BF16)
        rg = z_ref[0, :, OFF_RG + lo:OFF_RG + lo + RET_D]
        decay = jnp.where(causal, jnp.exp(diff * lg), 0.0)
        s_old = s_ref[0, h]
        scores = _mm_nt(q, k) * decay
        o = _mm(scores, v) + _mm(q_dec, s_old)
        s_ref[0, h] = math.exp(cl * lg) * s_old + _mm(k_dec_t, v)
        y = o * lax.rsqrt(jnp.mean(o * o, axis=-1, keepdims=True) + RMS_EPS) * gret_ref[:, lo:lo + RET_D]
        mix_ref[0, :, lo:lo + RET_D] = (y * (rg * _sigmoid(rg))).astype(BF16)

    for h in range(MLSTM_HEADS):
        k, kw_t, dw, inter_w, m_t, carry, n_old = mls[h]
        q = z_ref[0, :, OFF_MQ + h * MLSTM_DK:OFF_MQ + (h + 1) * MLSTM_DK]
        v = z_ref[0, :, OFF_MV + h * MLSTM_DV:OFF_MV + (h + 1) * MLSTM_DV].astype(BF16)
        mo = z_ref[0, :, OFF_MO + h * MLSTM_DV:OFF_MO + (h + 1) * MLSTM_DV]
        c_old = c_ref[0, h]
        sm = _mm_nt(q, k) * dw
        num = _mm(sm, v) + inter_w * _mm(q, c_old)
        den = jnp.sum(sm, axis=-1, keepdims=True) + inter_w * jnp.sum(q * n_old, axis=-1, keepdims=True)
        hh = num / jnp.maximum(jnp.abs(den), jnp.exp(-m_t))
        c_ref[0, h] = carry * c_old + _mm(kw_t, v)
        y = hh * lax.rsqrt(jnp.mean(hh * hh, axis=-1, keepdims=True) + RMS_EPS) * gml_ref[:, h * MLSTM_DV:(h + 1) * MLSTM_DV]
        lo = RET_HEADS * RET_D + h * MLSTM_DV
        mix_ref[0, :, lo:lo + MLSTM_DV] = (y * _sigmoid(mo)).astype(BF16)


def _mixers(z, gz, bias, cosf, sinf, s0, c0, n0, m0, g_ret, g_ml):
    b, l, _ = z.shape
    per_b = lambda *tail: (lambda i, c: (i,) + tail)
    return pl.pallas_call(
        _mixers_kernel,
        grid=(b, l // CHUNK),
        in_specs=[
            pl.BlockSpec((1, CHUNK, Z_COLS), lambda i, c: (i, c, 0)),
            pl.BlockSpec((1, CHUNK, LANES), lambda i, c: (i, c, 0)),
            pl.BlockSpec((1, LANES), lambda i, c: (0, 0)),
            pl.BlockSpec((CHUNK, RET_D), lambda i, c: (c, 0)),
            pl.BlockSpec((CHUNK, RET_D), lambda i, c: (c, 0)),
            pl.BlockSpec((1, RET_HEADS, RET_D, RET_D), per_b(0, 0, 0)),
            pl.BlockSpec((1, MLSTM_HEADS, MLSTM_DK, MLSTM_DV), per_b(0, 0, 0)),
            pl.BlockSpec((1, MLSTM_HEADS, MLSTM_DK), per_b(0, 0)),
            pl.BlockSpec((1, MLSTM_HEADS, LANES), per_b(0, 0)),
            pl.BlockSpec((1, RET_HEADS * RET_D), lambda i, c: (0, 0)),
            pl.BlockSpec((1, MLSTM_HEADS * MLSTM_DV), lambda i, c: (0, 0)),
        ],
        out_specs=[
            pl.BlockSpec((1, CHUNK, D_MODEL), lambda i, c: (i, c, 0)),
            pl.BlockSpec((1, RET_HEADS, RET_D, RET_D), per_b(0, 0, 0)),
            pl.BlockSpec((1, MLSTM_HEADS, MLSTM_DK, MLSTM_DV), per_b(0, 0, 0)),
            pl.BlockSpec((1, MLSTM_HEADS, MLSTM_DK), per_b(0, 0)),
            pl.BlockSpec((1, MLSTM_HEADS, LANES), per_b(0, 0)),
        ],
        out_shape=[
            jax.ShapeDtypeStruct((b, l, D_MODEL), BF16),
            jax.ShapeDtypeStruct((b, RET_HEADS, RET_D, RET_D), F32),
            jax.ShapeDtypeStruct((b, MLSTM_HEADS, MLSTM_DK, MLSTM_DV), F32),
            jax.ShapeDtypeStruct((b, MLSTM_HEADS, MLSTM_DK), F32),
            jax.ShapeDtypeStruct((b, MLSTM_HEADS, LANES), F32),
        ],
        compiler_params=_params(("arbitrary", "arbitrary")),
        name="mixers",
    )(z, gz, bias, cosf, sinf, s0, c0, n0, m0, g_ret, g_ml)


def _out_proj_kernel(a_ref, w_ref, r_ref, o_ref):
    o_ref[...] = r_ref[...] + jnp.dot(a_ref[...], w_ref[...], preferred_element_type=F32)


def _out_proj(a, w, r, tm=512):
    t = a.shape[0]
    return pl.pallas_call(
        _out_proj_kernel,
        grid=(t // tm,),
        in_specs=[
            pl.BlockSpec((tm, D_MODEL), lambda i: (i, 0)),
            pl.BlockSpec((D_MODEL, D_MODEL), lambda i: (0, 0)),
            pl.BlockSpec((tm, D_MODEL), lambda i: (i, 0)),
        ],
        out_specs=pl.BlockSpec((tm, D_MODEL), lambda i: (i, 0)),
        out_shape=jax.ShapeDtypeStruct((t, D_MODEL), F32),
        compiler_params=_params(("arbitrary",)),
        name="out_proj",
    )(a, w, r)


def _top16_steps(x_ref, payload_ref, val_ref, idx_ref):
    rows = x_ref.shape[0]
    for r in range(PEER_TOPK):
        x = x_ref[...]
        iota = lax.broadcasted_iota(jnp.int32, x.shape, 0)
        m = jnp.max(x, axis=0, keepdims=True)
        am = jnp.min(jnp.where(x == m, iota, rows), axis=0, keepdims=True)
        sel = iota == am
        val_ref[r:r + 1, :] = m
        if payload_ref is None:
            idx_ref[r:r + 1, :] = am
        else:
            idx_ref[r:r + 1, :] = jnp.max(jnp.where(sel, payload_ref[...], -1), axis=0, keepdims=True)
        x_ref[...] = jnp.where(sel, -jnp.inf, x)
        yield


ROUTE_SCRATCH = [
    pltpu.VMEM((N_KEYS, LANES), F32),
    pltpu.VMEM((N_KEYS, LANES), F32),
    pltpu.VMEM((PEER_TOPK, LANES), F32),
    pltpu.VMEM((PEER_TOPK, LANES), jnp.int32),
    pltpu.VMEM((PEER_TOPK, LANES), F32),
    pltpu.VMEM((PEER_TOPK, LANES), jnp.int32),
    pltpu.VMEM((CAND_ROWS, LANES), F32),
    pltpu.VMEM((CAND_ROWS, LANES), jnp.int32),
    pltpu.VMEM((PEER_TOPK, LANES), F32),
    pltpu.VMEM((PEER_TOPK, LANES), jnp.int32),
]


def _route_unit_steps(q1, q2, k1_ref, k2_ref, work, emit):
    s1_scr, s2_scr, t1_scr, i1_scr, t2_scr, i2_scr, cand_scr, cidx_scr, sc_scr, e_scr = work
    s1_scr[...] = _mm_nt(k1_ref[...], q1)
    s2_scr[...] = _mm_nt(k2_ref[...], q2)
    yield
    for _ in zip(_top16_steps(s1_scr, None, t1_scr, i1_scr), _top16_steps(s2_scr, None, t2_scr, i2_scr)):
        yield
    cand_scr[0:PEER_TOPK, :] = t1_scr[0:1, :] + t2_scr[...]
    cidx_scr[0:PEER_TOPK, :] = i1_scr[0:1, :] * N_KEYS + i2_scr[...]
    t2 = t2_scr[0:SUBLANES, :]
    i2 = i2_scr[0:SUBLANES, :]
    sub_iota = lax.broadcasted_iota(jnp.int32, (SUBLANES, LANES), 0)
    for a in range(1, SUBLANES):
        lo = PEER_TOPK + (a - 1) * SUBLANES
        cand_scr[lo:lo + SUBLANES, :] = jnp.where(sub_iota < PEER_TOPK // (a + 1), t1_scr[a:a + 1, :] + t2, -jnp.inf)
        cidx_scr[lo:lo + SUBLANES, :] = i1_scr[a:a + 1, :] * N_KEYS + i2
    cand_scr[CAND_ROWS - SUBLANES:CAND_ROWS, :] = t1_scr[SUBLANES:PEER_TOPK, :] + t2_scr[0:1, :]
    cidx_scr[CAND_ROWS - SUBLANES:CAND_ROWS, :] = i1_scr[SUBLANES:PEER_TOPK, :] * N_KEYS + i2_scr[0:1, :]
    yield
    yield from _top16_steps(cand_scr, cidx_scr, sc_scr, e_scr)
    sc = sc_scr[...]
    p = jnp.exp(sc - sc[0:1, :])
    emit(e_scr[...], p / jnp.sum(p, axis=0, keepdims=True))


def _qproj_kernel(h_ref, g_ref, wq_ref, q_ref):
    c = _rms(h_ref[...], g_ref[...]).astype(BF16)
    q = jnp.dot(c, wq_ref[...], preferred_element_type=F32).astype(BF16)
    for sub in range(q_ref.shape[0]):
        for j in range(2 * PEER_HEADS):
            q_ref[sub, j] = q[sub * LANES:(sub + 1) * LANES, j * N_KEYS:(j + 1) * N_KEYS]


def _qproj(h, g, wq, tm=512):
    t = h.shape[0]
    subs = tm // LANES
    return pl.pallas_call(
        _qproj_kernel,
        grid=(t // tm,),
        in_specs=[
            pl.BlockSpec((tm, D_MODEL), lambda i: (i, 0)),
            pl.BlockSpec((1, D_MODEL), lambda i: (0, 0)),
            pl.BlockSpec((D_MODEL, D_MODEL), lambda i: (0, 0)),
        ],
        out_specs=pl.BlockSpec((subs, 2 * PEER_HEADS, LANES, N_KEYS), lambda i: (i, 0, 0, 0)),
        out_shape=jax.ShapeDtypeStruct((t // LANES, 2 * PEER_HEADS, LANES, N_KEYS), BF16),
        compiler_params=_params(("arbitrary",)),
        name="qproj",
    )(h, g, wq)


PEER_TB = 512
PEER_SLOTS = 8
PEER_LEAD = PEER_SLOTS - 1
PEER_SUBS = PEER_TB // LANES
ROUTE_UNITS = PEER_HEADS * PEER_SUBS
PEER_GROUP = PEER_TB // ROUTE_UNITS
assert PEER_GROUP % PEER_SLOTS == 0
ROUTE_STEPS = 2 * PEER_TOPK + 3
SIDE_AFTER = (4, 10, 15)
assert len(SIDE_AFTER) * PEER_GROUP >= ROUTE_STEPS
HALF_D = D_MODEL // 2
PACK_TILES = HALF_D // LANES
INV_SQRT2 = 0.7071067811865476


def _pack_bf16_pairs(a):
    bits = lax.bitcast_convert_type(a.astype(BF16), jnp.uint16).astype(jnp.uint32)
    return bits[:, :HALF_D] | (bits[:, HALF_D:] << 16)


def _peer_kernel(q_this, q_next, k1_ref, k2_ref, h_ref, g_ref, uv_hbm, o_ref,
                 idx_s, idx_v, gate_scr, et_scr, gt_scr, c_scr, *scratch):
    work = scratch[:len(ROUTE_SCRATCH)]
    rows = scratch[len(ROUTE_SCRATCH):len(ROUTE_SCRATCH) + PEER_SLOTS]
    idx_sem, row_sems = scratch[len(ROUTE_SCRATCH) + PEER_SLOTS:]
    tb = h_ref.shape[0]

    def route(q_ref, u):
        if isinstance(u, int):
            sub, hd = u % PEER_SUBS, u // PEER_SUBS
            lo = hd * PEER_TOPK
        else:
            sub = jnp.bitwise_and(u, PEER_SUBS - 1)
            hd = jnp.right_shift(u, PEER_SUBS.bit_length() - 1)
            lo = pl.multiple_of(hd * PEER_TOPK, PEER_TOPK)

        def emit(e, gate):
            et_scr[sub, pl.ds(lo, PEER_TOPK), :] = e
            gt_scr[sub, pl.ds(lo, PEER_TOPK), :] = gate

        return _route_unit_steps(q_ref[sub, 2 * hd], q_ref[sub, 2 * hd + 1], k1_ref, k2_ref, work, emit)

    def finish(steps):
        for _ in steps:
            pass

    ids_copy = pltpu.make_async_copy(idx_v, idx_s, idx_sem)

    def publish():
        for sub in range(PEER_SUBS):
            idx_v[sub * LANES:(sub + 1) * LANES, :] = et_scr[sub].T
            gate_scr[sub * LANES:(sub + 1) * LANES, :] = gt_scr[sub].T
        ids_copy.start()

    @pl.when(pl.program_id(0) == 0)
    def _():
        def body(u, carry):
            finish(route(q_this, u))
            return carry

        lax.fori_loop(0, ROUTE_UNITS, body, 0)
        publish()

    c_scr[...] = _rms(h_ref[...], g_ref[...])
    ids_copy.wait()

    def issue(t, slot, k0, k1):
        for k in range(k0, k1):
            e = idx_s[t, k]
            pltpu.make_async_copy(uv_hbm.at[e], rows[slot].at[pl.ds(k, 1), :], row_sems.at[slot]).start(priority=k % 2)

    def wait(slot):
        pltpu.make_async_copy(uv_hbm.at[pl.ds(0, PEER_K), 0], rows[slot], row_sems.at[slot]).wait()

    def unpack(words):
        lo = lax.bitcast_convert_type(jnp.left_shift(words, jnp.uint32(16)), F32)
        hi = lax.bitcast_convert_type(jnp.bitwise_and(words, jnp.uint32(0xFFFF0000)), F32)
        return lo, hi

    eye = lax.broadcasted_iota(jnp.int32, (PEER_K, LANES), 0) == lax.broadcasted_iota(jnp.int32, (PEER_K, LANES), 1)
    per_piece = PEER_K // (2 * PACK_TILES)

    def token(t, slot, prefetch, side_steps):
        wait(slot)
        ahead_slot = (slot + PEER_LEAD) % PEER_SLOTS
        x = c_scr[pl.ds(t, 1), :]
        acc = None
        for j in range(PACK_TILES):
            if prefetch:
                issue(t + PEER_LEAD, ahead_slot, j * per_piece, (j + 1) * per_piece)
            lo, hi = unpack(rows[slot][:, j * LANES:(j + 1) * LANES])
            p = lo * x[:, j * LANES:(j + 1) * LANES] + hi * x[:, HALF_D + j * LANES:HALF_D + (j + 1) * LANES]
            acc = p if acc is None else acc + p
            if j in SIDE_AFTER:
                next(side_steps, None)
        act = jnp.sum(acc, axis=1, keepdims=True)
        gate_col = jnp.sum(jnp.where(eye, gate_scr[pl.ds(t, 1), :], 0.0), axis=1, keepdims=True)
        coef = gate_col * (0.5 * act * (1.0 + lax.erf(act * INV_SQRT2)))
        mixed_lo, mixed_hi = [], []
        for j in range(PACK_TILES):
            if prefetch:
                issue(t + PEER_LEAD, ahead_slot, (PACK_TILES + j) * per_piece, (PACK_TILES + j + 1) * per_piece)
            lo, hi = unpack(rows[slot][:, HALF_D + j * LANES:HALF_D + (j + 1) * LANES])
            mixed_lo.append(jnp.sum(lo * coef, axis=0, keepdims=True))
            mixed_hi.append(jnp.sum(hi * coef, axis=0, keepdims=True))
            if PACK_TILES + j in SIDE_AFTER:
                next(side_steps, None)
        o_ref[pl.ds(t, 1), :] = h_ref[pl.ds(t, 1), :] + jnp.concatenate(mixed_lo + mixed_hi, axis=1)

    for s in range(PEER_LEAD):
        issue(s, s, 0, PEER_K)

    def group(i, carry):
        steps = route(q_next, i)
        for s in range(PEER_GROUP):
            token(i * PEER_GROUP + s, s % PEER_SLOTS, True, steps)
        finish(steps)
        return carry

    lax.fori_loop(0, ROUTE_UNITS - 1, group, 0)
    last = tb - PEER_GROUP
    steps = route(q_next, ROUTE_UNITS - 1)
    for s in range(PEER_GROUP):
        token(last + s, s % PEER_SLOTS, s + PEER_LEAD < PEER_GROUP, steps)
    finish(steps)
    pl.when(pl.program_id(0) + 1 < pl.num_programs(0))(publish)


def _peer(q, k1, k2, h, g, uv):
    t = h.shape[0]
    tb = PEER_TB
    steps = t // tb
    q_block = (PEER_SUBS, 2 * PEER_HEADS, LANES, N_KEYS)
    return pl.pallas_call(
        _peer_kernel,
        grid=(steps,),
        in_specs=[
            pl.BlockSpec(q_block, lambda i: (i, 0, 0, 0)),
            pl.BlockSpec(q_block, lambda i: (jnp.minimum(i + 1, steps - 1), 0, 0, 0)),
            pl.BlockSpec((N_KEYS, N_KEYS), lambda i: (0, 0)),
            pl.BlockSpec((N_KEYS, N_KEYS), lambda i: (0, 0)),
            pl.BlockSpec((tb, D_MODEL), lambda i: (i, 0)),
            pl.BlockSpec((1, D_MODEL), lambda i: (0, 0)),
            pl.BlockSpec(memory_space=pl.ANY),
        ],
        out_specs=pl.BlockSpec((tb, D_MODEL), lambda i: (i, 0)),
        out_shape=jax.ShapeDtypeStruct((t, D_MODEL), F32),
        scratch_shapes=[
            pltpu.SMEM((tb, PEER_K), jnp.int32),
            pltpu.VMEM((tb, PEER_K), jnp.int32),
            pltpu.VMEM((tb, PEER_K), F32),
            pltpu.VMEM((PEER_SUBS, PEER_K, LANES), jnp.int32),
            pltpu.VMEM((PEER_SUBS, PEER_K, LANES), F32),
            pltpu.VMEM((tb, D_MODEL), F32),
        ] + ROUTE_SCRATCH + [pltpu.VMEM((PEER_K, D_MODEL), jnp.uint32)] * PEER_SLOTS + [
            pltpu.SemaphoreType.DMA(()),
            pltpu.SemaphoreType.DMA((PEER_SLOTS,)),
        ],
        compiler_params=_params(("arbitrary",)),
        name="peer",
    )(q, q, k1, k2, h, g, uv)


def _ple_kernel(h_ref, p_ref, gple_ref, wg_ref, wp_ref, gfin_ref, y_ref):
    h = h_ref[...]
    e = _rms(h, gple_ref[...]).astype(BF16)
    gate = _sigmoid(jnp.dot(e, wg_ref[...], preferred_element_type=F32))
    proj = jnp.dot(p_ref[...].astype(BF16), wp_ref[...], preferred_element_type=F32)
    y_ref[...] = _rms(h + gate * proj, gfin_ref[...])


def _ple(h, p, g_ple, wg, wp, g_fin, tm=512):
    t = h.shape[0]
    return pl.pallas_call(
        _ple_kernel,
        grid=(t // tm,),
        in_specs=[
            pl.BlockSpec((tm, D_MODEL), lambda i: (i, 0)),
            pl.BlockSpec((tm, PLE_DIM), lambda i: (i, 0)),
            pl.BlockSpec((1, D_MODEL), lambda i: (0, 0)),
            pl.BlockSpec((D_MODEL, D_MODEL), lambda i: (0, 0)),
            pl.BlockSpec((PLE_DIM, D_MODEL), lambda i: (0, 0)),
            pl.BlockSpec((1, D_MODEL), lambda i: (0, 0)),
        ],
        out_specs=pl.BlockSpec((tm, D_MODEL), lambda i: (i, 0)),
        out_shape=jax.ShapeDtypeStruct((t, D_MODEL), F32),
        compiler_params=_params(("arbitrary",)),
        name="ple",
    )(h, p, g_ple, wg, wp, g_fin)


def _rope_tables(pos):
    half = RET_D // 2
    inv = ROPE_BASE ** (-jnp.arange(half, dtype=F32) / half)
    ang = pos[:, None] * inv[None, :]
    cos, sin = jnp.cos(ang), jnp.sin(ang)
    return jnp.concatenate([cos, cos], axis=-1), jnp.concatenate([-sin, sin], axis=-1)


def _stream(x, p, pos, s0, c0, n0, m0, w):
    b, l, _ = x.shape
    t = b * l
    x2 = x.reshape(t, D_MODEL)
    z, gz = _in_proj(x2, w["g_mix"], w["w_in"], w["w_gates"])
    cosf, sinf = _rope_tables(pos)
    m0b = jnp.broadcast_to(m0[:, :, None], (b, MLSTM_HEADS, LANES))
    mix, s_new, c_new, n_new, m_new = _mixers(
        z.reshape(b, l, Z_COLS), gz.reshape(b, l, LANES), w["bias"], cosf, sinf, s0, c0, n0, m0b, w["g_ret"], w["g_ml"])
    h1 = _out_proj(mix.reshape(t, D_MODEL), w["w_out"], x2)
    q = _qproj(h1, w["g_ffn"], w["w_q"])
    h2 = _peer(q, w["k1"], w["k2"], h1, w["g_ffn"], w["uv"])
    y = _ple(h2, p.reshape(t, PLE_DIM), w["g_ple"], w["w_ple_gate"], w["w_ple_proj"], w["g_final"])
    return y.reshape(b, l, D_MODEL), s_new[None], c_new[None], n_new[None], m_new[None, :, :, 0]


def _prep_weights(g_mix, w_in, b_gates, g_ret, g_mlstm, w_out, g_ffn, w_peer_q, peer_keys1, peer_keys2, peer_u, peer_v,
                  g_ple, w_ple_gate, w_ple_proj, g_final):
    w_in0 = w_in[0]
    return {
        "g_mix": g_mix,
        "w_in": w_in0[:, :Z_COLS].astype(BF16),
        "w_gates": jnp.pad(w_in0[:, Z_COLS:], ((0, 0), (0, LANES - 2 * MLSTM_HEADS))).astype(BF16),
        "bias": jnp.pad(b_gates, ((0, 0), (0, LANES - 2 * MLSTM_HEADS))),
        "g_ret": g_ret,
        "g_ml": g_mlstm,
        "w_out": w_out[0].astype(BF16),
        "g_ffn": g_ffn,
        "w_q": w_peer_q[0].astype(BF16),
        "k1": peer_keys1[0].astype(BF16),
        "k2": peer_keys2[0].astype(BF16),
        "uv": jnp.concatenate([_pack_bf16_pairs(peer_u[0]), _pack_bf16_pairs(peer_v[0])], axis=1)[:, None, :],
        "g_ple": g_ple,
        "w_ple_gate": w_ple_gate[0].astype(BF16),
        "w_ple_proj": w_ple_proj[0].astype(BF16),
        "g_final": g_final[None, :],
    }


def kernel(x_prompt, x_sample, p_prompt, p_sample, state_ret, state_mlstm_C, state_mlstm_n, state_mlstm_m, g_mix, w_in, b_gates, g_ret, g_mlstm, w_out, g_ffn, w_peer_q, peer_keys1, peer_keys2, peer_u, peer_v, g_ple, w_ple_gate, w_ple_proj, g_final):
    w = _prep_weights(g_mix, w_in, b_gates, g_ret, g_mlstm, w_out, g_ffn, w_peer_q, peer_keys1, peer_keys2, peer_u, peer_v,
                      g_ple, w_ple_gate, w_ple_proj, g_final)
    bp, lp, _ = x_prompt.shape
    bs, ls, _ = x_sample.shape
    zeros = lambda *shape: jnp.zeros(shape, F32)
    y_s, ret_s, c_s, n_s, m_s = _stream(
        x_sample, p_sample[0], PAST_LEN + jnp.arange(ls, dtype=F32),
        state_ret[0], state_mlstm_C[0], state_mlstm_n[0], state_mlstm_m[0], w)
    y_p, ret_p, c_p, n_p, m_p = _stream(
        x_prompt, p_prompt[0], jnp.arange(lp, dtype=F32),
        zeros(bp, RET_HEADS, RET_D, RET_D), zeros(bp, MLSTM_HEADS, MLSTM_DK, MLSTM_DV),
        zeros(bp, MLSTM_HEADS, MLSTM_DK), zeros(bp, MLSTM_HEADS), w)
    return (y_p, y_s, ret_p, c_p, n_p, m_p, ret_s, c_s, n_s, m_s)
```

```python
import math

import numpy as np
import jax
import jax.numpy as jnp
from jax import lax
from jax.experimental import pallas as pl
from jax.experimental.pallas import tpu as pltpu

F32 = jnp.float32
BF16 = jnp.bfloat16

D_MODEL = 2048
CHUNK = 64
PAST_LEN = 1024
RMS_EPS = 1e-6
ROPE_BASE = 10000.0
RET_HEADS = 8
RET_D = 128
MLSTM_HEADS = 4
MLSTM_DK = 128
MLSTM_DV = 256
Z_COLS = 7168
OFF_RQ, OFF_RK, OFF_RV, OFF_RG = 0, 1024, 2048, 3072
OFF_MQ, OFF_MK, OFF_MV, OFF_MO = 4096, 4608, 5120, 6144
PEER_HEADS = 8
N_KEYS = 128
PEER_TOPK = 16
PEER_K = PEER_HEADS * PEER_TOPK
PLE_DIM = 256
LANES = 128
SUBLANES = 8
CAND_ROWS = PEER_TOPK + (SUBLANES - 1) * SUBLANES + SUBLANES

LOG_GAMMA = [float(np.log(np.float32(1.0) - np.float32(2.0) ** np.float32(-5.0 - h))) for h in range(RET_HEADS)]

VMEM_LIMIT = 56 * 1024 * 1024


def _params(sem):
    return pltpu.CompilerParams(dimension_semantics=sem, vmem_limit_bytes=VMEM_LIMIT)


def _rms(x, g):
    return x * lax.rsqrt(jnp.mean(x * x, axis=-1, keepdims=True) + RMS_EPS) * g


def _mm(a, b):
    return jnp.dot(a.astype(BF16), b.astype(BF16), preferred_element_type=F32)


def _mm_nt(a, b):
    return lax.dot_general(a.astype(BF16), b.astype(BF16), (((1,), (1,)), ((), ())), preferred_element_type=F32)


def _sigmoid(x):
    return 1.0 / (1.0 + jnp.exp(-x))


def _in_proj_kernel(x_ref, g_ref, w_ref, wg_ref, z_ref, gz_ref, a_scr):
    @pl.when(pl.program_id(1) == 0)
    def _():
        a_scr[...] = _rms(x_ref[...], g_ref[...]).astype(BF16)
        gz_ref[...] = jnp.dot(a_scr[...], wg_ref[...], preferred_element_type=F32)

    z_ref[...] = jnp.dot(a_scr[...], w_ref[...], preferred_element_type=F32)


def _in_proj(x, g, w, wg, tm=1024, tn=1024):
    t = x.shape[0]
    tm = min(tm, t)
    return pl.pallas_call(
        _in_proj_kernel,
        grid=(t // tm, Z_COLS // tn),
        in_specs=[
            pl.BlockSpec((tm, D_MODEL), lambda i, j: (i, 0)),
            pl.BlockSpec((1, D_MODEL), lambda i, j: (0, 0)),
            pl.BlockSpec((D_MODEL, tn), lambda i, j: (0, j)),
            pl.BlockSpec((D_MODEL, LANES), lambda i, j: (0, 0)),
        ],
        out_specs=[
            pl.BlockSpec((tm, tn), lambda i, j: (i, j)),
            pl.BlockSpec((tm, LANES), lambda i, j: (i, 0)),
        ],
        out_shape=[jax.ShapeDtypeStruct((t, Z_COLS), F32), jax.ShapeDtypeStruct((t, LANES), F32)],
        scratch_shapes=[pltpu.VMEM((tm, D_MODEL), BF16)],
        compiler_params=_params(("arbitrary", "arbitrary")),
        name="in_proj",
    )(x, g, w, wg)


def _log_sigmoid(x):
    return -(jnp.maximum(-x, 0.0) + jnp.log1p(jnp.exp(-jnp.abs(x))))


def _mixers_kernel(z_ref, gz_ref, bias_ref, cos_ref, sin_ref, s0_ref, c0_ref, n0_ref, m0_ref, gret_ref, gml_ref,
                   mix_ref, s_ref, c_ref, n_ref, m_ref):
    @pl.when(pl.program_id(1) == 0)
    def _():
        s_ref[...] = s0_ref[...]
        c_ref[...] = c0_ref[...]
        n_ref[...] = n0_ref[...]
        m_ref[...] = m0_ref[...]

    cl = CHUNK
    row = lax.broadcasted_iota(jnp.int32, (cl, cl), 0)
    col = lax.broadcasted_iota(jnp.int32, (cl, cl), 1)
    causal = row >= col
    diff = jnp.where(causal, (row - col).astype(F32), 0.0)
    posc = lax.broadcasted_iota(jnp.int32, (cl, 1), 0).astype(F32)
    cosf = cos_ref[...]
    sinf = sin_ref[...]

    def rope(x):
        return x * cosf + pltpu.roll(x, RET_D // 2, axis=1) * sinf

    ret = []
    for h in range(RET_HEADS):
        lg = LOG_GAMMA[h]
        lo = h * RET_D
        q = rope(z_ref[0, :, OFF_RQ + lo:OFF_RQ + lo + RET_D])
        k = rope(z_ref[0, :, OFF_RK + lo:OFF_RK + lo + RET_D]) * (RET_D ** -0.5)
        q_dec = (q * jnp.exp((posc + 1.0) * lg)).astype(BF16)
        k_dec_t = (k * jnp.exp((cl - 1.0 - posc) * lg)).T.astype(BF16)
        ret.append((q.astype(BF16), k.astype(BF16), q_dec, k_dec_t))

    gates = gz_ref[0] + bias_ref[...]
    tri = causal.astype(F32)
    bcum = jnp.dot(tri, _log_sigmoid(gates), preferred_element_type=F32, precision=lax.Precision.HIGHEST)
    bcum_t = bcum.T
    gates_t = gates.T
    mls = []
    for h in range(MLSTM_HEADS):
        k = z_ref[0, :, OFF_MK + h * MLSTM_DK:OFF_MK + (h + 1) * MLSTM_DK] * (MLSTM_DK ** -0.5)
        f = MLSTM_HEADS + h
        b_col = bcum[:, f:f + 1]
        b_row = bcum_t[f:f + 1, :]
        ig_col = gates[:, h:h + 1]
        ig_row = gates_t[h:h + 1, :]
        m_prev = m_ref[0, h:h + 1, 0:1]
        dlog = jnp.where(causal, b_col - b_row + ig_row, -jnp.inf)
        inter_log = b_col + m_prev
        m_t = jnp.maximum(inter_log, jnp.max(dlog, axis=-1, keepdims=True))
        dw = jnp.exp(dlog - m_t)
        inter_w = jnp.exp(inter_log - m_t)
        m_new = m_t[cl - 1:cl, :]
        b_last = b_col[cl - 1:cl, :]
        ws = jnp.exp(b_last - b_col + ig_col - m_new)
        carry = jnp.exp(b_last + m_prev - m_new)
        kw = k * ws
        n_old = n_ref[0, h:h + 1, :]
        n_ref[0, h:h + 1, :] = carry * n_old + jnp.sum(kw, axis=0, keepdims=True)
        m_ref[0, h:h + 1, :] = jnp.broadcast_to(m_new, (1, LANES))
        mls.append((k.astype(BF16), kw.T.astype(BF16), dw, inter_w, m_t, carry, n_old))

    for h in range(RET_HEADS):
        lg = LOG_GAMMA[h]
        lo = h * RET_D
        q, k, q_dec, k_dec_t = ret[h]
        v = z_ref[0, :, OFF_RV + lo:OFF_RV + lo + RET_D].astype(BF16)
        rg = z_ref[0, :, OFF_RG + lo:OFF_RG + lo + RET_D]
        decay = jnp.where(causal, jnp.exp(diff * lg), 0.0)
        s_old = s_ref[0, h]
        scores = _mm_nt(q, k) * decay
        o = _mm(scores, v) + _mm(q_dec, s_old)
        s_ref[0, h] = math.exp(cl * lg) * s_old + _mm(k_dec_t, v)
        y = o * lax.rsqrt(jnp.mean(o * o, axis=-1, keepdims=True) + RMS_EPS) * gret_ref[:, lo:lo + RET_D]
        mix_ref[0, :, lo:lo + RET_D] = (y * (rg * _sigmoid(rg))).astype(BF16)

    for h in range(MLSTM_HEADS):
        k, kw_t, dw, inter_w, m_t, carry, n_old = mls[h]
        q = z_ref[0, :, OFF_MQ + h * MLSTM_DK:OFF_MQ + (h + 1) * MLSTM_DK]
        v = z_ref[0, :, OFF_MV + h * MLSTM_DV:OFF_MV + (h + 1) * MLSTM_DV].astype(BF16)
        mo = z_ref[0, :, OFF_MO + h * MLSTM_DV:OFF_MO + (h + 1) * MLSTM_DV]
        c_old = c_ref[0, h]
        sm = _mm_nt(q, k) * dw
        num = _mm(sm, v) + inter_w * _mm(q, c_old)
        den = jnp.sum(sm, axis=-1, keepdims=True) + inter_w * jnp.sum(q * n_old, axis=-1, keepdims=True)
        hh = num / jnp.maximum(jnp.abs(den), jnp.exp(-m_t))
        c_ref[0, h] = carry * c_old + _mm(kw_t, v)
        y = hh * lax.rsqrt(jnp.mean(hh * hh, axis=-1, keepdims=True) + RMS_EPS) * gml_ref[:, h * MLSTM_DV:(h + 1) * MLSTM_DV]
        lo = RET_HEADS * RET_D + h * MLSTM_DV
        mix_ref[0, :, lo:lo + MLSTM_DV] = (y * _sigmoid(mo)).astype(BF16)


def _mixers(z, gz, bias, cosf, sinf, s0, c0, n0, m0, g_ret, g_ml):
    b, l, _ = z.shape
    per_b = lambda *tail: (lambda i, c: (i,) + tail)
    return pl.pallas_call(
        _mixers_kernel,
        grid=(b, l // CHUNK),
        in_specs=[
            pl.BlockSpec((1, CHUNK, Z_COLS), lambda i, c: (i, c, 0)),
            pl.BlockSpec((1, CHUNK, LANES), lambda i, c: (i, c, 0)),
            pl.BlockSpec((1, LANES), lambda i, c: (0, 0)),
            pl.BlockSpec((CHUNK, RET_D), lambda i, c: (c, 0)),
            pl.BlockSpec((CHUNK, RET_D), lambda i, c: (c, 0)),
            pl.BlockSpec((1, RET_HEADS, RET_D, RET_D), per_b(0, 0, 0)),
            pl.BlockSpec((1, MLSTM_HEADS, MLSTM_DK, MLSTM_DV), per_b(0, 0, 0)),
            pl.BlockSpec((1, MLSTM_HEADS, MLSTM_DK), per_b(0, 0)),
            pl.BlockSpec((1, MLSTM_HEADS, LANES), per_b(0, 0)),
            pl.BlockSpec((1, RET_HEADS * RET_D), lambda i, c: (0, 0)),
            pl.BlockSpec((1, MLSTM_HEADS * MLSTM_DV), lambda i, c: (0, 0)),
        ],
        out_specs=[
            pl.BlockSpec((1, CHUNK, D_MODEL), lambda i, c: (i, c, 0)),
            pl.BlockSpec((1, RET_HEADS, RET_D, RET_D), per_b(0, 0, 0)),
            pl.BlockSpec((1, MLSTM_HEADS, MLSTM_DK, MLSTM_DV), per_b(0, 0, 0)),
            pl.BlockSpec((1, MLSTM_HEADS, MLSTM_DK), per_b(0, 0)),
            pl.BlockSpec((1, MLSTM_HEADS, LANES), per_b(0, 0)),
        ],
        out_shape=[
            jax.ShapeDtypeStruct((b, l, D_MODEL), BF16),
            jax.ShapeDtypeStruct((b, RET_HEADS, RET_D, RET_D), F32),
            jax.ShapeDtypeStruct((b, MLSTM_HEADS, MLSTM_DK, MLSTM_DV), F32),
            jax.ShapeDtypeStruct((b, MLSTM_HEADS, MLSTM_DK), F32),
            jax.ShapeDtypeStruct((b, MLSTM_HEADS, LANES), F32),
        ],
        compiler_params=_params(("arbitrary", "arbitrary")),
        name="mixers",
    )(z, gz, bias, cosf, sinf, s0, c0, n0, m0, g_ret, g_ml)


def _out_proj_kernel(a_ref, w_ref, r_ref, o_ref):
    o_ref[...] = r_ref[...] + jnp.dot(a_ref[...], w_ref[...], preferred_element_type=F32)


def _out_proj(a, w, r, tm=512):
    t = a.shape[0]
    return pl.pallas_call(
        _out_proj_kernel,
        grid=(t // tm,),
        in_specs=[
            pl.BlockSpec((tm, D_MODEL), lambda i: (i, 0)),
            pl.BlockSpec((D_MODEL, D_MODEL), lambda i: (0, 0)),
            pl.BlockSpec((tm, D_MODEL), lambda i: (i, 0)),
        ],
        out_specs=pl.BlockSpec((tm, D_MODEL), lambda i: (i, 0)),
        out_shape=jax.ShapeDtypeStruct((t, D_MODEL), F32),
        compiler_params=_params(("arbitrary",)),
        name="out_proj",
    )(a, w, r)


def _top16_steps(x_ref, payload_ref, val_ref, idx_ref):
    rows = x_ref.shape[0]
    for r in range(PEER_TOPK):
        x = x_ref[...]
        iota = lax.broadcasted_iota(jnp.int32, x.shape, 0)
        m = jnp.max(x, axis=0, keepdims=True)
        am = jnp.min(jnp.where(x == m, iota, rows), axis=0, keepdims=True)
        sel = iota == am
        val_ref[r:r + 1, :] = m
        if payload_ref is None:
            idx_ref[r:r + 1, :] = am
        else:
            idx_ref[r:r + 1, :] = jnp.max(jnp.where(sel, payload_ref[...], -1), axis=0, keepdims=True)
        x_ref[...] = jnp.where(sel, -jnp.inf, x)
        yield


ROUTE_SCRATCH = [
    pltpu.VMEM((N_KEYS, LANES), F32),
    pltpu.VMEM((N_KEYS, LANES), F32),
    pltpu.VMEM((PEER_TOPK, LANES), F32),
    pltpu.VMEM((PEER_TOPK, LANES), jnp.int32),
    pltpu.VMEM((PEER_TOPK, LANES), F32),
    pltpu.VMEM((PEER_TOPK, LANES), jnp.int32),
    pltpu.VMEM((CAND_ROWS, LANES), F32),
    pltpu.VMEM((CAND_ROWS, LANES), jnp.int32),
    pltpu.VMEM((PEER_TOPK, LANES), F32),
    pltpu.VMEM((PEER_TOPK, LANES), jnp.int32),
]


def _route_unit_steps(q1, q2, k1_ref, k2_ref, work, emit):
    s1_scr, s2_scr, t1_scr, i1_scr, t2_scr, i2_scr, cand_scr, cidx_scr, sc_scr, e_scr = work
    s1_scr[...] = _mm_nt(k1_ref[...], q1)
    s2_scr[...] = _mm_nt(k2_ref[...], q2)
    yield
    for _ in zip(_top16_steps(s1_scr, None, t1_scr, i1_scr), _top16_steps(s2_scr, None, t2_scr, i2_scr)):
        yield
    cand_scr[0:PEER_TOPK, :] = t1_scr[0:1, :] + t2_scr[...]
    cidx_scr[0:PEER_TOPK, :] = i1_scr[0:1, :] * N_KEYS + i2_scr[...]
    t2 = t2_scr[0:SUBLANES, :]
    i2 = i2_scr[0:SUBLANES, :]
    sub_iota = lax.broadcasted_iota(jnp.int32, (SUBLANES, LANES), 0)
    for a in range(1, SUBLANES):
        lo = PEER_TOPK + (a - 1) * SUBLANES
        cand_scr[lo:lo + SUBLANES, :] = jnp.where(sub_iota < PEER_TOPK // (a + 1), t1_scr[a:a + 1, :] + t2, -jnp.inf)
        cidx_scr[lo:lo + SUBLANES, :] = i1_scr[a:a + 1, :] * N_KEYS + i2
    cand_scr[CAND_ROWS - SUBLANES:CAND_ROWS, :] = t1_scr[SUBLANES:PEER_TOPK, :] + t2_scr[0:1, :]
    cidx_scr[CAND_ROWS - SUBLANES:CAND_ROWS, :] = i1_scr[SUBLANES:PEER_TOPK, :] * N_KEYS + i2_scr[0:1, :]
    yield
    yield from _top16_steps(cand_scr, cidx_scr, sc_scr, e_scr)
    sc = sc_scr[...]
    p = jnp.exp(sc - sc[0:1, :])
    emit(e_scr[...], p / jnp.sum(p, axis=0, keepdims=True))


def _qproj_kernel(h_ref, g_ref, wq_ref, q_ref):
    c = _rms(h_ref[...], g_ref[...]).astype(BF16)
    q = jnp.dot(c, wq_ref[...], preferred_element_type=F32).astype(BF16)
    for sub in range(q_ref.shape[0]):
        for j in range(2 * PEER_HEADS):
            q_ref[sub, j] = q[sub * LANES:(sub + 1) * LANES, j * N_KEYS:(j + 1) * N_KEYS]


def _qproj(h, g, wq, tm=512):
    t = h.shape[0]
    subs = tm // LANES
    return pl.pallas_call(
        _qproj_kernel,
        grid=(t // tm,),
        in_specs=[
            pl.BlockSpec((tm, D_MODEL), lambda i: (i, 0)),
            pl.BlockSpec((1, D_MODEL), lambda i: (0, 0)),
            pl.BlockSpec((D_MODEL, D_MODEL), lambda i: (0, 0)),
        ],
        out_specs=pl.BlockSpec((subs, 2 * PEER_HEADS, LANES, N_KEYS), lambda i: (i, 0, 0, 0)),
        out_shape=jax.ShapeDtypeStruct((t // LANES, 2 * PEER_HEADS, LANES, N_KEYS), BF16),
        compiler_params=_params(("arbitrary",)),
        name="qproj",
    )(h, g, wq)


PEER_TB = 512
PEER_SLOTS = 8
PEER_LEAD = PEER_SLOTS - 1
PEER_SUBS = PEER_TB // LANES
ROUTE_UNITS = PEER_HEADS * PEER_SUBS
PEER_GROUP = PEER_TB // ROUTE_UNITS
assert PEER_GROUP % PEER_SLOTS == 0
ROUTE_STEPS = 2 * PEER_TOPK + 3
K_TILE = 256
SIDE_AFTER = (4, 10, 15)
assert len(SIDE_AFTER) * PEER_GROUP >= ROUTE_STEPS
HALF_D = D_MODEL // 2
PACK_TILES = HALF_D // LANES
INV_SQRT2 = 0.7071067811865476


def _pack_bf16_pairs(a):
    bits = lax.bitcast_convert_type(a.astype(BF16), jnp.uint16).astype(jnp.uint32)
    return bits[:, :HALF_D] | (bits[:, HALF_D:] << 16)


def _peer_kernel(hn_ref, wq_ref, k1_ref, k2_ref, h_ref, g_ref, uv_hbm, o_ref,
                 idx_s, idx_v, gate_scr, et_scr, gt_scr, c_scr, cn_scr, q_scr, acc_scr, *scratch):
    work = scratch[:len(ROUTE_SCRATCH)]
    rows = scratch[len(ROUTE_SCRATCH):len(ROUTE_SCRATCH) + PEER_SLOTS]
    idx_sem, row_sems = scratch[len(ROUTE_SCRATCH) + PEER_SLOTS:]
    tb = h_ref.shape[0]

    def route(q_ref, u):
        if isinstance(u, int):
            sub, hd = u % PEER_SUBS, u // PEER_SUBS
            lo = hd * PEER_TOPK
        else:
            sub = jnp.bitwise_and(u, PEER_SUBS - 1)
            hd = jnp.right_shift(u, PEER_SUBS.bit_length() - 1)
            lo = pl.multiple_of(hd * PEER_TOPK, PEER_TOPK)

        def emit(e, gate):
            et_scr[sub, pl.ds(lo, PEER_TOPK), :] = e
            gt_scr[sub, pl.ds(lo, PEER_TOPK), :] = gate

        return _route_unit_steps(q_ref[sub, 2 * hd], q_ref[sub, 2 * hd + 1], k1_ref, k2_ref, work, emit)

    def finish(steps):
        for _ in steps:
            pass

    def query_block(j):
        for sub in range(PEER_SUBS):
            q_scr[sub, j] = jnp.dot(cn_scr[sub * LANES:(sub + 1) * LANES, :], wq_ref[j],
                                    preferred_element_type=F32).astype(BF16)

    def query_steps(i):
        if isinstance(i, int):
            hd, jb, half = min(i // PEER_SUBS + 1, PEER_HEADS - 1), (i % PEER_SUBS) // 2, i % 2
            r0 = half * 2 * LANES
        else:
            hd = jnp.minimum(jnp.right_shift(i, PEER_SUBS.bit_length() - 1) + 1, PEER_HEADS - 1)
            jb = jnp.right_shift(jnp.bitwise_and(i, PEER_SUBS - 1), 1)
            half = jnp.bitwise_and(i, 1)
            r0 = pl.multiple_of(half * 2 * LANES, 2 * LANES)
        j = 2 * hd + jb
        for kt in range(D_MODEL // K_TILE):
            part = jnp.dot(cn_scr[pl.ds(r0, 2 * LANES), kt * K_TILE:(kt + 1) * K_TILE],
                           wq_ref[j, kt * K_TILE:(kt + 1) * K_TILE, :], preferred_element_type=F32)
            acc_scr[...] = part if kt == 0 else acc_scr[...] + part
            yield
        q_scr[2 * half, j] = acc_scr[0:LANES, :].astype(BF16)
        q_scr[2 * half + 1, j] = acc_scr[LANES:2 * LANES, :].astype(BF16)
        yield

    def merged(main, extra, every):
        n = 0
        for _ in main:
            n += 1
            if n % every == 0:
                next(extra, None)
            yield
        for _ in extra:
            yield

    ids_copy = pltpu.make_async_copy(idx_v, idx_s, idx_sem)

    def publish():
        for sub in range(PEER_SUBS):
            idx_v[sub * LANES:(sub + 1) * LANES, :] = et_scr[sub].T
            gate_scr[sub * LANES:(sub + 1) * LANES, :] = gt_scr[sub].T
        ids_copy.start()

    @pl.when(pl.program_id(0) == 0)
    def _():
        cn_scr[...] = _rms(h_ref[...], g_ref[...]).astype(BF16)

        def qbody(j, carry):
            query_block(j)
            return carry

        lax.fori_loop(0, 2 * PEER_HEADS, qbody, 0)

        def body(u, carry):
            finish(route(q_scr, u))
            return carry

        lax.fori_loop(0, ROUTE_UNITS, body, 0)
        publish()

    c_scr[...] = _rms(h_ref[...], g_ref[...])
    cn_scr[...] = _rms(hn_ref[...], g_ref[...]).astype(BF16)
    query_block(0)
    query_block(1)
    ids_copy.wait()

    def issue(t, slot, k0, k1):
        for k in range(k0, k1):
            e = idx_s[t, k]
            pltpu.make_async_copy(uv_hbm.at[e], rows[slot].at[pl.ds(k, 1), :], row_sems.at[slot]).start(priority=k % 2)

    def wait(slot):
        pltpu.make_async_copy(uv_hbm.at[pl.ds(0, PEER_K), 0], rows[slot], row_sems.at[slot]).wait()

    def unpack(words):
        lo = lax.bitcast_convert_type(jnp.left_shift(words, jnp.uint32(16)), F32)
        hi = lax.bitcast_convert_type(jnp.bitwise_and(words, jnp.uint32(0xFFFF0000)), F32)
        return lo, hi

    eye = lax.broadcasted_iota(jnp.int32, (PEER_K, LANES), 0) == lax.broadcasted_iota(jnp.int32, (PEER_K, LANES), 1)
    per_piece = PEER_K // (2 * PACK_TILES)

    def token(t, slot, prefetch, side_steps):
        wait(slot)
        ahead_slot = (slot + PEER_LEAD) % PEER_SLOTS
        x = c_scr[pl.ds(t, 1), :]
        acc = None
        for j in range(PACK_TILES):
            if prefetch:
                issue(t + PEER_LEAD, ahead_slot, j * per_piece, (j + 1) * per_piece)
            lo, hi = unpack(rows[slot][:, j * LANES:(j + 1) * LANES])
            p = lo * x[:, j * LANES:(j + 1) * LANES] + hi * x[:, HALF_D + j * LANES:HALF_D + (j + 1) * LANES]
            acc = p if acc is None else acc + p
            if j in SIDE_AFTER:
                next(side_steps, None)
        act = jnp.sum(acc, axis=1, keepdims=True)
        gate_col = jnp.sum(jnp.where(eye, gate_scr[pl.ds(t, 1), :], 0.0), axis=1, keepdims=True)
        coef = gate_col * (0.5 * act * (1.0 + lax.erf(act * INV_SQRT2)))
        mixed_lo, mixed_hi = [], []
        for j in range(PACK_TILES):
            if prefetch:
                issue(t + PEER_LEAD, ahead_slot, (PACK_TILES + j) * per_piece, (PACK_TILES + j + 1) * per_piece)
            lo, hi = unpack(rows[slot][:, HALF_D + j * LANES:HALF_D + (j + 1) * LANES])
            mixed_lo.append(jnp.sum(lo * coef, axis=0, keepdims=True))
            mixed_hi.append(jnp.sum(hi * coef, axis=0, keepdims=True))
            if PACK_TILES + j in SIDE_AFTER:
                next(side_steps, None)
        o_ref[pl.ds(t, 1), :] = h_ref[pl.ds(t, 1), :] + jnp.concatenate(mixed_lo + mixed_hi, axis=1)

    for s in range(PEER_LEAD):
        issue(s, s, 0, PEER_K)

    def group(i, carry):
        steps = merged(route(q_scr, i), query_steps(i), 3)
        for s in range(PEER_GROUP):
            token(i * PEER_GROUP + s, s % PEER_SLOTS, True, steps)
        finish(steps)
        return carry

    lax.fori_loop(0, ROUTE_UNITS - 1, group, 0)
    last = tb - PEER_GROUP
    steps = merged(route(q_scr, ROUTE_UNITS - 1), query_steps(ROUTE_UNITS - 1), 3)
    for s in range(PEER_GROUP):
        token(last + s, s % PEER_SLOTS, s + PEER_LEAD < PEER_GROUP, steps)
    finish(steps)
    pl.when(pl.program_id(0) + 1 < pl.num_programs(0))(publish)


def _peer(wq3, k1, k2, h, g, uv):
    t = h.shape[0]
    tb = PEER_TB
    steps = t // tb
    return pl.pallas_call(
        _peer_kernel,
        grid=(steps,),
        in_specs=[
            pl.BlockSpec((tb, D_MODEL), lambda i: (jnp.minimum(i + 1, steps - 1), 0)),
            pl.BlockSpec((2 * PEER_HEADS, D_MODEL, N_KEYS), lambda i: (0, 0, 0), pipeline_mode=pl.Buffered(1)),
            pl.BlockSpec((N_KEYS, N_KEYS), lambda i: (0, 0)),
            pl.BlockSpec((N_KEYS, N_KEYS), lambda i: (0, 0)),
            pl.BlockSpec((tb, D_MODEL), lambda i: (i, 0)),
            pl.BlockSpec((1, D_MODEL), lambda i: (0, 0)),
            pl.BlockSpec(memory_space=pl.ANY),
        ],
        out_specs=pl.BlockSpec((tb, D_MODEL), lambda i: (i, 0)),
        out_shape=jax.ShapeDtypeStruct((t, D_MODEL), F32),
        scratch_shapes=[
            pltpu.SMEM((tb, PEER_K), jnp.int32),
            pltpu.VMEM((tb, PEER_K), jnp.int32),
            pltpu.VMEM((tb, PEER_K), F32),
            pltpu.VMEM((PEER_SUBS, PEER_K, LANES), jnp.int32),
            pltpu.VMEM((PEER_SUBS, PEER_K, LANES), F32),
            pltpu.VMEM((tb, D_MODEL), F32),
            pltpu.VMEM((tb, D_MODEL), BF16),
            pltpu.VMEM((PEER_SUBS, 2 * PEER_HEADS, LANES, N_KEYS), BF16),
            pltpu.VMEM((2 * LANES, N_KEYS), F32),
        ] + ROUTE_SCRATCH + [pltpu.VMEM((PEER_K, D_MODEL), jnp.uint32)] * PEER_SLOTS + [
            pltpu.SemaphoreType.DMA(()),
            pltpu.SemaphoreType.DMA((PEER_SLOTS,)),
        ],
        compiler_params=_params(("arbitrary",)),
        name="peer",
    )(h, wq3, k1, k2, h, g, uv)


def _ple_kernel(h_ref, p_ref, gple_ref, wg_ref, wp_ref, gfin_ref, y_ref):
    h = h_ref[...]
    e = _rms(h, gple_ref[...]).astype(BF16)
    gate = _sigmoid(jnp.dot(e, wg_ref[...], preferred_element_type=F32))
    proj = jnp.dot(p_ref[...].astype(BF16), wp_ref[...], preferred_element_type=F32)
    y_ref[...] = _rms(h + gate * proj, gfin_ref[...])


def _ple(h, p, g_ple, wg, wp, g_fin, tm=512):
    t = h.shape[0]
    return pl.pallas_call(
        _ple_kernel,
        grid=(t // tm,),
        in_specs=[
            pl.BlockSpec((tm, D_MODEL), lambda i: (i, 0)),
            pl.BlockSpec((tm, PLE_DIM), lambda i: (i, 0)),
            pl.BlockSpec((1, D_MODEL), lambda i: (0, 0)),
            pl.BlockSpec((D_MODEL, D_MODEL), lambda i: (0, 0)),
            pl.BlockSpec((PLE_DIM, D_MODEL), lambda i: (0, 0)),
            pl.BlockSpec((1, D_MODEL), lambda i: (0, 0)),
        ],
        out_specs=pl.BlockSpec((tm, D_MODEL), lambda i: (i, 0)),
        out_shape=jax.ShapeDtypeStruct((t, D_MODEL), F32),
        compiler_params=_params(("arbitrary",)),
        name="ple",
    )(h, p, g_ple, wg, wp, g_fin)


def _rope_tables(pos):
    half = RET_D // 2
    inv = ROPE_BASE ** (-jnp.arange(half, dtype=F32) / half)
    ang = pos[:, None] * inv[None, :]
    cos, sin = jnp.cos(ang), jnp.sin(ang)
    return jnp.concatenate([cos, cos], axis=-1), jnp.concatenate([-sin, sin], axis=-1)


def _stream(x, p, pos, s0, c0, n0, m0, w):
    b, l, _ = x.shape
    t = b * l
    x2 = x.reshape(t, D_MODEL)
    z, gz = _in_proj(x2, w["g_mix"], w["w_in"], w["w_gates"])
    cosf, sinf = _rope_tables(pos)
    m0b = jnp.broadcast_to(m0[:, :, None], (b, MLSTM_HEADS, LANES))
    mix, s_new, c_new, n_new, m_new = _mixers(
        z.reshape(b, l, Z_COLS), gz.reshape(b, l, LANES), w["bias"], cosf, sinf, s0, c0, n0, m0b, w["g_ret"], w["g_ml"])
    h1 = _out_proj(mix.reshape(t, D_MODEL), w["w_out"], x2)
    h2 = _peer(w["w_q"], w["k1"], w["k2"], h1, w["g_ffn"], w["uv"])
    y = _ple(h2, p.reshape(t, PLE_DIM), w["g_ple"], w["w_ple_gate"], w["w_ple_proj"], w["g_final"])
    return y.reshape(b, l, D_MODEL), s_new[None], c_new[None], n_new[None], m_new[None, :, :, 0]


def _prep_weights(g_mix, w_in, b_gates, g_ret, g_mlstm, w_out, g_ffn, w_peer_q, peer_keys1, peer_keys2, peer_u, peer_v,
                  g_ple, w_ple_gate, w_ple_proj, g_final):
    w_in0 = w_in[0]
    return {
        "g_mix": g_mix,
        "w_in": w_in0[:, :Z_COLS].astype(BF16),
        "w_gates": jnp.pad(w_in0[:, Z_COLS:], ((0, 0), (0, LANES - 2 * MLSTM_HEADS))).astype(BF16),
        "bias": jnp.pad(b_gates, ((0, 0), (0, LANES - 2 * MLSTM_HEADS))),
        "g_ret": g_ret,
        "g_ml": g_mlstm,
        "w_out": w_out[0].astype(BF16),
        "g_ffn": g_ffn,
        "w_q": w_peer_q[0].astype(BF16).reshape(D_MODEL, 2 * PEER_HEADS, N_KEYS).transpose(1, 0, 2),
        "k1": peer_keys1[0].astype(BF16),
        "k2": peer_keys2[0].astype(BF16),
        "uv": jnp.concatenate([_pack_bf16_pairs(peer_u[0]), _pack_bf16_pairs(peer_v[0])], axis=1)[:, None, :],
        "g_ple": g_ple,
        "w_ple_gate": w_ple_gate[0].astype(BF16),
        "w_ple_proj": w_ple_proj[0].astype(BF16),
        "g_final": g_final[None, :],
    }


def kernel(x_prompt, x_sample, p_prompt, p_sample, state_ret, state_mlstm_C, state_mlstm_n, state_mlstm_m, g_mix, w_in, b_gates, g_ret, g_mlstm, w_out, g_ffn, w_peer_q, peer_keys1, peer_keys2, peer_u, peer_v, g_ple, w_ple_gate, w_ple_proj, g_final):
    w = _prep_weights(g_mix, w_in, b_gates, g_ret, g_mlstm, w_out, g_ffn, w_peer_q, peer_keys1, peer_keys2, peer_u, peer_v,
                      g_ple, w_ple_gate, w_ple_proj, g_final)
    bp, lp, _ = x_prompt.shape
    bs, ls, _ = x_sample.shape
    zeros = lambda *shape: jnp.zeros(shape, F32)
    y_s, ret_s, c_s, n_s, m_s = _stream(
        x_sample, p_sample[0], PAST_LEN + jnp.arange(ls, dtype=F32),
        state_ret[0], state_mlstm_C[0], state_mlstm_n[0], state_mlstm_m[0], w)
    y_p, ret_p, c_p, n_p, m_p = _stream(
        x_prompt, p_prompt[0], jnp.arange(lp, dtype=F32),
        zeros(bp, RET_HEADS, RET_D, RET_D), zeros(bp, MLSTM_HEADS, MLSTM_DK, MLSTM_DV),
        zeros(bp, MLSTM_HEADS, MLSTM_DK), zeros(bp, MLSTM_HEADS), w)
    return (y_p, y_s, ret_p, c_p, n_p, m_p, ret_s, c_s, n_s, m_s)
```

```python
import math

import numpy as np
import jax
import jax.numpy as jnp
from jax import lax
from jax.experimental import pallas as pl
from jax.experimental.pallas import tpu as pltpu

F32 = jnp.float32
BF16 = jnp.bfloat16

D_MODEL = 2048
CHUNK = 64
PAST_LEN = 1024
RMS_EPS = 1e-6
ROPE_BASE = 10000.0
RET_HEADS = 8
RET_D = 128
MLSTM_HEADS = 4
MLSTM_DK = 128
MLSTM_DV = 256
Z_COLS = 7168
OFF_RQ, OFF_RK, OFF_RV, OFF_RG = 0, 1024, 2048, 3072
OFF_MQ, OFF_MK, OFF_MV, OFF_MO = 4096, 4608, 5120, 6144
PEER_HEADS = 8
N_KEYS = 128
PEER_TOPK = 16
PEER_K = PEER_HEADS * PEER_TOPK
PLE_DIM = 256
LANES = 128
SUBLANES = 8
CAND_ROWS = PEER_TOPK + (SUBLANES - 1) * SUBLANES + SUBLANES

LOG_GAMMA = [float(np.log(np.float32(1.0) - np.float32(2.0) ** np.float32(-5.0 - h))) for h in range(RET_HEADS)]

VMEM_LIMIT = 56 * 1024 * 1024


def _params(sem):
    return pltpu.CompilerParams(dimension_semantics=sem, vmem_limit_bytes=VMEM_LIMIT)


def _rms(x, g):
    return x * lax.rsqrt(jnp.mean(x * x, axis=-1, keepdims=True) + RMS_EPS) * g


def _mm(a, b):
    return jnp.dot(a.astype(BF16), b.astype(BF16), preferred_element_type=F32)


def _mm_nt(a, b):
    return lax.dot_general(a.astype(BF16), b.astype(BF16), (((1,), (1,)), ((), ())), preferred_element_type=F32)


def _sigmoid(x):
    return 1.0 / (1.0 + jnp.exp(-x))


def _in_proj_kernel(x_ref, g_ref, w_ref, wg_ref, z_ref, gz_ref, a_scr):
    @pl.when(pl.program_id(1) == 0)
    def _():
        a_scr[...] = _rms(x_ref[...], g_ref[...]).astype(BF16)
        gz_ref[...] = jnp.dot(a_scr[...], wg_ref[...], preferred_element_type=F32)

    z_ref[...] = jnp.dot(a_scr[...], w_ref[...], preferred_element_type=F32)


def _in_proj(x, g, w, wg, tm=1024, tn=1024):
    t = x.shape[0]
    tm = min(tm, t)
    return pl.pallas_call(
        _in_proj_kernel,
        grid=(t // tm, Z_COLS // tn),
        in_specs=[
            pl.BlockSpec((tm, D_MODEL), lambda i, j: (i, 0)),
            pl.BlockSpec((1, D_MODEL), lambda i, j: (0, 0)),
            pl.BlockSpec((D_MODEL, tn), lambda i, j: (0, j)),
            pl.BlockSpec((D_MODEL, LANES), lambda i, j: (0, 0)),
        ],
        out_specs=[
            pl.BlockSpec((tm, tn), lambda i, j: (i, j)),
            pl.BlockSpec((tm, LANES), lambda i, j: (i, 0)),
        ],
        out_shape=[jax.ShapeDtypeStruct((t, Z_COLS), F32), jax.ShapeDtypeStruct((t, LANES), F32)],
        scratch_shapes=[pltpu.VMEM((tm, D_MODEL), BF16)],
        compiler_params=_params(("arbitrary", "arbitrary")),
        name="in_proj",
    )(x, g, w, wg)


def _log_sigmoid(x):
    return -(jnp.maximum(-x, 0.0) + jnp.log1p(jnp.exp(-jnp.abs(x))))


def _mixers_kernel(z_ref, gz_ref, bias_ref, cos_ref, sin_ref, s0_ref, c0_ref, n0_ref, m0_ref, gret_ref, gml_ref,
                   mix_ref, s_ref, c_ref, n_ref, m_ref):
    @pl.when(pl.program_id(1) == 0)
    def _():
        s_ref[...] = s0_ref[...]
        c_ref[...] = c0_ref[...]
        n_ref[...] = n0_ref[...]
        m_ref[...] = m0_ref[...]

    cl = CHUNK
    row = lax.broadcasted_iota(jnp.int32, (cl, cl), 0)
    col = lax.broadcasted_iota(jnp.int32, (cl, cl), 1)
    causal = row >= col
    diff = jnp.where(causal, (row - col).astype(F32), 0.0)
    posc = lax.broadcasted_iota(jnp.int32, (cl, 1), 0).astype(F32)
    cosf = cos_ref[...]
    sinf = sin_ref[...]

    def rope(x):
        return x * cosf + pltpu.roll(x, RET_D // 2, axis=1) * sinf

    ret = []
    for h in range(RET_HEADS):
        lg = LOG_GAMMA[h]
        lo = h * RET_D
        q = rope(z_ref[0, :, OFF_RQ + lo:OFF_RQ + lo + RET_D])
        k = rope(z_ref[0, :, OFF_RK + lo:OFF_RK + lo + RET_D]) * (RET_D ** -0.5)
        q_dec = (q * jnp.exp((posc + 1.0) * lg)).astype(BF16)
        k_dec_t = (k * jnp.exp((cl - 1.0 - posc) * lg)).T.astype(BF16)
        ret.append((q.astype(BF16), k.astype(BF16), q_dec, k_dec_t))

    gates = gz_ref[0] + bias_ref[...]
    tri = causal.astype(F32)
    bcum = jnp.dot(tri, _log_sigmoid(gates), preferred_element_type=F32, precision=lax.Precision.HIGHEST)
    bcum_t = bcum.T
    gates_t = gates.T
    mls = []
    for h in range(MLSTM_HEADS):
        k = z_ref[0, :, OFF_MK + h * MLSTM_DK:OFF_MK + (h + 1) * MLSTM_DK] * (MLSTM_DK ** -0.5)
        f = MLSTM_HEADS + h
        b_col = bcum[:, f:f + 1]
        b_row = bcum_t[f:f + 1, :]
        ig_col = gates[:, h:h + 1]
        ig_row = gates_t[h:h + 1, :]
        m_prev = m_ref[0, h:h + 1, 0:1]
        dlog = jnp.where(causal, b_col - b_row + ig_row, -jnp.inf)
        inter_log = b_col + m_prev
        m_t = jnp.maximum(inter_log, jnp.max(dlog, axis=-1, keepdims=True))
        dw = jnp.exp(dlog - m_t)
        inter_w = jnp.exp(inter_log - m_t)
        m_new = m_t[cl - 1:cl, :]
        b_last = b_col[cl - 1:cl, :]
        ws = jnp.exp(b_last - b_col + ig_col - m_new)
        carry = jnp.exp(b_last + m_prev - m_new)
        kw = k * ws
        n_old = n_ref[0, h:h + 1, :]
        n_ref[0, h:h + 1, :] = carry * n_old + jnp.sum(kw, axis=0, keepdims=True)
        m_ref[0, h:h + 1, :] = jnp.broadcast_to(m_new, (1, LANES))
        mls.append((k.astype(BF16), kw.T.astype(BF16), dw, inter_w, m_t, carry, n_old))

    for h in range(RET_HEADS):
        lg = LOG_GAMMA[h]
        lo = h * RET_D
        q, k, q_dec, k_dec_t = ret[h]
        v = z_ref[0, :, OFF_RV + lo:OFF_RV + lo + RET_D].astype(BF16)
        rg = z_ref[0, :, OFF_RG + lo:OFF_RG + lo + RET_D]
        decay = jnp.where(causal, jnp.exp(diff * lg), 0.0)
        s_old = s_ref[0, h]
        scores = _mm_nt(q, k) * decay
        o = _mm(scores, v) + _mm(q_dec, s_old)
        s_ref[0, h] = math.exp(cl * lg) * s_old + _mm(k_dec_t, v)
        y = o * lax.rsqrt(jnp.mean(o * o, axis=-1, keepdims=True) + RMS_EPS) * gret_ref[:, lo:lo + RET_D]
        mix_ref[0, :, lo:lo + RET_D] = (y * (rg * _sigmoid(rg))).astype(BF16)

    for h in range(MLSTM_HEADS):
        k, kw_t, dw, inter_w, m_t, carry, n_old = mls[h]
        q = z_ref[0, :, OFF_MQ + h * MLSTM_DK:OFF_MQ + (h + 1) * MLSTM_DK]
        v = z_ref[0, :, OFF_MV + h * MLSTM_DV:OFF_MV + (h + 1) * MLSTM_DV].astype(BF16)
        mo = z_ref[0, :, OFF_MO + h * MLSTM_DV:OFF_MO + (h + 1) * MLSTM_DV]
        c_old = c_ref[0, h]
        sm = _mm_nt(q, k) * dw
        num = _mm(sm, v) + inter_w * _mm(q, c_old)
        den = jnp.sum(sm, axis=-1, keepdims=True) + inter_w * jnp.sum(q * n_old, axis=-1, keepdims=True)
        hh = num / jnp.maximum(jnp.abs(den), jnp.exp(-m_t))
        c_ref[0, h] = carry * c_old + _mm(kw_t, v)
        y = hh * lax.rsqrt(jnp.mean(hh * hh, axis=-1, keepdims=True) + RMS_EPS) * gml_ref[:, h * MLSTM_DV:(h + 1) * MLSTM_DV]
        lo = RET_HEADS * RET_D + h * MLSTM_DV
        mix_ref[0, :, lo:lo + MLSTM_DV] = (y * _sigmoid(mo)).astype(BF16)


def _mixers(z, gz, bias, cosf, sinf, s0, c0, n0, m0, g_ret, g_ml):
    b, l, _ = z.shape
    per_b = lambda *tail: (lambda i, c: (i,) + tail)
    return pl.pallas_call(
        _mixers_kernel,
        grid=(b, l // CHUNK),
        in_specs=[
            pl.BlockSpec((1, CHUNK, Z_COLS), lambda i, c: (i, c, 0)),
            pl.BlockSpec((1, CHUNK, LANES), lambda i, c: (i, c, 0)),
            pl.BlockSpec((1, LANES), lambda i, c: (0, 0)),
            pl.BlockSpec((CHUNK, RET_D), lambda i, c: (c, 0)),
            pl.BlockSpec((CHUNK, RET_D), lambda i, c: (c, 0)),
            pl.BlockSpec((1, RET_HEADS, RET_D, RET_D), per_b(0, 0, 0)),
            pl.BlockSpec((1, MLSTM_HEADS, MLSTM_DK, MLSTM_DV), per_b(0, 0, 0)),
            pl.BlockSpec((1, MLSTM_HEADS, MLSTM_DK), per_b(0, 0)),
            pl.BlockSpec((1, MLSTM_HEADS, LANES), per_b(0, 0)),
            pl.BlockSpec((1, RET_HEADS * RET_D), lambda i, c: (0, 0)),
            pl.BlockSpec((1, MLSTM_HEADS * MLSTM_DV), lambda i, c: (0, 0)),
        ],
        out_specs=[
            pl.BlockSpec((1, CHUNK, D_MODEL), lambda i, c: (i, c, 0)),
            pl.BlockSpec((1, RET_HEADS, RET_D, RET_D), per_b(0, 0, 0)),
            pl.BlockSpec((1, MLSTM_HEADS, MLSTM_DK, MLSTM_DV), per_b(0, 0, 0)),
            pl.BlockSpec((1, MLSTM_HEADS, MLSTM_DK), per_b(0, 0)),
            pl.BlockSpec((1, MLSTM_HEADS, LANES), per_b(0, 0)),
        ],
        out_shape=[
            jax.ShapeDtypeStruct((b, l, D_MODEL), BF16),
            jax.ShapeDtypeStruct((b, RET_HEADS, RET_D, RET_D), F32),
            jax.ShapeDtypeStruct((b, MLSTM_HEADS, MLSTM_DK, MLSTM_DV), F32),
            jax.ShapeDtypeStruct((b, MLSTM_HEADS, MLSTM_DK), F32),
            jax.ShapeDtypeStruct((b, MLSTM_HEADS, LANES), F32),
        ],
        compiler_params=_params(("arbitrary", "arbitrary")),
        name="mixers",
    )(z, gz, bias, cosf, sinf, s0, c0, n0, m0, g_ret, g_ml)


def _out_proj_kernel(a_ref, w_ref, r_ref, o_ref):
    o_ref[...] = r_ref[...] + jnp.dot(a_ref[...], w_ref[...], preferred_element_type=F32)


def _out_proj(a, w, r, tm=512):
    t = a.shape[0]
    return pl.pallas_call(
        _out_proj_kernel,
        grid=(t // tm,),
        in_specs=[
            pl.BlockSpec((tm, D_MODEL), lambda i: (i, 0)),
            pl.BlockSpec((D_MODEL, D_MODEL), lambda i: (0, 0)),
            pl.BlockSpec((tm, D_MODEL), lambda i: (i, 0)),
        ],
        out_specs=pl.BlockSpec((tm, D_MODEL), lambda i: (i, 0)),
        out_shape=jax.ShapeDtypeStruct((t, D_MODEL), F32),
        compiler_params=_params(("arbitrary",)),
        name="out_proj",
    )(a, w, r)


def _top16_steps(x_ref, payload_ref, val_ref, idx_ref):
    rows = x_ref.shape[0]
    for r in range(PEER_TOPK):
        x = x_ref[...]
        iota = lax.broadcasted_iota(jnp.int32, x.shape, 0)
        m = jnp.max(x, axis=0, keepdims=True)
        am = jnp.min(jnp.where(x == m, iota, rows), axis=0, keepdims=True)
        sel = iota == am
        val_ref[r:r + 1, :] = m
        if payload_ref is None:
            idx_ref[r:r + 1, :] = am
        else:
            idx_ref[r:r + 1, :] = jnp.max(jnp.where(sel, payload_ref[...], -1), axis=0, keepdims=True)
        x_ref[...] = jnp.where(sel, -jnp.inf, x)
        yield


ROUTE_SCRATCH = [
    pltpu.VMEM((N_KEYS, LANES), F32),
    pltpu.VMEM((N_KEYS, LANES), F32),
    pltpu.VMEM((PEER_TOPK, LANES), F32),
    pltpu.VMEM((PEER_TOPK, LANES), jnp.int32),
    pltpu.VMEM((PEER_TOPK, LANES), F32),
    pltpu.VMEM((PEER_TOPK, LANES), jnp.int32),
    pltpu.VMEM((CAND_ROWS, LANES), F32),
    pltpu.VMEM((CAND_ROWS, LANES), jnp.int32),
    pltpu.VMEM((PEER_TOPK, LANES), F32),
    pltpu.VMEM((PEER_TOPK, LANES), jnp.int32),
]


def _route_unit_steps(q1, q2, k1_ref, k2_ref, work, emit):
    s1_scr, s2_scr, t1_scr, i1_scr, t2_scr, i2_scr, cand_scr, cidx_scr, sc_scr, e_scr = work
    s1_scr[...] = _mm_nt(k1_ref[...], q1)
    s2_scr[...] = _mm_nt(k2_ref[...], q2)
    yield
    for _ in zip(_top16_steps(s1_scr, None, t1_scr, i1_scr), _top16_steps(s2_scr, None, t2_scr, i2_scr)):
        yield
    cand_scr[0:PEER_TOPK, :] = t1_scr[0:1, :] + t2_scr[...]
    cidx_scr[0:PEER_TOPK, :] = i1_scr[0:1, :] * N_KEYS + i2_scr[...]
    t2 = t2_scr[0:SUBLANES, :]
    i2 = i2_scr[0:SUBLANES, :]
    sub_iota = lax.broadcasted_iota(jnp.int32, (SUBLANES, LANES), 0)
    for a in range(1, SUBLANES):
        lo = PEER_TOPK + (a - 1) * SUBLANES
        cand_scr[lo:lo + SUBLANES, :] = jnp.where(sub_iota < PEER_TOPK // (a + 1), t1_scr[a:a + 1, :] + t2, -jnp.inf)
        cidx_scr[lo:lo + SUBLANES, :] = i1_scr[a:a + 1, :] * N_KEYS + i2
    cand_scr[CAND_ROWS - SUBLANES:CAND_ROWS, :] = t1_scr[SUBLANES:PEER_TOPK, :] + t2_scr[0:1, :]
    cidx_scr[CAND_ROWS - SUBLANES:CAND_ROWS, :] = i1_scr[SUBLANES:PEER_TOPK, :] * N_KEYS + i2_scr[0:1, :]
    yield
    yield from _top16_steps(cand_scr, cidx_scr, sc_scr, e_scr)
    sc = sc_scr[...]
    p = jnp.exp(sc - sc[0:1, :])
    emit(e_scr[...], p / jnp.sum(p, axis=0, keepdims=True))


PEER_TB = 512
PEER_SLOTS = 8
PEER_LEAD = PEER_SLOTS - 1
PEER_SUBS = PEER_TB // LANES
ROUTE_UNITS = PEER_HEADS * PEER_SUBS
PEER_GROUP = PEER_TB // ROUTE_UNITS
assert PEER_GROUP % PEER_SLOTS == 0
ROUTE_STEPS = 2 * PEER_TOPK + 3
K_TILE = 256
SIDE_AFTER = (4, 10, 15)
assert len(SIDE_AFTER) * PEER_GROUP >= ROUTE_STEPS
HALF_D = D_MODEL // 2
PACK_TILES = HALF_D // LANES
INV_SQRT2 = 0.7071067811865476


def _pack_bf16_pairs(a):
    bits = lax.bitcast_convert_type(a.astype(BF16), jnp.uint16).astype(jnp.uint32)
    return bits[:, :HALF_D] | (bits[:, HALF_D:] << 16)


def _peer_kernel(hn_ref, wq_ref, k1_ref, k2_ref, h_ref, g_ref, uv_hbm, o_ref,
                 idx_s, idx_v, gate_scr, et_scr, gt_scr, c_scr, cn_scr, q_scr, acc_scr, *scratch):
    work = scratch[:len(ROUTE_SCRATCH)]
    rows = scratch[len(ROUTE_SCRATCH):len(ROUTE_SCRATCH) + PEER_SLOTS]
    idx_sem, row_sems = scratch[len(ROUTE_SCRATCH) + PEER_SLOTS:]
    tb = h_ref.shape[0]

    def route(q_ref, u):
        if isinstance(u, int):
            sub, hd = u % PEER_SUBS, u // PEER_SUBS
            lo = hd * PEER_TOPK
        else:
            sub = jnp.bitwise_and(u, PEER_SUBS - 1)
            hd = jnp.right_shift(u, PEER_SUBS.bit_length() - 1)
            lo = pl.multiple_of(hd * PEER_TOPK, PEER_TOPK)

        def emit(e, gate):
            et_scr[sub, pl.ds(lo, PEER_TOPK), :] = e
            gt_scr[sub, pl.ds(lo, PEER_TOPK), :] = gate

        return _route_unit_steps(q_ref[sub, 2 * hd], q_ref[sub, 2 * hd + 1], k1_ref, k2_ref, work, emit)

    def finish(steps):
        for _ in steps:
            pass

    def query_block(j):
        for sub in range(PEER_SUBS):
            q_scr[sub, j] = jnp.dot(cn_scr[sub * LANES:(sub + 1) * LANES, :], wq_ref[j],
                                    preferred_element_type=F32).astype(BF16)

    def query_steps(i):
        if isinstance(i, int):
            hd, jb, half = min(i // PEER_SUBS + 1, PEER_HEADS - 1), (i % PEER_SUBS) // 2, i % 2
            r0 = half * 2 * LANES
        else:
            hd = jnp.minimum(jnp.right_shift(i, PEER_SUBS.bit_length() - 1) + 1, PEER_HEADS - 1)
            jb = jnp.right_shift(jnp.bitwise_and(i, PEER_SUBS - 1), 1)
            half = jnp.bitwise_and(i, 1)
            r0 = pl.multiple_of(half * 2 * LANES, 2 * LANES)
        j = 2 * hd + jb
        for kt in range(D_MODEL // K_TILE):
            part = jnp.dot(cn_scr[pl.ds(r0, 2 * LANES), kt * K_TILE:(kt + 1) * K_TILE],
                           wq_ref[j, kt * K_TILE:(kt + 1) * K_TILE, :], preferred_element_type=F32)
            acc_scr[...] = part if kt == 0 else acc_scr[...] + part
            yield
        q_scr[2 * half, j] = acc_scr[0:LANES, :].astype(BF16)
        q_scr[2 * half + 1, j] = acc_scr[LANES:2 * LANES, :].astype(BF16)
        yield

    def merged(main, extra, every):
        n = 0
        for _ in main:
            n += 1
            if n % every == 0:
                next(extra, None)
            yield
        for _ in extra:
            yield

    ids_copy = pltpu.make_async_copy(idx_v, idx_s, idx_sem)

    def publish():
        for sub in range(PEER_SUBS):
            idx_v[sub * LANES:(sub + 1) * LANES, :] = et_scr[sub].T
            gate_scr[sub * LANES:(sub + 1) * LANES, :] = gt_scr[sub].T
        ids_copy.start()

    @pl.when(pl.program_id(0) == 0)
    def _():
        cn_scr[...] = _rms(h_ref[...], g_ref[...]).astype(BF16)

        def qbody(j, carry):
            query_block(j)
            return carry

        lax.fori_loop(0, 2 * PEER_HEADS, qbody, 0)

        def body(u, carry):
            finish(route(q_scr, u))
            return carry

        lax.fori_loop(0, ROUTE_UNITS, body, 0)
        publish()

    c_scr[...] = _rms(h_ref[...], g_ref[...])
    cn_scr[...] = _rms(hn_ref[...], g_ref[...]).astype(BF16)
    query_block(0)
    query_block(1)
    ids_copy.wait()

    def issue(t, slot, k0, k1):
        for k in range(k0, k1):
            e = idx_s[t, k]
            pltpu.make_async_copy(uv_hbm.at[e], rows[slot].at[pl.ds(k, 1), :], row_sems.at[slot]).start(priority=k % 2)

    def wait(slot):
        pltpu.make_async_copy(uv_hbm.at[pl.ds(0, PEER_K), 0], rows[slot], row_sems.at[slot]).wait()

    def unpack(words):
        lo = lax.bitcast_convert_type(jnp.left_shift(words, jnp.uint32(16)), F32)
        hi = lax.bitcast_convert_type(jnp.bitwise_and(words, jnp.uint32(0xFFFF0000)), F32)
        return lo, hi

    eye = lax.broadcasted_iota(jnp.int32, (PEER_K, LANES), 0) == lax.broadcasted_iota(jnp.int32, (PEER_K, LANES), 1)
    per_piece = PEER_K // (2 * PACK_TILES)

    def token(t, slot, prefetch, side_steps):
        wait(slot)
        ahead_slot = (slot + PEER_LEAD) % PEER_SLOTS
        x = c_scr[pl.ds(t, 1), :]
        acc = None
        for j in range(PACK_TILES):
            if prefetch:
                issue(t + PEER_LEAD, ahead_slot, j * per_piece, (j + 1) * per_piece)
            lo, hi = unpack(rows[slot][:, j * LANES:(j + 1) * LANES])
            p = lo * x[:, j * LANES:(j + 1) * LANES] + hi * x[:, HALF_D + j * LANES:HALF_D + (j + 1) * LANES]
            acc = p if acc is None else acc + p
            if j in SIDE_AFTER:
                next(side_steps, None)
        act = jnp.sum(acc, axis=1, keepdims=True)
        gate_col = jnp.sum(jnp.where(eye, gate_scr[pl.ds(t, 1), :], 0.0), axis=1, keepdims=True)
        coef = gate_col * (0.5 * act * (1.0 + lax.erf(act * INV_SQRT2)))
        mixed_lo, mixed_hi = [], []
        for j in range(PACK_TILES):
            if prefetch:
                issue(t + PEER_LEAD, ahead_slot, (PACK_TILES + j) * per_piece, (PACK_TILES + j + 1) * per_piece)
            lo, hi = unpack(rows[slot][:, HALF_D + j * LANES:HALF_D + (j + 1) * LANES])
            mixed_lo.append(jnp.sum(lo * coef, axis=0, keepdims=True))
            mixed_hi.append(jnp.sum(hi * coef, axis=0, keepdims=True))
            if PACK_TILES + j in SIDE_AFTER:
                next(side_steps, None)
        o_ref[pl.ds(t, 1), :] = h_ref[pl.ds(t, 1), :] + jnp.concatenate(mixed_lo + mixed_hi, axis=1)

    for s in range(PEER_LEAD):
        issue(s, s, 0, PEER_K)

    def group(i, carry):
        steps = merged(route(q_scr, i), query_steps(i), 3)
        for s in range(PEER_GROUP):
            token(i * PEER_GROUP + s, s % PEER_SLOTS, True, steps)
        finish(steps)
        return carry

    lax.fori_loop(0, ROUTE_UNITS - 1, group, 0)
    last = tb - PEER_GROUP
    steps = merged(route(q_scr, ROUTE_UNITS - 1), query_steps(ROUTE_UNITS - 1), 3)
    for s in range(PEER_GROUP):
        token(last + s, s % PEER_SLOTS, s + PEER_LEAD < PEER_GROUP, steps)
    finish(steps)
    pl.when(pl.program_id(0) + 1 < pl.num_programs(0))(publish)


def _peer(wq3, k1, k2, h, g, uv):
    t = h.shape[0]
    tb = PEER_TB
    steps = t // tb
    return pl.pallas_call(
        _peer_kernel,
        grid=(steps,),
        in_specs=[
            pl.BlockSpec((tb, D_MODEL), lambda i: (jnp.minimum(i + 1, steps - 1), 0)),
            pl.BlockSpec((2 * PEER_HEADS, D_MODEL, N_KEYS), lambda i: (0, 0, 0), pipeline_mode=pl.Buffered(1)),
            pl.BlockSpec((N_KEYS, N_KEYS), lambda i: (0, 0)),
            pl.BlockSpec((N_KEYS, N_KEYS), lambda i: (0, 0)),
            pl.BlockSpec((tb, D_MODEL), lambda i: (i, 0)),
            pl.BlockSpec((1, D_MODEL), lambda i: (0, 0)),
            pl.BlockSpec(memory_space=pl.ANY),
        ],
        out_specs=pl.BlockSpec((tb, D_MODEL), lambda i: (i, 0)),
        out_shape=jax.ShapeDtypeStruct((t, D_MODEL), F32),
        scratch_shapes=[
            pltpu.SMEM((tb, PEER_K), jnp.int32),
            pltpu.VMEM((tb, PEER_K), jnp.int32),
            pltpu.VMEM((tb, PEER_K), F32),
            pltpu.VMEM((PEER_SUBS, PEER_K, LANES), jnp.int32),
            pltpu.VMEM((PEER_SUBS, PEER_K, LANES), F32),
            pltpu.VMEM((tb, D_MODEL), F32),
            pltpu.VMEM((tb, D_MODEL), BF16),
            pltpu.VMEM((PEER_SUBS, 2 * PEER_HEADS, LANES, N_KEYS), BF16),
            pltpu.VMEM((2 * LANES, N_KEYS), F32),
        ] + ROUTE_SCRATCH + [pltpu.VMEM((PEER_K, D_MODEL), jnp.uint32)] * PEER_SLOTS + [
            pltpu.SemaphoreType.DMA(()),
            pltpu.SemaphoreType.DMA((PEER_SLOTS,)),
        ],
        compiler_params=_params(("arbitrary",)),
        name="peer",
    )(h, wq3, k1, k2, h, g, uv)


def _ple_kernel(h_ref, p_ref, gple_ref, wg_ref, wp_ref, gfin_ref, y_ref):
    h = h_ref[...]
    e = _rms(h, gple_ref[...]).astype(BF16)
    gate = _sigmoid(jnp.dot(e, wg_ref[...], preferred_element_type=F32))
    proj = jnp.dot(p_ref[...].astype(BF16), wp_ref[...], preferred_element_type=F32)
    y_ref[...] = _rms(h + gate * proj, gfin_ref[...])


def _ple(h, p, g_ple, wg, wp, g_fin, tm=512):
    t = h.shape[0]
    return pl.pallas_call(
        _ple_kernel,
        grid=(t // tm,),
        in_specs=[
            pl.BlockSpec((tm, D_MODEL), lambda i: (i, 0)),
            pl.BlockSpec((tm, PLE_DIM), lambda i: (i, 0)),
            pl.BlockSpec((1, D_MODEL), lambda i: (0, 0)),
            pl.BlockSpec((D_MODEL, D_MODEL), lambda i: (0, 0)),
            pl.BlockSpec((PLE_DIM, D_MODEL), lambda i: (0, 0)),
            pl.BlockSpec((1, D_MODEL), lambda i: (0, 0)),
        ],
        out_specs=pl.BlockSpec((tm, D_MODEL), lambda i: (i, 0)),
        out_shape=jax.ShapeDtypeStruct((t, D_MODEL), F32),
        compiler_params=_params(("arbitrary",)),
        name="ple",
    )(h, p, g_ple, wg, wp, g_fin)


def _rope_tables(pos):
    half = RET_D // 2
    inv = ROPE_BASE ** (-jnp.arange(half, dtype=F32) / half)
    ang = pos[:, None] * inv[None, :]
    cos, sin = jnp.cos(ang), jnp.sin(ang)
    return jnp.concatenate([cos, cos], axis=-1), jnp.concatenate([-sin, sin], axis=-1)


def _stream(x, p, pos, s0, c0, n0, m0, w):
    b, l, _ = x.shape
    t = b * l
    x2 = x.reshape(t, D_MODEL)
    z, gz = _in_proj(x2, w["g_mix"], w["w_in"], w["w_gates"])
    cosf, sinf = _rope_tables(pos)
    m0b = jnp.broadcast_to(m0[:, :, None], (b, MLSTM_HEADS, LANES))
    mix, s_new, c_new, n_new, m_new = _mixers(
        z.reshape(b, l, Z_COLS), gz.reshape(b, l, LANES), w["bias"], cosf, sinf, s0, c0, n0, m0b, w["g_ret"], w["g_ml"])
    h1 = _out_proj(mix.reshape(t, D_MODEL), w["w_out"], x2)
    h2 = _peer(w["w_q"], w["k1"], w["k2"], h1, w["g_ffn"], w["uv"])
    y = _ple(h2, p.reshape(t, PLE_DIM), w["g_ple"], w["w_ple_gate"], w["w_ple_proj"], w["g_final"])
    return y.reshape(b, l, D_MODEL), s_new[None], c_new[None], n_new[None], m_new[None, :, :, 0]


def _prep_weights(g_mix, w_in, b_gates, g_ret, g_mlstm, w_out, g_ffn, w_peer_q, peer_keys1, peer_keys2, peer_u, peer_v,
                  g_ple, w_ple_gate, w_ple_proj, g_final):
    w_in0 = w_in[0]
    return {
        "g_mix": g_mix,
        "w_in": w_in0[:, :Z_COLS].astype(BF16),
        "w_gates": jnp.pad(w_in0[:, Z_COLS:], ((0, 0), (0, LANES - 2 * MLSTM_HEADS))).astype(BF16),
        "bias": jnp.pad(b_gates, ((0, 0), (0, LANES - 2 * MLSTM_HEADS))),
        "g_ret": g_ret,
        "g_ml": g_mlstm,
        "w_out": w_out[0].astype(BF16),
        "g_ffn": g_ffn,
        "w_q": w_peer_q[0].astype(BF16).reshape(D_MODEL, 2 * PEER_HEADS, N_KEYS).transpose(1, 0, 2),
        "k1": peer_keys1[0].astype(BF16),
        "k2": peer_keys2[0].astype(BF16),
        "uv": jnp.concatenate([_pack_bf16_pairs(peer_u[0]), _pack_bf16_pairs(peer_v[0])], axis=1)[:, None, :],
        "g_ple": g_ple,
        "w_ple_gate": w_ple_gate[0].astype(BF16),
        "w_ple_proj": w_ple_proj[0].astype(BF16),
        "g_final": g_final[None, :],
    }


def kernel(x_prompt, x_sample, p_prompt, p_sample, state_ret, state_mlstm_C, state_mlstm_n, state_mlstm_m, g_mix, w_in, b_gates, g_ret, g_mlstm, w_out, g_ffn, w_peer_q, peer_keys1, peer_keys2, peer_u, peer_v, g_ple, w_ple_gate, w_ple_proj, g_final):
    w = _prep_weights(g_mix, w_in, b_gates, g_ret, g_mlstm, w_out, g_ffn, w_peer_q, peer_keys1, peer_keys2, peer_u, peer_v,
                      g_ple, w_ple_gate, w_ple_proj, g_final)
    bp, lp, _ = x_prompt.shape
    bs, ls, _ = x_sample.shape
    zeros = lambda *shape: jnp.zeros(shape, F32)
    y_s, ret_s, c_s, n_s, m_s = _stream(
        x_sample, p_sample[0], PAST_LEN + jnp.arange(ls, dtype=F32),
        state_ret[0], state_mlstm_C[0], state_mlstm_n[0], state_mlstm_m[0], w)
    y_p, ret_p, c_p, n_p, m_p = _stream(
        x_prompt, p_prompt[0], jnp.arange(lp, dtype=F32),
        zeros(bp, RET_HEADS, RET_D, RET_D), zeros(bp, MLSTM_HEADS, MLSTM_DK, MLSTM_DV),
        zeros(bp, MLSTM_HEADS, MLSTM_DK), zeros(bp, MLSTM_HEADS), w)
    return (y_p, y_s, ret_p, c_p, n_p, m_p, ret_s, c_s, n_s, m_s)
```

```python
import math

import numpy as np
import jax
import jax.numpy as jnp
from jax import lax
from jax.experimental import pallas as pl
from jax.experimental.pallas import tpu as pltpu

F32 = jnp.float32
BF16 = jnp.bfloat16

D_MODEL = 2048
CHUNK = 64
PAST_LEN = 1024
RMS_EPS = 1e-6
ROPE_BASE = 10000.0
RET_HEADS = 8
RET_D = 128
MLSTM_HEADS = 4
MLSTM_DK = 128
MLSTM_DV = 256
Z_COLS = 7168
OFF_RQ, OFF_RK, OFF_RV, OFF_RG = 0, 1024, 2048, 3072
OFF_MQ, OFF_MK, OFF_MV, OFF_MO = 4096, 4608, 5120, 6144
PEER_HEADS = 8
N_KEYS = 128
PEER_TOPK = 16
PEER_K = PEER_HEADS * PEER_TOPK
PLE_DIM = 256
LANES = 128
SUBLANES = 8
CAND_ROWS = PEER_TOPK + (SUBLANES - 1) * SUBLANES + SUBLANES

LOG_GAMMA = [float(np.log(np.float32(1.0) - np.float32(2.0) ** np.float32(-5.0 - h))) for h in range(RET_HEADS)]

VMEM_LIMIT = 56 * 1024 * 1024


def _params(sem):
    return pltpu.CompilerParams(dimension_semantics=sem, vmem_limit_bytes=VMEM_LIMIT)


def _rms(x, g):
    return x * lax.rsqrt(jnp.mean(x * x, axis=-1, keepdims=True) + RMS_EPS) * g


def _mm(a, b):
    return jnp.dot(a.astype(BF16), b.astype(BF16), preferred_element_type=F32)


def _mm_nt(a, b):
    return lax.dot_general(a.astype(BF16), b.astype(BF16), (((1,), (1,)), ((), ())), preferred_element_type=F32)


def _sigmoid(x):
    return 1.0 / (1.0 + jnp.exp(-x))


def _in_proj_kernel(x_ref, g_ref, w_ref, wg_ref, z_ref, gz_ref, a_scr):
    @pl.when(pl.program_id(1) == 0)
    def _():
        a_scr[...] = _rms(x_ref[...], g_ref[...]).astype(BF16)
        gz_ref[...] = jnp.dot(a_scr[...], wg_ref[...], preferred_element_type=F32)

    z_ref[...] = jnp.dot(a_scr[...], w_ref[...], preferred_element_type=F32)


def _in_proj(x, g, w, wg, tm=1024, tn=1024):
    t = x.shape[0]
    tm = min(tm, t)
    return pl.pallas_call(
        _in_proj_kernel,
        grid=(t // tm, Z_COLS // tn),
        in_specs=[
            pl.BlockSpec((tm, D_MODEL), lambda i, j: (i, 0)),
            pl.BlockSpec((1, D_MODEL), lambda i, j: (0, 0)),
            pl.BlockSpec((D_MODEL, tn), lambda i, j: (0, j)),
            pl.BlockSpec((D_MODEL, LANES), lambda i, j: (0, 0)),
        ],
        out_specs=[
            pl.BlockSpec((tm, tn), lambda i, j: (i, j)),
            pl.BlockSpec((tm, LANES), lambda i, j: (i, 0)),
        ],
        out_shape=[jax.ShapeDtypeStruct((t, Z_COLS), F32), jax.ShapeDtypeStruct((t, LANES), F32)],
        scratch_shapes=[pltpu.VMEM((tm, D_MODEL), BF16)],
        compiler_params=_params(("arbitrary", "arbitrary")),
        name="in_proj",
    )(x, g, w, wg)


def _log_sigmoid(x):
    return -(jnp.maximum(-x, 0.0) + jnp.log1p(jnp.exp(-jnp.abs(x))))


def _mixers_kernel(z_ref, gz_ref, bias_ref, cos_ref, sin_ref, s0_ref, c0_ref, n0_ref, m0_ref, gret_ref, gml_ref,
                   mix_ref, s_ref, c_ref, n_ref, m_ref):
    @pl.when(pl.program_id(1) == 0)
    def _():
        s_ref[...] = s0_ref[...]
        c_ref[...] = c0_ref[...]
        n_ref[...] = n0_ref[...]
        m_ref[...] = m0_ref[...]

    cl = CHUNK
    row = lax.broadcasted_iota(jnp.int32, (cl, cl), 0)
    col = lax.broadcasted_iota(jnp.int32, (cl, cl), 1)
    causal = row >= col
    diff = jnp.where(causal, (row - col).astype(F32), 0.0)
    posc = lax.broadcasted_iota(jnp.int32, (cl, 1), 0).astype(F32)
    cosf = cos_ref[...]
    sinf = sin_ref[...]

    def rope(x):
        return x * cosf + pltpu.roll(x, RET_D // 2, axis=1) * sinf

    ret = []
    for h in range(RET_HEADS):
        lg = LOG_GAMMA[h]
        lo = h * RET_D
        q = rope(z_ref[0, :, OFF_RQ + lo:OFF_RQ + lo + RET_D])
        k = rope(z_ref[0, :, OFF_RK + lo:OFF_RK + lo + RET_D]) * (RET_D ** -0.5)
        q_dec = (q * jnp.exp((posc + 1.0) * lg)).astype(BF16)
        k_dec_t = (k * jnp.exp((cl - 1.0 - posc) * lg)).T.astype(BF16)
        ret.append((q.astype(BF16), k.astype(BF16), q_dec, k_dec_t))

    gates = gz_ref[0] + bias_ref[...]
    tri = causal.astype(F32)
    bcum = jnp.dot(tri, _log_sigmoid(gates), preferred_element_type=F32, precision=lax.Precision.HIGHEST)
    bcum_t = bcum.T
    gates_t = gates.T
    mls = []
    for h in range(MLSTM_HEADS):
        k = z_ref[0, :, OFF_MK + h * MLSTM_DK:OFF_MK + (h + 1) * MLSTM_DK] * (MLSTM_DK ** -0.5)
        f = MLSTM_HEADS + h
        b_col = bcum[:, f:f + 1]
        b_row = bcum_t[f:f + 1, :]
        ig_col = gates[:, h:h + 1]
        ig_row = gates_t[h:h + 1, :]
        m_prev = m_ref[0, h:h + 1, 0:1]
        dlog = jnp.where(causal, b_col - b_row + ig_row, -jnp.inf)
        inter_log = b_col + m_prev
        m_t = jnp.maximum(inter_log, jnp.max(dlog, axis=-1, keepdims=True))
        dw = jnp.exp(dlog - m_t)
        inter_w = jnp.exp(inter_log - m_t)
        m_new = m_t[cl - 1:cl, :]
        b_last = b_col[cl - 1:cl, :]
        ws = jnp.exp(b_last - b_col + ig_col - m_new)
        carry = jnp.exp(b_last + m_prev - m_new)
        kw = k * ws
        n_old = n_ref[0, h:h + 1, :]
        n_ref[0, h:h + 1, :] = carry * n_old + jnp.sum(kw, axis=0, keepdims=True)
        m_ref[0, h:h + 1, :] = jnp.broadcast_to(m_new, (1, LANES))
        mls.append((k.astype(BF16), kw.T.astype(BF16), dw, inter_w, m_t, carry, n_old))

    for h in range(RET_HEADS):
        lg = LOG_GAMMA[h]
        lo = h * RET_D
        q, k, q_dec, k_dec_t = ret[h]
        v = z_ref[0, :, OFF_RV + lo:OFF_RV + lo + RET_D].astype(BF16)
        rg = z_ref[0, :, OFF_RG + lo:OFF_RG + lo + RET_D]
        decay = jnp.where(causal, jnp.exp(diff * lg), 0.0)
        s_old = s_ref[0, h]
        scores = _mm_nt(q, k) * decay
        o = _mm(scores, v) + _mm(q_dec, s_old)
        s_ref[0, h] = math.exp(cl * lg) * s_old + _mm(k_dec_t, v)
        y = o * lax.rsqrt(jnp.mean(o * o, axis=-1, keepdims=True) + RMS_EPS) * gret_ref[:, lo:lo + RET_D]
        mix_ref[0, :, lo:lo + RET_D] = (y * (rg * _sigmoid(rg))).astype(BF16)

    for h in range(MLSTM_HEADS):
        k, kw_t, dw, inter_w, m_t, carry, n_old = mls[h]
        q = z_ref[0, :, OFF_MQ + h * MLSTM_DK:OFF_MQ + (h + 1) * MLSTM_DK]
        v = z_ref[0, :, OFF_MV + h * MLSTM_DV:OFF_MV + (h + 1) * MLSTM_DV].astype(BF16)
        mo = z_ref[0, :, OFF_MO + h * MLSTM_DV:OFF_MO + (h + 1) * MLSTM_DV]
        c_old = c_ref[0, h]
        sm = _mm_nt(q, k) * dw
        num = _mm(sm, v) + inter_w * _mm(q, c_old)
        den = jnp.sum(sm, axis=-1, keepdims=True) + inter_w * jnp.sum(q * n_old, axis=-1, keepdims=True)
        hh = num / jnp.maximum(jnp.abs(den), jnp.exp(-m_t))
        c_ref[0, h] = carry * c_old + _mm(kw_t, v)
        y = hh * lax.rsqrt(jnp.mean(hh * hh, axis=-1, keepdims=True) + RMS_EPS) * gml_ref[:, h * MLSTM_DV:(h + 1) * MLSTM_DV]
        lo = RET_HEADS * RET_D + h * MLSTM_DV
        mix_ref[0, :, lo:lo + MLSTM_DV] = (y * _sigmoid(mo)).astype(BF16)


def _mixers(z, gz, bias, cosf, sinf, s0, c0, n0, m0, g_ret, g_ml):
    b, l, _ = z.shape
    per_b = lambda *tail: (lambda i, c: (i,) + tail)
    return pl.pallas_call(
        _mixers_kernel,
        grid=(b, l // CHUNK),
        in_specs=[
            pl.BlockSpec((1, CHUNK, Z_COLS), lambda i, c: (i, c, 0)),
            pl.BlockSpec((1, CHUNK, LANES), lambda i, c: (i, c, 0)),
            pl.BlockSpec((1, LANES), lambda i, c: (0, 0)),
            pl.BlockSpec((CHUNK, RET_D), lambda i, c: (c, 0)),
            pl.BlockSpec((CHUNK, RET_D), lambda i, c: (c, 0)),
            pl.BlockSpec((1, RET_HEADS, RET_D, RET_D), per_b(0, 0, 0)),
            pl.BlockSpec((1, MLSTM_HEADS, MLSTM_DK, MLSTM_DV), per_b(0, 0, 0)),
            pl.BlockSpec((1, MLSTM_HEADS, MLSTM_DK), per_b(0, 0)),
            pl.BlockSpec((1, MLSTM_HEADS, LANES), per_b(0, 0)),
            pl.BlockSpec((1, RET_HEADS * RET_D), lambda i, c: (0, 0)),
            pl.BlockSpec((1, MLSTM_HEADS * MLSTM_DV), lambda i, c: (0, 0)),
        ],
        out_specs=[
            pl.BlockSpec((1, CHUNK, D_MODEL), lambda i, c: (i, c, 0)),
            pl.BlockSpec((1, RET_HEADS, RET_D, RET_D), per_b(0, 0, 0)),
            pl.BlockSpec((1, MLSTM_HEADS, MLSTM_DK, MLSTM_DV), per_b(0, 0, 0)),
            pl.BlockSpec((1, MLSTM_HEADS, MLSTM_DK), per_b(0, 0)),
            pl.BlockSpec((1, MLSTM_HEADS, LANES), per_b(0, 0)),
        ],
        out_shape=[
            jax.ShapeDtypeStruct((b, l, D_MODEL), BF16),
            jax.ShapeDtypeStruct((b, RET_HEADS, RET_D, RET_D), F32),
            jax.ShapeDtypeStruct((b, MLSTM_HEADS, MLSTM_DK, MLSTM_DV), F32),
            jax.ShapeDtypeStruct((b, MLSTM_HEADS, MLSTM_DK), F32),
            jax.ShapeDtypeStruct((b, MLSTM_HEADS, LANES), F32),
        ],
        compiler_params=_params(("arbitrary", "arbitrary")),
        name="mixers",
    )(z, gz, bias, cosf, sinf, s0, c0, n0, m0, g_ret, g_ml)


def _out_proj_kernel(a_ref, w_ref, r_ref, o_ref):
    o_ref[...] = r_ref[...] + jnp.dot(a_ref[...], w_ref[...], preferred_element_type=F32)


def _out_proj(a, w, r, tm=512):
    t = a.shape[0]
    return pl.pallas_call(
        _out_proj_kernel,
        grid=(t // tm,),
        in_specs=[
            pl.BlockSpec((tm, D_MODEL), lambda i: (i, 0)),
            pl.BlockSpec((D_MODEL, D_MODEL), lambda i: (0, 0)),
            pl.BlockSpec((tm, D_MODEL), lambda i: (i, 0)),
        ],
        out_specs=pl.BlockSpec((tm, D_MODEL), lambda i: (i, 0)),
        out_shape=jax.ShapeDtypeStruct((t, D_MODEL), F32),
        compiler_params=_params(("arbitrary",)),
        name="out_proj",
    )(a, w, r)


def _top16_steps(x_ref, payload_ref, val_ref, idx_ref):
    rows = x_ref.shape[0]
    for r in range(PEER_TOPK):
        x = x_ref[...]
        iota = lax.broadcasted_iota(jnp.int32, x.shape, 0)
        m = jnp.max(x, axis=0, keepdims=True)
        am = jnp.min(jnp.where(x == m, iota, rows), axis=0, keepdims=True)
        sel = iota == am
        val_ref[r:r + 1, :] = m
        if payload_ref is None:
            idx_ref[r:r + 1, :] = am
        else:
            idx_ref[r:r + 1, :] = jnp.max(jnp.where(sel, payload_ref[...], -1), axis=0, keepdims=True)
        x_ref[...] = jnp.where(sel, -jnp.inf, x)
        yield


ROUTE_SCRATCH = [
    pltpu.VMEM((N_KEYS, LANES), F32),
    pltpu.VMEM((N_KEYS, LANES), F32),
    pltpu.VMEM((PEER_TOPK, LANES), F32),
    pltpu.VMEM((PEER_TOPK, LANES), jnp.int32),
    pltpu.VMEM((PEER_TOPK, LANES), F32),
    pltpu.VMEM((PEER_TOPK, LANES), jnp.int32),
    pltpu.VMEM((CAND_ROWS, LANES), F32),
    pltpu.VMEM((CAND_ROWS, LANES), jnp.int32),
    pltpu.VMEM((PEER_TOPK, LANES), F32),
    pltpu.VMEM((PEER_TOPK, LANES), jnp.int32),
]


def _route_unit_steps(q1, q2, k1_ref, k2_ref, work, emit):
    s1_scr, s2_scr, t1_scr, i1_scr, t2_scr, i2_scr, cand_scr, cidx_scr, sc_scr, e_scr = work
    s1_scr[...] = _mm_nt(k1_ref[...], q1)
    s2_scr[...] = _mm_nt(k2_ref[...], q2)
    yield
    for _ in zip(_top16_steps(s1_scr, None, t1_scr, i1_scr), _top16_steps(s2_scr, None, t2_scr, i2_scr)):
        yield
    cand_scr[0:PEER_TOPK, :] = t1_scr[0:1, :] + t2_scr[...]
    cidx_scr[0:PEER_TOPK, :] = i1_scr[0:1, :] * N_KEYS + i2_scr[...]
    t2 = t2_scr[0:SUBLANES, :]
    i2 = i2_scr[0:SUBLANES, :]
    sub_iota = lax.broadcasted_iota(jnp.int32, (SUBLANES, LANES), 0)
    for a in range(1, SUBLANES):
        lo = PEER_TOPK + (a - 1) * SUBLANES
        cand_scr[lo:lo + SUBLANES, :] = jnp.where(sub_iota < PEER_TOPK // (a + 1), t1_scr[a:a + 1, :] + t2, -jnp.inf)
        cidx_scr[lo:lo + SUBLANES, :] = i1_scr[a:a + 1, :] * N_KEYS + i2
    cand_scr[CAND_ROWS - SUBLANES:CAND_ROWS, :] = t1_scr[SUBLANES:PEER_TOPK, :] + t2_scr[0:1, :]
    cidx_scr[CAND_ROWS - SUBLANES:CAND_ROWS, :] = i1_scr[SUBLANES:PEER_TOPK, :] * N_KEYS + i2_scr[0:1, :]
    yield
    yield from _top16_steps(cand_scr, cidx_scr, sc_scr, e_scr)
    sc = sc_scr[...]
    p = jnp.exp(sc - sc[0:1, :])
    emit(e_scr[...], p / jnp.sum(p, axis=0, keepdims=True))


PEER_TB = 512
PEER_SLOTS = 8
PEER_LEAD = PEER_SLOTS - 1
PEER_SUBS = PEER_TB // LANES
ROUTE_UNITS = PEER_HEADS * PEER_SUBS
PEER_GROUP = PEER_TB // ROUTE_UNITS
assert PEER_GROUP % PEER_SLOTS == 0
ROUTE_STEPS = 2 * PEER_TOPK + 3
K_TILE = 256
PREP_CHUNKS = 8
assert 2 * PREP_CHUNKS + 2 >= 2 * PEER_LEAD
SIDE_AFTER = (4, 10, 15)
assert len(SIDE_AFTER) * PEER_GROUP >= ROUTE_STEPS
HALF_D = D_MODEL // 2
PACK_TILES = HALF_D // LANES
INV_SQRT2 = 0.7071067811865476


def _pack_bf16_pairs(a):
    bits = lax.bitcast_convert_type(a.astype(BF16), jnp.uint16).astype(jnp.uint32)
    return bits[:, :HALF_D] | (bits[:, HALF_D:] << 16)


def _peer_kernel(hn_ref, wq_ref, k1_ref, k2_ref, h_ref, g_ref, uv_hbm, o_ref,
                 idx_s, idx_v, gate_scr, et_scr, gt_scr, c_scr, cn_scr, q_scr, acc_scr, *scratch):
    work = scratch[:len(ROUTE_SCRATCH)]
    rows = scratch[len(ROUTE_SCRATCH):len(ROUTE_SCRATCH) + PEER_SLOTS]
    idx_sem, row_sems = scratch[len(ROUTE_SCRATCH) + PEER_SLOTS:]
    tb = h_ref.shape[0]

    def route(q_ref, u):
        if isinstance(u, int):
            sub, hd = u % PEER_SUBS, u // PEER_SUBS
            lo = hd * PEER_TOPK
        else:
            sub = jnp.bitwise_and(u, PEER_SUBS - 1)
            hd = jnp.right_shift(u, PEER_SUBS.bit_length() - 1)
            lo = pl.multiple_of(hd * PEER_TOPK, PEER_TOPK)

        def emit(e, gate):
            et_scr[sub, pl.ds(lo, PEER_TOPK), :] = e
            gt_scr[sub, pl.ds(lo, PEER_TOPK), :] = gate

        return _route_unit_steps(q_ref[sub, 2 * hd], q_ref[sub, 2 * hd + 1], k1_ref, k2_ref, work, emit)

    def finish(steps):
        for _ in steps:
            pass

    def query_block(j):
        for sub in range(PEER_SUBS):
            q_scr[sub, j] = jnp.dot(cn_scr[sub * LANES:(sub + 1) * LANES, :], wq_ref[j],
                                    preferred_element_type=F32).astype(BF16)

    def query_steps(i):
        if isinstance(i, int):
            hd, jb, half = min(i // PEER_SUBS + 1, PEER_HEADS - 1), (i % PEER_SUBS) // 2, i % 2
            r0 = half * 2 * LANES
        else:
            hd = jnp.minimum(jnp.right_shift(i, PEER_SUBS.bit_length() - 1) + 1, PEER_HEADS - 1)
            jb = jnp.right_shift(jnp.bitwise_and(i, PEER_SUBS - 1), 1)
            half = jnp.bitwise_and(i, 1)
            r0 = pl.multiple_of(half * 2 * LANES, 2 * LANES)
        j = 2 * hd + jb
        for kt in range(D_MODEL // K_TILE):
            part = jnp.dot(cn_scr[pl.ds(r0, 2 * LANES), kt * K_TILE:(kt + 1) * K_TILE],
                           wq_ref[j, kt * K_TILE:(kt + 1) * K_TILE, :], preferred_element_type=F32)
            acc_scr[...] = part if kt == 0 else acc_scr[...] + part
            yield
        q_scr[2 * half, j] = acc_scr[0:LANES, :].astype(BF16)
        q_scr[2 * half + 1, j] = acc_scr[LANES:2 * LANES, :].astype(BF16)
        yield

    def merged(main, extra, every):
        n = 0
        for _ in main:
            n += 1
            if n % every == 0:
                next(extra, None)
            yield
        for _ in extra:
            yield

    ids_copy = pltpu.make_async_copy(idx_v, idx_s, idx_sem)

    def publish():
        for sub in range(PEER_SUBS):
            idx_v[sub * LANES:(sub + 1) * LANES, :] = et_scr[sub].T
            gate_scr[sub * LANES:(sub + 1) * LANES, :] = gt_scr[sub].T
        ids_copy.start()

    @pl.when(pl.program_id(0) == 0)
    def _():
        cn_scr[...] = _rms(h_ref[...], g_ref[...]).astype(BF16)

        def qbody(j, carry):
            query_block(j)
            return carry

        lax.fori_loop(0, 2 * PEER_HEADS, qbody, 0)

        def body(u, carry):
            finish(route(q_scr, u))
            return carry

        lax.fori_loop(0, ROUTE_UNITS, body, 0)
        publish()


    def issue(t, slot, k0, k1):
        for k in range(k0, k1):
            e = idx_s[t, k]
            pltpu.make_async_copy(uv_hbm.at[e], rows[slot].at[pl.ds(k, 1), :], row_sems.at[slot]).start(priority=k % 2)

    def wait(slot):
        pltpu.make_async_copy(uv_hbm.at[pl.ds(0, PEER_K), 0], rows[slot], row_sems.at[slot]).wait()

    def unpack(words):
        lo = lax.bitcast_convert_type(jnp.left_shift(words, jnp.uint32(16)), F32)
        hi = lax.bitcast_convert_type(jnp.bitwise_and(words, jnp.uint32(0xFFFF0000)), F32)
        return lo, hi

    eye = lax.broadcasted_iota(jnp.int32, (PEER_K, LANES), 0) == lax.broadcasted_iota(jnp.int32, (PEER_K, LANES), 1)
    per_piece = PEER_K // (2 * PACK_TILES)

    def token(t, slot, prefetch, side_steps):
        wait(slot)
        ahead_slot = (slot + PEER_LEAD) % PEER_SLOTS
        x = c_scr[pl.ds(t, 1), :]
        acc = None
        for j in range(PACK_TILES):
            if prefetch:
                issue(t + PEER_LEAD, ahead_slot, j * per_piece, (j + 1) * per_piece)
            lo, hi = unpack(rows[slot][:, j * LANES:(j + 1) * LANES])
            p = lo * x[:, j * LANES:(j + 1) * LANES] + hi * x[:, HALF_D + j * LANES:HALF_D + (j + 1) * LANES]
            acc = p if acc is None else acc + p
            if j in SIDE_AFTER:
                next(side_steps, None)
        act = jnp.sum(acc, axis=1, keepdims=True)
        gate_col = jnp.sum(jnp.where(eye, gate_scr[pl.ds(t, 1), :], 0.0), axis=1, keepdims=True)
        coef = gate_col * (0.5 * act * (1.0 + lax.erf(act * INV_SQRT2)))
        mixed_lo, mixed_hi = [], []
        for j in range(PACK_TILES):
            if prefetch:
                issue(t + PEER_LEAD, ahead_slot, (PACK_TILES + j) * per_piece, (PACK_TILES + j + 1) * per_piece)
            lo, hi = unpack(rows[slot][:, HALF_D + j * LANES:HALF_D + (j + 1) * LANES])
            mixed_lo.append(jnp.sum(lo * coef, axis=0, keepdims=True))
            mixed_hi.append(jnp.sum(hi * coef, axis=0, keepdims=True))
            if PACK_TILES + j in SIDE_AFTER:
                next(side_steps, None)
        o_ref[pl.ds(t, 1), :] = h_ref[pl.ds(t, 1), :] + jnp.concatenate(mixed_lo + mixed_hi, axis=1)

    chunk = tb // PREP_CHUNKS

    def norm_rows(dst, src, r):
        rs = slice(r * chunk, (r + 1) * chunk)
        dst[rs, :] = _rms(src[rs, :], g_ref[...]).astype(dst.dtype)

    prep = ([lambda r=r: norm_rows(c_scr, h_ref, r) for r in range(PREP_CHUNKS)]
            + [lambda r=r: norm_rows(cn_scr, hn_ref, r) for r in range(PREP_CHUNKS)]
            + [lambda j=j: query_block(j) for j in range(2)])
    fill = [(s, part) for s in range(PEER_LEAD) for part in range(2)]
    before = len(prep) - len(fill)
    for piece in prep[:before]:
        piece()
    ids_copy.wait()
    for (s, part), piece in zip(fill, prep[before:]):
        issue(s, s, part * (PEER_K // 2), (part + 1) * (PEER_K // 2))
        piece()

    def group(i, carry):
        steps = merged(route(q_scr, i), query_steps(i), 3)
        for s in range(PEER_GROUP):
            token(i * PEER_GROUP + s, s % PEER_SLOTS, True, steps)
        finish(steps)
        return carry

    lax.fori_loop(0, ROUTE_UNITS - 1, group, 0)
    last = tb - PEER_GROUP
    steps = merged(route(q_scr, ROUTE_UNITS - 1), query_steps(ROUTE_UNITS - 1), 3)
    for s in range(PEER_GROUP):
        token(last + s, s % PEER_SLOTS, s + PEER_LEAD < PEER_GROUP, steps)
    finish(steps)
    pl.when(pl.program_id(0) + 1 < pl.num_programs(0))(publish)


def _peer(wq3, k1, k2, h, g, uv):
    t = h.shape[0]
    tb = PEER_TB
    steps = t // tb
    return pl.pallas_call(
        _peer_kernel,
        grid=(steps,),
        in_specs=[
            pl.BlockSpec((tb, D_MODEL), lambda i: (jnp.minimum(i + 1, steps - 1), 0)),
            pl.BlockSpec((2 * PEER_HEADS, D_MODEL, N_KEYS), lambda i: (0, 0, 0), pipeline_mode=pl.Buffered(1)),
            pl.BlockSpec((N_KEYS, N_KEYS), lambda i: (0, 0)),
            pl.BlockSpec((N_KEYS, N_KEYS), lambda i: (0, 0)),
            pl.BlockSpec((tb, D_MODEL), lambda i: (i, 0)),
            pl.BlockSpec((1, D_MODEL), lambda i: (0, 0)),
            pl.BlockSpec(memory_space=pl.ANY),
        ],
        out_specs=pl.BlockSpec((tb, D_MODEL), lambda i: (i, 0)),
        out_shape=jax.ShapeDtypeStruct((t, D_MODEL), F32),
        scratch_shapes=[
            pltpu.SMEM((tb, PEER_K), jnp.int32),
            pltpu.VMEM((tb, PEER_K), jnp.int32),
            pltpu.VMEM((tb, PEER_K), F32),
            pltpu.VMEM((PEER_SUBS, PEER_K, LANES), jnp.int32),
            pltpu.VMEM((PEER_SUBS, PEER_K, LANES), F32),
            pltpu.VMEM((tb, D_MODEL), F32),
            pltpu.VMEM((tb, D_MODEL), BF16),
            pltpu.VMEM((PEER_SUBS, 2 * PEER_HEADS, LANES, N_KEYS), BF16),
            pltpu.VMEM((2 * LANES, N_KEYS), F32),
        ] + ROUTE_SCRATCH + [pltpu.VMEM((PEER_K, D_MODEL), jnp.uint32)] * PEER_SLOTS + [
            pltpu.SemaphoreType.DMA(()),
            pltpu.SemaphoreType.DMA((PEER_SLOTS,)),
        ],
        compiler_params=_params(("arbitrary",)),
        name="peer",
    )(h, wq3, k1, k2, h, g, uv)


def _ple_kernel(h_ref, p_ref, gple_ref, wg_ref, wp_ref, gfin_ref, y_ref):
    h = h_ref[...]
    e = _rms(h, gple_ref[...]).astype(BF16)
    gate = _sigmoid(jnp.dot(e, wg_ref[...], preferred_element_type=F32))
    proj = jnp.dot(p_ref[...].astype(BF16), wp_ref[...], preferred_element_type=F32)
    y_ref[...] = _rms(h + gate * proj, gfin_ref[...])


def _ple(h, p, g_ple, wg, wp, g_fin, tm=512):
    t = h.shape[0]
    return pl.pallas_call(
        _ple_kernel,
        grid=(t // tm,),
        in_specs=[
            pl.BlockSpec((tm, D_MODEL), lambda i: (i, 0)),
            pl.BlockSpec((tm, PLE_DIM), lambda i: (i, 0)),
            pl.BlockSpec((1, D_MODEL), lambda i: (0, 0)),
            pl.BlockSpec((D_MODEL, D_MODEL), lambda i: (0, 0)),
            pl.BlockSpec((PLE_DIM, D_MODEL), lambda i: (0, 0)),
            pl.BlockSpec((1, D_MODEL), lambda i: (0, 0)),
        ],
        out_specs=pl.BlockSpec((tm, D_MODEL), lambda i: (i, 0)),
        out_shape=jax.ShapeDtypeStruct((t, D_MODEL), F32),
        compiler_params=_params(("arbitrary",)),
        name="ple",
    )(h, p, g_ple, wg, wp, g_fin)


def _rope_tables(pos):
    half = RET_D // 2
    inv = ROPE_BASE ** (-jnp.arange(half, dtype=F32) / half)
    ang = pos[:, None] * inv[None, :]
    cos, sin = jnp.cos(ang), jnp.sin(ang)
    return jnp.concatenate([cos, cos], axis=-1), jnp.concatenate([-sin, sin], axis=-1)


def _stream(x, p, pos, s0, c0, n0, m0, w):
    b, l, _ = x.shape
    t = b * l
    x2 = x.reshape(t, D_MODEL)
    z, gz = _in_proj(x2, w["g_mix"], w["w_in"], w["w_gates"])
    cosf, sinf = _rope_tables(pos)
    m0b = jnp.broadcast_to(m0[:, :, None], (b, MLSTM_HEADS, LANES))
    mix, s_new, c_new, n_new, m_new = _mixers(
        z.reshape(b, l, Z_COLS), gz.reshape(b, l, LANES), w["bias"], cosf, sinf, s0, c0, n0, m0b, w["g_ret"], w["g_ml"])
    h1 = _out_proj(mix.reshape(t, D_MODEL), w["w_out"], x2)
    h2 = _peer(w["w_q"], w["k1"], w["k2"], h1, w["g_ffn"], w["uv"])
    y = _ple(h2, p.reshape(t, PLE_DIM), w["g_ple"], w["w_ple_gate"], w["w_ple_proj"], w["g_final"])
    return y.reshape(b, l, D_MODEL), s_new[None], c_new[None], n_new[None], m_new[None, :, :, 0]


def _prep_weights(g_mix, w_in, b_gates, g_ret, g_mlstm, w_out, g_ffn, w_peer_q, peer_keys1, peer_keys2, peer_u, peer_v,
                  g_ple, w_ple_gate, w_ple_proj, g_final):
    w_in0 = w_in[0]
    return {
        "g_mix": g_mix,
        "w_in": w_in0[:, :Z_COLS].astype(BF16),
        "w_gates": jnp.pad(w_in0[:, Z_COLS:], ((0, 0), (0, LANES - 2 * MLSTM_HEADS))).astype(BF16),
        "bias": jnp.pad(b_gates, ((0, 0), (0, LANES - 2 * MLSTM_HEADS))),
        "g_ret": g_ret,
        "g_ml": g_mlstm,
        "w_out": w_out[0].astype(BF16),
        "g_ffn": g_ffn,
        "w_q": w_peer_q[0].astype(BF16).reshape(D_MODEL, 2 * PEER_HEADS, N_KEYS).transpose(1, 0, 2),
        "k1": peer_keys1[0].astype(BF16),
        "k2": peer_keys2[0].astype(BF16),
        "uv": jnp.concatenate([_pack_bf16_pairs(peer_u[0]), _pack_bf16_pairs(peer_v[0])], axis=1)[:, None, :],
        "g_ple": g_ple,
        "w_ple_gate": w_ple_gate[0].astype(BF16),
        "w_ple_proj": w_ple_proj[0].astype(BF16),
        "g_final": g_final[None, :],
    }


def kernel(x_prompt, x_sample, p_prompt, p_sample, state_ret, state_mlstm_C, state_mlstm_n, state_mlstm_m, g_mix, w_in, b_gates, g_ret, g_mlstm, w_out, g_ffn, w_peer_q, peer_keys1, peer_keys2, peer_u, peer_v, g_ple, w_ple_gate, w_ple_proj, g_final):
    w = _prep_weights(g_mix, w_in, b_gates, g_ret, g_mlstm, w_out, g_ffn, w_peer_q, peer_keys1, peer_keys2, peer_u, peer_v,
                      g_ple, w_ple_gate, w_ple_proj, g_final)
    bp, lp, _ = x_prompt.shape
    bs, ls, _ = x_sample.shape
    zeros = lambda *shape: jnp.zeros(shape, F32)
    y_s, ret_s, c_s, n_s, m_s = _stream(
        x_sample, p_sample[0], PAST_LEN + jnp.arange(ls, dtype=F32),
        state_ret[0], state_mlstm_C[0], state_mlstm_n[0], state_mlstm_m[0], w)
    y_p, ret_p, c_p, n_p, m_p = _stream(
        x_prompt, p_prompt[0], jnp.arange(lp, dtype=F32),
        zeros(bp, RET_HEADS, RET_D, RET_D), zeros(bp, MLSTM_HEADS, MLSTM_DK, MLSTM_DV),
        zeros(bp, MLSTM_HEADS, MLSTM_DK), zeros(bp, MLSTM_HEADS), w)
    return (y_p, y_s, ret_p, c_p, n_p, m_p, ret_s, c_s, n_s, m_s)
```

```python
import math

import numpy as np
import jax
import jax.numpy as jnp
from jax import lax
from jax.experimental import pallas as pl
from jax.experimental.pallas import tpu as pltpu

F32 = jnp.float32
BF16 = jnp.bfloat16

D_MODEL = 2048
CHUNK = 64
PAST_LEN = 1024
RMS_EPS = 1e-6
ROPE_BASE = 10000.0
RET_HEADS = 8
RET_D = 128
MLSTM_HEADS = 4
MLSTM_DK = 128
MLSTM_DV = 256
Z_COLS = 7168
OFF_RQ, OFF_RK, OFF_RV, OFF_RG = 0, 1024, 2048, 3072
OFF_MQ, OFF_MK, OFF_MV, OFF_MO = 4096, 4608, 5120, 6144
PEER_HEADS = 8
N_KEYS = 128
PEER_TOPK = 16
PEER_K = PEER_HEADS * PEER_TOPK
PLE_DIM = 256
LANES = 128
SUBLANES = 8
CAND_ROWS = PEER_TOPK + (SUBLANES - 1) * SUBLANES + SUBLANES

LOG_GAMMA = [float(np.log(np.float32(1.0) - np.float32(2.0) ** np.float32(-5.0 - h))) for h in range(RET_HEADS)]

VMEM_LIMIT = 56 * 1024 * 1024


def _params(sem, fuse_inputs=None):
    return pltpu.CompilerParams(dimension_semantics=sem, vmem_limit_bytes=VMEM_LIMIT, allow_input_fusion=fuse_inputs)


def _rms(x, g):
    return x * lax.rsqrt(jnp.mean(x * x, axis=-1, keepdims=True) + RMS_EPS) * g


def _mm(a, b):
    return jnp.dot(a.astype(BF16), b.astype(BF16), preferred_element_type=F32)


def _mm_nt(a, b):
    return lax.dot_general(a.astype(BF16), b.astype(BF16), (((1,), (1,)), ((), ())), preferred_element_type=F32)


def _sigmoid(x):
    return 1.0 / (1.0 + jnp.exp(-x))


def _in_proj_kernel(x_ref, g_ref, w_ref, wg_ref, z_ref, gz_ref, a_scr):
    @pl.when(pl.program_id(1) == 0)
    def _():
        a_scr[...] = _rms(x_ref[...], g_ref[...]).astype(BF16)
        gz_ref[...] = jnp.dot(a_scr[...], wg_ref[...], preferred_element_type=F32)

    z_ref[...] = jnp.dot(a_scr[...], w_ref[...], preferred_element_type=F32)


def _in_proj(x, g, w, wg, tm=1024, tn=1024):
    t = x.shape[0]
    tm = min(tm, t)
    return pl.pallas_call(
        _in_proj_kernel,
        grid=(t // tm, Z_COLS // tn),
        in_specs=[
            pl.BlockSpec((tm, D_MODEL), lambda i, j: (i, 0)),
            pl.BlockSpec((1, D_MODEL), lambda i, j: (0, 0)),
            pl.BlockSpec((D_MODEL, tn), lambda i, j: (0, j)),
            pl.BlockSpec((D_MODEL, LANES), lambda i, j: (0, 0)),
        ],
        out_specs=[
            pl.BlockSpec((tm, tn), lambda i, j: (i, j)),
            pl.BlockSpec((tm, LANES), lambda i, j: (i, 0)),
        ],
        out_shape=[jax.ShapeDtypeStruct((t, Z_COLS), F32), jax.ShapeDtypeStruct((t, LANES), F32)],
        scratch_shapes=[pltpu.VMEM((tm, D_MODEL), BF16)],
        compiler_params=_params(("arbitrary", "arbitrary"), fuse_inputs=[False, False, True, True]),
        name="in_proj",
    )(x, g, w, wg)


def _log_sigmoid(x):
    return -(jnp.maximum(-x, 0.0) + jnp.log1p(jnp.exp(-jnp.abs(x))))


def _mixers_kernel(z_ref, gz_ref, bias_ref, cos_ref, sin_ref, s0_ref, c0_ref, n0_ref, m0_ref, gret_ref, gml_ref,
                   mix_ref, s_ref, c_ref, n_ref, m_ref):
    @pl.when(pl.program_id(1) == 0)
    def _():
        s_ref[...] = s0_ref[...]
        c_ref[...] = c0_ref[...]
        n_ref[...] = n0_ref[...]
        m_ref[...] = m0_ref[...]

    cl = CHUNK
    row = lax.broadcasted_iota(jnp.int32, (cl, cl), 0)
    col = lax.broadcasted_iota(jnp.int32, (cl, cl), 1)
    causal = row >= col
    diff = jnp.where(causal, (row - col).astype(F32), 0.0)
    posc = lax.broadcasted_iota(jnp.int32, (cl, 1), 0).astype(F32)
    cosf = cos_ref[...]
    sinf = sin_ref[...]

    def rope(x):
        return x * cosf + pltpu.roll(x, RET_D // 2, axis=1) * sinf

    ret = []
    for h in range(RET_HEADS):
        lg = LOG_GAMMA[h]
        lo = h * RET_D
        q = rope(z_ref[0, :, OFF_RQ + lo:OFF_RQ + lo + RET_D])
        k = rope(z_ref[0, :, OFF_RK + lo:OFF_RK + lo + RET_D]) * (RET_D ** -0.5)
        q_dec = (q * jnp.exp((posc + 1.0) * lg)).astype(BF16)
        k_dec_t = (k * jnp.exp((cl - 1.0 - posc) * lg)).T.astype(BF16)
        ret.append((q.astype(BF16), k.astype(BF16), q_dec, k_dec_t))

    gates = gz_ref[0] + bias_ref[...]
    tri = causal.astype(F32)
    bcum = jnp.dot(tri, _log_sigmoid(gates), preferred_element_type=F32, precision=lax.Precision.HIGHEST)
    bcum_t = bcum.T
    gates_t = gates.T
    mls = []
    for h in range(MLSTM_HEADS):
        k = z_ref[0, :, OFF_MK + h * MLSTM_DK:OFF_MK + (h + 1) * MLSTM_DK] * (MLSTM_DK ** -0.5)
        f = MLSTM_HEADS + h
        b_col = bcum[:, f:f + 1]
        b_row = bcum_t[f:f + 1, :]
        ig_col = gates[:, h:h + 1]
        ig_row = gates_t[h:h + 1, :]
        m_prev = m_ref[0, h:h + 1, 0:1]
        dlog = jnp.where(causal, b_col - b_row + ig_row, -jnp.inf)
        inter_log = b_col + m_prev
        m_t = jnp.maximum(inter_log, jnp.max(dlog, axis=-1, keepdims=True))
        dw = jnp.exp(dlog - m_t)
        inter_w = jnp.exp(inter_log - m_t)
        m_new = m_t[cl - 1:cl, :]
        b_last = b_col[cl - 1:cl, :]
        ws = jnp.exp(b_last - b_col + ig_col - m_new)
        carry = jnp.exp(b_last + m_prev - m_new)
        kw = k * ws
        n_old = n_ref[0, h:h + 1, :]
        n_ref[0, h:h + 1, :] = carry * n_old + jnp.sum(kw, axis=0, keepdims=True)
        m_ref[0, h:h + 1, :] = jnp.broadcast_to(m_new, (1, LANES))
        mls.append((k.astype(BF16), kw.T.astype(BF16), dw, inter_w, m_t, carry, n_old))

    for h in range(RET_HEADS):
        lg = LOG_GAMMA[h]
        lo = h * RET_D
        q, k, q_dec, k_dec_t = ret[h]
        v = z_ref[0, :, OFF_RV + lo:OFF_RV + lo + RET_D].astype(BF16)
        rg = z_ref[0, :, OFF_RG + lo:OFF_RG + lo + RET_D]
        decay = jnp.where(causal, jnp.exp(diff * lg), 0.0)
        s_old = s_ref[0, h]
        scores = _mm_nt(q, k) * decay
        o = _mm(scores, v) + _mm(q_dec, s_old)
        s_ref[0, h] = math.exp(cl * lg) * s_old + _mm(k_dec_t, v)
        y = o * lax.rsqrt(jnp.mean(o * o, axis=-1, keepdims=True) + RMS_EPS) * gret_ref[:, lo:lo + RET_D]
        mix_ref[0, :, lo:lo + RET_D] = (y * (rg * _sigmoid(rg))).astype(BF16)

    for h in range(MLSTM_HEADS):
        k, kw_t, dw, inter_w, m_t, carry, n_old = mls[h]
        q = z_ref[0, :, OFF_MQ + h * MLSTM_DK:OFF_MQ + (h + 1) * MLSTM_DK]
        v = z_ref[0, :, OFF_MV + h * MLSTM_DV:OFF_MV + (h + 1) * MLSTM_DV].astype(BF16)
        mo = z_ref[0, :, OFF_MO + h * MLSTM_DV:OFF_MO + (h + 1) * MLSTM_DV]
        c_old = c_ref[0, h]
        sm = _mm_nt(q, k) * dw
        num = _mm(sm, v) + inter_w * _mm(q, c_old)
        den = jnp.sum(sm, axis=-1, keepdims=True) + inter_w * jnp.sum(q * n_old, axis=-1, keepdims=True)
        hh = num / jnp.maximum(jnp.abs(den), jnp.exp(-m_t))
        c_ref[0, h] = carry * c_old + _mm(kw_t, v)
        y = hh * lax.rsqrt(jnp.mean(hh * hh, axis=-1, keepdims=True) + RMS_EPS) * gml_ref[:, h * MLSTM_DV:(h + 1) * MLSTM_DV]
        lo = RET_HEADS * RET_D + h * MLSTM_DV
        mix_ref[0, :, lo:lo + MLSTM_DV] = (y * _sigmoid(mo)).astype(BF16)


def _mixers(z, gz, bias, cosf, sinf, s0, c0, n0, m0, g_ret, g_ml):
    b, l, _ = z.shape
    per_b = lambda *tail: (lambda i, c: (i,) + tail)
    return pl.pallas_call(
        _mixers_kernel,
        grid=(b, l // CHUNK),
        in_specs=[
            pl.BlockSpec((1, CHUNK, Z_COLS), lambda i, c: (i, c, 0)),
            pl.BlockSpec((1, CHUNK, LANES), lambda i, c: (i, c, 0)),
            pl.BlockSpec((1, LANES), lambda i, c: (0, 0)),
            pl.BlockSpec((CHUNK, RET_D), lambda i, c: (c, 0)),
            pl.BlockSpec((CHUNK, RET_D), lambda i, c: (c, 0)),
            pl.BlockSpec((1, RET_HEADS, RET_D, RET_D), per_b(0, 0, 0)),
            pl.BlockSpec((1, MLSTM_HEADS, MLSTM_DK, MLSTM_DV), per_b(0, 0, 0)),
            pl.BlockSpec((1, MLSTM_HEADS, MLSTM_DK), per_b(0, 0)),
            pl.BlockSpec((1, MLSTM_HEADS, LANES), per_b(0, 0)),
            pl.BlockSpec((1, RET_HEADS * RET_D), lambda i, c: (0, 0)),
            pl.BlockSpec((1, MLSTM_HEADS * MLSTM_DV), lambda i, c: (0, 0)),
        ],
        out_specs=[
            pl.BlockSpec((1, CHUNK, D_MODEL), lambda i, c: (i, c, 0)),
            pl.BlockSpec((1, RET_HEADS, RET_D, RET_D), per_b(0, 0, 0)),
            pl.BlockSpec((1, MLSTM_HEADS, MLSTM_DK, MLSTM_DV), per_b(0, 0, 0)),
            pl.BlockSpec((1, MLSTM_HEADS, MLSTM_DK), per_b(0, 0)),
            pl.BlockSpec((1, MLSTM_HEADS, LANES), per_b(0, 0)),
        ],
        out_shape=[
            jax.ShapeDtypeStruct((b, l, D_MODEL), BF16),
            jax.ShapeDtypeStruct((b, RET_HEADS, RET_D, RET_D), F32),
            jax.ShapeDtypeStruct((b, MLSTM_HEADS, MLSTM_DK, MLSTM_DV), F32),
            jax.ShapeDtypeStruct((b, MLSTM_HEADS, MLSTM_DK), F32),
            jax.ShapeDtypeStruct((b, MLSTM_HEADS, LANES), F32),
        ],
        compiler_params=_params(("arbitrary", "arbitrary")),
        name="mixers",
    )(z, gz, bias, cosf, sinf, s0, c0, n0, m0, g_ret, g_ml)


def _out_proj_kernel(a_ref, w_ref, r_ref, o_ref):
    o_ref[...] = r_ref[...] + jnp.dot(a_ref[...], w_ref[...], preferred_element_type=F32)


def _out_proj(a, w, r, tm=512):
    t = a.shape[0]
    return pl.pallas_call(
        _out_proj_kernel,
        grid=(t // tm,),
        in_specs=[
            pl.BlockSpec((tm, D_MODEL), lambda i: (i, 0)),
            pl.BlockSpec((D_MODEL, D_MODEL), lambda i: (0, 0)),
            pl.BlockSpec((tm, D_MODEL), lambda i: (i, 0)),
        ],
        out_specs=pl.BlockSpec((tm, D_MODEL), lambda i: (i, 0)),
        out_shape=jax.ShapeDtypeStruct((t, D_MODEL), F32),
        compiler_params=_params(("arbitrary",)),
        name="out_proj",
    )(a, w, r)


def _top16_steps(x_ref, payload_ref, val_ref, idx_ref):
    rows = x_ref.shape[0]
    for r in range(PEER_TOPK):
        x = x_ref[...]
        iota = lax.broadcasted_iota(jnp.int32, x.shape, 0)
        m = jnp.max(x, axis=0, keepdims=True)
        am = jnp.min(jnp.where(x == m, iota, rows), axis=0, keepdims=True)
        sel = iota == am
        val_ref[r:r + 1, :] = m
        if payload_ref is None:
            idx_ref[r:r + 1, :] = am
        else:
            idx_ref[r:r + 1, :] = jnp.max(jnp.where(sel, payload_ref[...], -1), axis=0, keepdims=True)
        x_ref[...] = jnp.where(sel, -jnp.inf, x)
        yield


ROUTE_SCRATCH = [
    pltpu.VMEM((N_KEYS, LANES), F32),
    pltpu.VMEM((N_KEYS, LANES), F32),
    pltpu.VMEM((PEER_TOPK, LANES), F32),
    pltpu.VMEM((PEER_TOPK, LANES), jnp.int32),
    pltpu.VMEM((PEER_TOPK, LANES), F32),
    pltpu.VMEM((PEER_TOPK, LANES), jnp.int32),
    pltpu.VMEM((CAND_ROWS, LANES), F32),
    pltpu.VMEM((CAND_ROWS, LANES), jnp.int32),
    pltpu.VMEM((PEER_TOPK, LANES), F32),
    pltpu.VMEM((PEER_TOPK, LANES), jnp.int32),
]


def _route_unit_steps(q1, q2, k1_ref, k2_ref, work, emit):
    s1_scr, s2_scr, t1_scr, i1_scr, t2_scr, i2_scr, cand_scr, cidx_scr, sc_scr, e_scr = work
    s1_scr[...] = _mm_nt(k1_ref[...], q1)
    s2_scr[...] = _mm_nt(k2_ref[...], q2)
    yield
    for _ in zip(_top16_steps(s1_scr, None, t1_scr, i1_scr), _top16_steps(s2_scr, None, t2_scr, i2_scr)):
        yield
    cand_scr[0:PEER_TOPK, :] = t1_scr[0:1, :] + t2_scr[...]
    cidx_scr[0:PEER_TOPK, :] = i1_scr[0:1, :] * N_KEYS + i2_scr[...]
    t2 = t2_scr[0:SUBLANES, :]
    i2 = i2_scr[0:SUBLANES, :]
    sub_iota = lax.broadcasted_iota(jnp.int32, (SUBLANES, LANES), 0)
    for a in range(1, SUBLANES):
        lo = PEER_TOPK + (a - 1) * SUBLANES
        cand_scr[lo:lo + SUBLANES, :] = jnp.where(sub_iota < PEER_TOPK // (a + 1), t1_scr[a:a + 1, :] + t2, -jnp.inf)
        cidx_scr[lo:lo + SUBLANES, :] = i1_scr[a:a + 1, :] * N_KEYS + i2
    cand_scr[CAND_ROWS - SUBLANES:CAND_ROWS, :] = t1_scr[SUBLANES:PEER_TOPK, :] + t2_scr[0:1, :]
    cidx_scr[CAND_ROWS - SUBLANES:CAND_ROWS, :] = i1_scr[SUBLANES:PEER_TOPK, :] * N_KEYS + i2_scr[0:1, :]
    yield
    yield from _top16_steps(cand_scr, cidx_scr, sc_scr, e_scr)
    sc = sc_scr[...]
    p = jnp.exp(sc - sc[0:1, :])
    emit(e_scr[...], p / jnp.sum(p, axis=0, keepdims=True))


PEER_TB = 512
PEER_SLOTS = 8
PEER_LEAD = PEER_SLOTS - 1
PEER_SUBS = PEER_TB // LANES
ROUTE_UNITS = PEER_HEADS * PEER_SUBS
PEER_GROUP = PEER_TB // ROUTE_UNITS
assert PEER_GROUP % PEER_SLOTS == 0
ROUTE_STEPS = 2 * PEER_TOPK + 3
K_TILE = 256
SIDE_AFTER = (4, 10, 15)
assert len(SIDE_AFTER) * PEER_GROUP >= ROUTE_STEPS
HALF_D = D_MODEL // 2
PACK_TILES = HALF_D // LANES
INV_SQRT2 = 0.7071067811865476


def _pack_bf16_pairs(a):
    bits = lax.bitcast_convert_type(a.astype(BF16), jnp.uint16).astype(jnp.uint32)
    return bits[:, :HALF_D] | (bits[:, HALF_D:] << 16)


def _peer_kernel(hn_ref, wq_ref, k1_ref, k2_ref, h_ref, g_ref, uv_hbm, o_ref,
                 idx_s, idx_v, gate_scr, et_scr, gt_scr, c_scr, cn_scr, q_scr, acc_scr, *scratch):
    work = scratch[:len(ROUTE_SCRATCH)]
    rows = scratch[len(ROUTE_SCRATCH):len(ROUTE_SCRATCH) + PEER_SLOTS]
    idx_sem, row_sems = scratch[len(ROUTE_SCRATCH) + PEER_SLOTS:]
    tb = h_ref.shape[0]

    def route(q_ref, u):
        if isinstance(u, int):
            sub, hd = u % PEER_SUBS, u // PEER_SUBS
            lo = hd * PEER_TOPK
        else:
            sub = jnp.bitwise_and(u, PEER_SUBS - 1)
            hd = jnp.right_shift(u, PEER_SUBS.bit_length() - 1)
            lo = pl.multiple_of(hd * PEER_TOPK, PEER_TOPK)

        def emit(e, gate):
            et_scr[sub, pl.ds(lo, PEER_TOPK), :] = e
            gt_scr[sub, pl.ds(lo, PEER_TOPK), :] = gate

        return _route_unit_steps(q_ref[sub, 2 * hd], q_ref[sub, 2 * hd + 1], k1_ref, k2_ref, work, emit)

    def finish(steps):
        for _ in steps:
            pass

    def query_block(j):
        for sub in range(PEER_SUBS):
            q_scr[sub, j] = jnp.dot(cn_scr[sub * LANES:(sub + 1) * LANES, :], wq_ref[j],
                                    preferred_element_type=F32).astype(BF16)

    def query_steps(i):
        if isinstance(i, int):
            hd, jb, half = min(i // PEER_SUBS + 1, PEER_HEADS - 1), (i % PEER_SUBS) // 2, i % 2
            r0 = half * 2 * LANES
        else:
            hd = jnp.minimum(jnp.right_shift(i, PEER_SUBS.bit_length() - 1) + 1, PEER_HEADS - 1)
            jb = jnp.right_shift(jnp.bitwise_and(i, PEER_SUBS - 1), 1)
            half = jnp.bitwise_and(i, 1)
            r0 = pl.multiple_of(half * 2 * LANES, 2 * LANES)
        j = 2 * hd + jb
        for kt in range(D_MODEL // K_TILE):
            part = jnp.dot(cn_scr[pl.ds(r0, 2 * LANES), kt * K_TILE:(kt + 1) * K_TILE],
                           wq_ref[j, kt * K_TILE:(kt + 1) * K_TILE, :], preferred_element_type=F32)
            acc_scr[...] = part if kt == 0 else acc_scr[...] + part
            yield
        q_scr[2 * half, j] = acc_scr[0:LANES, :].astype(BF16)
        q_scr[2 * half + 1, j] = acc_scr[LANES:2 * LANES, :].astype(BF16)
        yield

    def merged(main, extra, every):
        n = 0
        for _ in main:
            n += 1
            if n % every == 0:
                next(extra, None)
            yield
        for _ in extra:
            yield

    ids_copy = pltpu.make_async_copy(idx_v, idx_s, idx_sem)

    def publish():
        for sub in range(PEER_SUBS):
            idx_v[sub * LANES:(sub + 1) * LANES, :] = et_scr[sub].T
            gate_scr[sub * LANES:(sub + 1) * LANES, :] = gt_scr[sub].T
        ids_copy.start()

    @pl.when(pl.program_id(0) == 0)
    def _():
        cn_scr[...] = _rms(h_ref[...], g_ref[...]).astype(BF16)

        def qbody(j, carry):
            query_block(j)
            return carry

        lax.fori_loop(0, 2 * PEER_HEADS, qbody, 0)

        def body(u, carry):
            finish(route(q_scr, u))
            return carry

        lax.fori_loop(0, ROUTE_UNITS, body, 0)
        publish()

    c_scr[...] = _rms(h_ref[...], g_ref[...])
    cn_scr[...] = _rms(hn_ref[...], g_ref[...]).astype(BF16)
    query_block(0)
    query_block(1)
    ids_copy.wait()

    def issue(t, slot, k0, k1):
        for k in range(k0, k1):
            e = idx_s[t, k]
            pltpu.make_async_copy(uv_hbm.at[e], rows[slot].at[pl.ds(k, 1), :], row_sems.at[slot]).start(priority=k % 2)

    def wait(slot):
        pltpu.make_async_copy(uv_hbm.at[pl.ds(0, PEER_K), 0], rows[slot], row_sems.at[slot]).wait()

    def unpack(words):
        lo = lax.bitcast_convert_type(jnp.left_shift(words, jnp.uint32(16)), F32)
        hi = lax.bitcast_convert_type(jnp.bitwise_and(words, jnp.uint32(0xFFFF0000)), F32)
        return lo, hi

    eye = lax.broadcasted_iota(jnp.int32, (PEER_K, LANES), 0) == lax.broadcasted_iota(jnp.int32, (PEER_K, LANES), 1)
    per_piece = PEER_K // (2 * PACK_TILES)

    def token(t, slot, prefetch, side_steps):
        wait(slot)
        ahead_slot = (slot + PEER_LEAD) % PEER_SLOTS
        x = c_scr[pl.ds(t, 1), :]
        acc = None
        for j in range(PACK_TILES):
            if prefetch:
                issue(t + PEER_LEAD, ahead_slot, j * per_piece, (j + 1) * per_piece)
            lo, hi = unpack(rows[slot][:, j * LANES:(j + 1) * LANES])
            p = lo * x[:, j * LANES:(j + 1) * LANES] + hi * x[:, HALF_D + j * LANES:HALF_D + (j + 1) * LANES]
            acc = p if acc is None else acc + p
            if j in SIDE_AFTER:
                next(side_steps, None)
        act = jnp.sum(acc, axis=1, keepdims=True)
        gate_col = jnp.sum(jnp.where(eye, gate_scr[pl.ds(t, 1), :], 0.0), axis=1, keepdims=True)
        coef = gate_col * (0.5 * act * (1.0 + lax.erf(act * INV_SQRT2)))
        mixed_lo, mixed_hi = [], []
        for j in range(PACK_TILES):
            if prefetch:
                issue(t + PEER_LEAD, ahead_slot, (PACK_TILES + j) * per_piece, (PACK_TILES + j + 1) * per_piece)
            lo, hi = unpack(rows[slot][:, HALF_D + j * LANES:HALF_D + (j + 1) * LANES])
            mixed_lo.append(jnp.sum(lo * coef, axis=0, keepdims=True))
            mixed_hi.append(jnp.sum(hi * coef, axis=0, keepdims=True))
            if PACK_TILES + j in SIDE_AFTER:
                next(side_steps, None)
        o_ref[pl.ds(t, 1), :] = h_ref[pl.ds(t, 1), :] + jnp.concatenate(mixed_lo + mixed_hi, axis=1)

    for s in range(PEER_LEAD):
        issue(s, s, 0, PEER_K)

    def group(i, carry):
        steps = merged(route(q_scr, i), query_steps(i), 3)
        for s in range(PEER_GROUP):
            token(i * PEER_GROUP + s, s % PEER_SLOTS, True, steps)
        finish(steps)
        return carry

    lax.fori_loop(0, ROUTE_UNITS - 1, group, 0)
    last = tb - PEER_GROUP
    steps = merged(route(q_scr, ROUTE_UNITS - 1), query_steps(ROUTE_UNITS - 1), 3)
    for s in range(PEER_GROUP):
        token(last + s, s % PEER_SLOTS, s + PEER_LEAD < PEER_GROUP, steps)
    finish(steps)
    pl.when(pl.program_id(0) + 1 < pl.num_programs(0))(publish)


def _peer(wq3, k1, k2, h, g, uv):
    t = h.shape[0]
    tb = PEER_TB
    steps = t // tb
    return pl.pallas_call(
        _peer_kernel,
        grid=(steps,),
        in_specs=[
            pl.BlockSpec((tb, D_MODEL), lambda i: (jnp.minimum(i + 1, steps - 1), 0)),
            pl.BlockSpec((2 * PEER_HEADS, D_MODEL, N_KEYS), lambda i: (0, 0, 0), pipeline_mode=pl.Buffered(1)),
            pl.BlockSpec((N_KEYS, N_KEYS), lambda i: (0, 0)),
            pl.BlockSpec((N_KEYS, N_KEYS), lambda i: (0, 0)),
            pl.BlockSpec((tb, D_MODEL), lambda i: (i, 0)),
            pl.BlockSpec((1, D_MODEL), lambda i: (0, 0)),
            pl.BlockSpec(memory_space=pl.ANY),
        ],
        out_specs=pl.BlockSpec((tb, D_MODEL), lambda i: (i, 0)),
        out_shape=jax.ShapeDtypeStruct((t, D_MODEL), F32),
        scratch_shapes=[
            pltpu.SMEM((tb, PEER_K), jnp.int32),
            pltpu.VMEM((tb, PEER_K), jnp.int32),
            pltpu.VMEM((tb, PEER_K), F32),
            pltpu.VMEM((PEER_SUBS, PEER_K, LANES), jnp.int32),
            pltpu.VMEM((PEER_SUBS, PEER_K, LANES), F32),
            pltpu.VMEM((tb, D_MODEL), F32),
            pltpu.VMEM((tb, D_MODEL), BF16),
            pltpu.VMEM((PEER_SUBS, 2 * PEER_HEADS, LANES, N_KEYS), BF16),
            pltpu.VMEM((2 * LANES, N_KEYS), F32),
        ] + ROUTE_SCRATCH + [pltpu.VMEM((PEER_K, D_MODEL), jnp.uint32)] * PEER_SLOTS + [
            pltpu.SemaphoreType.DMA(()),
            pltpu.SemaphoreType.DMA((PEER_SLOTS,)),
        ],
        compiler_params=_params(("arbitrary",)),
        name="peer",
    )(h, wq3, k1, k2, h, g, uv)


def _ple_kernel(h_ref, p_ref, gple_ref, wg_ref, wp_ref, gfin_ref, y_ref):
    h = h_ref[...]
    e = _rms(h, gple_ref[...]).astype(BF16)
    gate = _sigmoid(jnp.dot(e, wg_ref[...], preferred_element_type=F32))
    proj = jnp.dot(p_ref[...].astype(BF16), wp_ref[...], preferred_element_type=F32)
    y_ref[...] = _rms(h + gate * proj, gfin_ref[...])


def _ple(h, p, g_ple, wg, wp, g_fin, tm=512):
    t = h.shape[0]
    return pl.pallas_call(
        _ple_kernel,
        grid=(t // tm,),
        in_specs=[
            pl.BlockSpec((tm, D_MODEL), lambda i: (i, 0)),
            pl.BlockSpec((tm, PLE_DIM), lambda i: (i, 0)),
            pl.BlockSpec((1, D_MODEL), lambda i: (0, 0)),
            pl.BlockSpec((D_MODEL, D_MODEL), lambda i: (0, 0)),
            pl.BlockSpec((PLE_DIM, D_MODEL), lambda i: (0, 0)),
            pl.BlockSpec((1, D_MODEL), lambda i: (0, 0)),
        ],
        out_specs=pl.BlockSpec((tm, D_MODEL), lambda i: (i, 0)),
        out_shape=jax.ShapeDtypeStruct((t, D_MODEL), F32),
        compiler_params=_params(("arbitrary",)),
        name="ple",
    )(h, p, g_ple, wg, wp, g_fin)


def _rope_tables(pos):
    half = RET_D // 2
    inv = ROPE_BASE ** (-jnp.arange(half, dtype=F32) / half)
    ang = pos[:, None] * inv[None, :]
    cos, sin = jnp.cos(ang), jnp.sin(ang)
    return jnp.concatenate([cos, cos], axis=-1), jnp.concatenate([-sin, sin], axis=-1)


def _stream(x, p, pos, s0, c0, n0, m0, w):
    b, l, _ = x.shape
    t = b * l
    x2 = x.reshape(t, D_MODEL)
    z, gz = _in_proj(x2, w["g_mix"], w["w_in"], w["w_gates"])
    cosf, sinf = _rope_tables(pos)
    m0b = jnp.broadcast_to(m0[:, :, None], (b, MLSTM_HEADS, LANES))
    mix, s_new, c_new, n_new, m_new = _mixers(
        z.reshape(b, l, Z_COLS), gz.reshape(b, l, LANES), w["bias"], cosf, sinf, s0, c0, n0, m0b, w["g_ret"], w["g_ml"])
    h1 = _out_proj(mix.reshape(t, D_MODEL), w["w_out"], x2)
    h2 = _peer(w["w_q"], w["k1"], w["k2"], h1, w["g_ffn"], w["uv"])
    y = _ple(h2, p.reshape(t, PLE_DIM), w["g_ple"], w["w_ple_gate"], w["w_ple_proj"], w["g_final"])
    return y.reshape(b, l, D_MODEL), s_new[None], c_new[None], n_new[None], m_new[None, :, :, 0]


def _prep_weights(g_mix, w_in, b_gates, g_ret, g_mlstm, w_out, g_ffn, w_peer_q, peer_keys1, peer_keys2, peer_u, peer_v,
                  g_ple, w_ple_gate, w_ple_proj, g_final):
    w_in0 = w_in[0]
    return {
        "g_mix": g_mix,
        "w_in": w_in0[:, :Z_COLS].astype(BF16),
        "w_gates": jnp.pad(w_in0[:, Z_COLS:], ((0, 0), (0, LANES - 2 * MLSTM_HEADS))).astype(BF16),
        "bias": jnp.pad(b_gates, ((0, 0), (0, LANES - 2 * MLSTM_HEADS))),
        "g_ret": g_ret,
        "g_ml": g_mlstm,
        "w_out": w_out[0].astype(BF16),
        "g_ffn": g_ffn,
        "w_q": w_peer_q[0].astype(BF16).reshape(D_MODEL, 2 * PEER_HEADS, N_KEYS).transpose(1, 0, 2),
        "k1": peer_keys1[0].astype(BF16),
        "k2": peer_keys2[0].astype(BF16),
        "uv": jnp.concatenate([_pack_bf16_pairs(peer_u[0]), _pack_bf16_pairs(peer_v[0])], axis=1)[:, None, :],
        "g_ple": g_ple,
        "w_ple_gate": w_ple_gate[0].astype(BF16),
        "w_ple_proj": w_ple_proj[0].astype(BF16),
        "g_final": g_final[None, :],
    }


def kernel(x_prompt, x_sample, p_prompt, p_sample, state_ret, state_mlstm_C, state_mlstm_n, state_mlstm_m, g_mix, w_in, b_gates, g_ret, g_mlstm, w_out, g_ffn, w_peer_q, peer_keys1, peer_keys2, peer_u, peer_v, g_ple, w_ple_gate, w_ple_proj, g_final):
    w = _prep_weights(g_mix, w_in, b_gates, g_ret, g_mlstm, w_out, g_ffn, w_peer_q, peer_keys1, peer_keys2, peer_u, peer_v,
                      g_ple, w_ple_gate, w_ple_proj, g_final)
    bp, lp, _ = x_prompt.shape
    bs, ls, _ = x_sample.shape
    zeros = lambda *shape: jnp.zeros(shape, F32)
    y_s, ret_s, c_s, n_s, m_s = _stream(
        x_sample, p_sample[0], PAST_LEN + jnp.arange(ls, dtype=F32),
        state_ret[0], state_mlstm_C[0], state_mlstm_n[0], state_mlstm_m[0], w)
    y_p, ret_p, c_p, n_p, m_p = _stream(
        x_prompt, p_prompt[0], jnp.arange(lp, dtype=F32),
        zeros(bp, RET_HEADS, RET_D, RET_D), zeros(bp, MLSTM_HEADS, MLSTM_DK, MLSTM_DV),
        zeros(bp, MLSTM_HEADS, MLSTM_DK), zeros(bp, MLSTM_HEADS), w)
    return (y_p, y_s, ret_p, c_p, n_p, m_p, ret_s, c_s, n_s, m_s)
```
